```python
import jax
import jax.numpy as jnp
from jax import lax
import numpy as np

D_MODEL = 1024
BATCH = 2
SEQ = 16384
DEPTH = 1
DEC_BATCH = 8
DEC_SEQ = 8192
PAST_LEN = 128

ATTN_PATTERNS = ((128, 1), (512, 4), (2048, 16))
N_GROUPS = 3
ATTN_HEADS = 8
ATTN_HEAD_DIM = 64
ATTN_DIM = ATTN_HEADS * ATTN_HEAD_DIM
ATTN_BLOCK = 64
ROPE_THETA = 10000.0
ML_DIM = D_MODEL
ML_HEADS = 4
ML_HEAD_DIM = ML_DIM // ML_HEADS
ML_CHUNK = 128
CONV_W = 5
ATTN_QKV_COLS = N_GROUPS * 3 * ATTN_DIM
ML_QKV_COLS = 3 * ML_DIM
ML_O_COLS = ML_DIM
ML_GATE_COLS = 4 * ML_HEADS
MERGE_COLS = 2 * D_MODEL
N_IN = ATTN_QKV_COLS + ML_QKV_COLS + ML_O_COLS + ML_GATE_COLS + MERGE_COLS
N_EXPERTS = 32
TOP_K = 4
D_FF = D_MODEL
SWIGLU_LIMIT = 7.0
SWIGLU_ALPHA = 1.702
RMS_EPS = 1e-5
NEG_INF = -1e30

kernel_name = "hybrid_dilated_attn_mlstm_moe_encoder"


def rmsnorm(x, w):
    xf = x.astype(jnp.float32)
    y = xf * lax.rsqrt(jnp.mean(xf * xf, axis=-1, keepdims=True) + RMS_EPS)
    return (y * w.astype(jnp.float32)).astype(x.dtype)


def rope(x):
    s, dh = x.shape[1], x.shape[-1]
    half = dh // 2
    inv_freq = ROPE_THETA ** (-jnp.arange(half, dtype=jnp.float32) / half)
    ang = jnp.arange(s, dtype=jnp.float32)[:, None] * inv_freq[None, :]
    cos = jnp.cos(ang)[None, :, None, :]
    sin = jnp.sin(ang)[None, :, None, :]
    x1, x2 = x[..., :half], x[..., half:]
    return jnp.concatenate([x1 * cos - x2 * sin, x2 * cos + x1 * sin], axis=-1)


def dilated_window_attention(q, k, v, dilation, radius):
    b, s, h, dh = q.shape
    L = s // dilation
    nb = -(-L // ATTN_BLOCK)
    pad = nb * ATTN_BLOCK - L

    def to_sub(t):
        return t.reshape(b, L, dilation, h, dh).transpose(0, 2, 1, 3, 4).reshape(b * dilation, L, h, dh)

    qs, ks, vs = to_sub(q), to_sub(k), to_sub(v)
    qb = jnp.pad(qs, ((0, 0), (0, pad), (0, 0), (0, 0))).reshape(b * dilation, nb, ATTN_BLOCK, h, dh)

    def key_blocks(t):
        tp = jnp.pad(t, ((0, 0), (ATTN_BLOCK, pad + ATTN_BLOCK), (0, 0), (0, 0)))
        tp = tp.reshape(b * dilation, nb + 2, ATTN_BLOCK, h, dh)
        return jnp.concatenate([tp[:, :-2], tp[:, 1:-1], tp[:, 2:]], axis=2)

    kb, vb = key_blocks(ks), key_blocks(vs)
    qpos = jnp.arange(nb)[:, None] * ATTN_BLOCK + jnp.arange(ATTN_BLOCK)[None, :]
    kpos = (jnp.arange(nb)[:, None] - 1) * ATTN_BLOCK + jnp.arange(3 * ATTN_BLOCK)[None, :]
    kp = kpos[:, None, :]
    valid = (jnp.abs(kp - qpos[:, :, None]) <= radius) & (kp >= 0) & (kp < L)

    scores = jnp.einsum('rnqhd,rnkhd->rnhqk', qb, kb) * (dh ** -0.5)
    scores = jnp.where(valid[None, :, None], scores, NEG_INF)
    mx = jnp.max(scores, axis=-1, keepdims=True)
    p = jnp.exp(scores - mx)
    den = jnp.sum(p, axis=-1, keepdims=True)
    out = jnp.einsum('rnhqk,rnkhd->rnqhd', p, vb) / jnp.swapaxes(den, 2, 3)
    lse = jnp.swapaxes((mx + jnp.log(den))[..., 0], 2, 3)

    out = out.reshape(b * dilation, nb * ATTN_BLOCK, h, dh)[:, :L]
    out = out.reshape(b, dilation, L, h, dh).transpose(0, 2, 1, 3, 4).reshape(b, s, h, dh)
    lse = lse.reshape(b * dilation, nb * ATTN_BLOCK, h)[:, :L]
    lse = lse.reshape(b, dilation, L, h).transpose(0, 2, 1, 3).reshape(b, s, h)
    return out, lse


def attention_mixer(qkv):
    b, s, _ = qkv.shape
    qkv = qkv.reshape(b, s, N_GROUPS, 3, ATTN_HEADS, ATTN_HEAD_DIM)
    outs, lses = [], []
    for g, (window, dilation) in enumerate(ATTN_PATTERNS):
        q = rope(qkv[:, :, g, 0])
        k = rope(qkv[:, :, g, 1])
        v = qkv[:, :, g, 2]
        o, l = dilated_window_attention(q, k, v, dilation, window // (2 * dilation))
        outs.append(o)
        lses.append(l)
    w = jax.nn.softmax(jnp.stack(lses, axis=0), axis=0)
    out = jnp.einsum('gbsh,gbshd->bshd', w, jnp.stack(outs, axis=0))
    return out.reshape(b, s, ATTN_DIM)


def centred_depthwise_conv(x, w, bias):
    c = x.shape[-1]
    y = lax.conv_general_dilated(
        x, w.astype(x.dtype)[:, None, :], window_strides=(1,),
        padding=((CONV_W // 2, CONV_W // 2),),
        dimension_numbers=('NWC', 'WIO', 'NWC'), feature_group_count=c)
    return y + bias


def mlstm_scan(q, k, v, log_i, log_f):
    b, h, s, dh = q.shape
    nc = s // ML_CHUNK

    def chunks(t):
        return jnp.moveaxis(t.reshape((b, h, nc, ML_CHUNK) + t.shape[3:]), 2, 0)

    xs = (chunks(q), chunks(k), chunks(v), chunks(log_i), chunks(log_f))
    lower = jnp.tril(jnp.ones((ML_CHUNK, ML_CHUNK), dtype=bool))

    def step(carry, inp):
        C, n, m = carry
        qj, kj, vj, ij, fj = inp
        bcum = jnp.cumsum(fj, axis=-1)
        dmat = bcum[..., :, None] - bcum[..., None, :] + ij[..., None, :]
        dmat = jnp.where(lower, dmat, NEG_INF)
        inter = bcum + m[..., None]
        m_t = jnp.maximum(inter, jnp.max(dmat, axis=-1))
        w_intra = jnp.exp(dmat - m_t[..., None])
        w_inter = jnp.exp(inter - m_t)
        qk = jnp.einsum('bhtd,bhsd->bhts', qj, kj) * w_intra
        num = w_inter[..., None] * jnp.einsum('bhtd,bhde->bhte', qj, C) + jnp.einsum('bhts,bhse->bhte', qk, vj)
        nq = w_inter * jnp.einsum('bhtd,bhd->bht', qj, n) + jnp.sum(qk, axis=-1)
        den = jnp.maximum(jnp.abs(nq), jnp.exp(-m_t))
        h_out = num / den[..., None]
        b_last = bcum[..., -1]
        log_w = b_last[..., None] - bcum + ij
        m_new = jnp.maximum(b_last + m, jnp.max(log_w, axis=-1))
        decay = jnp.exp(b_last + m - m_new)
        wk = kj * jnp.exp(log_w - m_new[..., None])[..., None]
        C_new = decay[..., None, None] * C + jnp.einsum('bhsd,bhse->bhde', wk, vj)
        n_new = decay[..., None] * n + jnp.sum(wk, axis=2)
        return (C_new, n_new, m_new), h_out

    init = (jnp.zeros((b, h, dh, dh), jnp.float32), jnp.zeros((b, h, dh), jnp.float32),
            jnp.zeros((b, h), jnp.float32))
    _, hs = lax.scan(step, init, xs)
    return jnp.moveaxis(hs, 0, 2).reshape(b, h, s, dh)


def mlstm_mixer(ml_qkv, ml_o, ml_gates, conv_w, conv_b, gate_b):
    b, s, _ = ml_qkv.shape
    qk = jax.nn.silu(centred_depthwise_conv(ml_qkv[..., :2 * ML_DIM], conv_w, conv_b))
    v = ml_qkv[..., 2 * ML_DIM:]

    def heads(t):
        return t.reshape(b, s, ML_HEADS, ML_HEAD_DIM).transpose(0, 2, 1, 3)

    q = heads(qk[..., :ML_DIM])
    k = heads(qk[..., ML_DIM:]) * (ML_HEAD_DIM ** -0.5)
    v = heads(v)
    g = (ml_gates.reshape(b, s, 4, ML_HEADS) + gate_b).transpose(2, 0, 3, 1)
    h_fwd = mlstm_scan(q, k, v, g[0], jax.nn.log_sigmoid(g[1]))
    flip = lambda t: jnp.flip(t, axis=2)
    h_bwd = flip(mlstm_scan(flip(q), flip(k), flip(v), flip(g[2]), flip(jax.nn.log_sigmoid(g[3]))))
    hsum = (h_fwd + h_bwd).transpose(0, 2, 1, 3).reshape(b, s, ML_DIM)
    return jax.nn.sigmoid(ml_o) * hsum


def moe_ffn(x, router_w, router_b, w_gu, b_gu, w_down, b_down):
    b, s, d = x.shape
    xt = x.reshape(b * s, d)
    logits = (xt @ router_w + router_b).astype(jnp.float32)
    top_vals, top_idx = lax.top_k(logits, TOP_K)
    top_w = jax.nn.softmax(top_vals, axis=-1)
    combine = jnp.sum(jax.nn.one_hot(top_idx, N_EXPERTS, dtype=jnp.float32) * top_w[..., None], axis=1)

    def expert(acc, p):
        w1, b1, w2, b2, c = p
        gu = xt @ w1 + b1
        gate, up = gu[..., :D_FF], gu[..., D_FF:]
        gate = jnp.minimum(gate, SWIGLU_LIMIT)
        up = jnp.clip(up, -SWIGLU_LIMIT, SWIGLU_LIMIT)
        hid = (up + 1.0) * gate * jax.nn.sigmoid(SWIGLU_ALPHA * gate)
        return acc + c[:, None] * (hid @ w2 + b2), None

    acc, _ = lax.scan(expert, jnp.zeros((b * s, d), jnp.float32), (w_gu, b_gu, w_down, b_down, combine.T))
    return acc.reshape(b, s, d).astype(x.dtype)


def encoder_layer(x, norm_mix_w, w_in, mlstm_conv_w, mlstm_conv_b, mlstm_gate_b, w_proj_attn,
                  w_proj_mlstm, w_out, norm_ffn_w, router_w, router_b, expert_w_gu, expert_b_gu,
                  expert_w_down, expert_b_down):
    h = rmsnorm(x, norm_mix_w)
    proj = (h @ w_in).astype(jnp.float32)
    c0 = ATTN_QKV_COLS
    c1 = c0 + ML_QKV_COLS
    c2 = c1 + ML_O_COLS
    c3 = c2 + ML_GATE_COLS
    attn_qkv, ml_qkv, ml_o, ml_gates, merge = (proj[..., :c0], proj[..., c0:c1], proj[..., c1:c2],
                                               proj[..., c2:c3], proj[..., c3:])
    y_attn = attention_mixer(attn_qkv) @ w_proj_attn
    y_ml = mlstm_mixer(ml_qkv, ml_o, ml_gates, mlstm_conv_w, mlstm_conv_b, mlstm_gate_b) @ w_proj_mlstm
    gates = jax.nn.sigmoid(merge)
    mixed = (gates[..., :D_MODEL] * y_attn + gates[..., D_MODEL:] * y_ml) @ w_out
    x = x + mixed.astype(x.dtype)
    x = x + moe_ffn(rmsnorm(x, norm_ffn_w), router_w, router_b, expert_w_gu, expert_b_gu,
                    expert_w_down, expert_b_down)
    return x


def setup_inputs(seed: int = 0) -> dict:
    key = jax.random.key(seed)
    ks = jax.random.split(key, 20)
    f32 = jnp.float32
    nrm = lambda k, shape, scale: jax.random.normal(k, shape, f32) * scale
    forget_bias = jnp.linspace(3.0, 6.0, ML_HEADS, dtype=f32)
    gate_mask = jnp.array([0.0, 1.0, 0.0, 1.0], f32)[:, None]
    return {
        "x_prompt": nrm(ks[0], (BATCH, SEQ, D_MODEL), 1.0),
        "x_sample": nrm(ks[1], (DEC_BATCH, DEC_SEQ, D_MODEL), 1.0),
        "norm_mix_w": 1.0 + nrm(ks[2], (DEPTH, D_MODEL), 0.02),
        "w_in": nrm(ks[3], (DEPTH, D_MODEL, N_IN), D_MODEL ** -0.5),
        "mlstm_conv_w": nrm(ks[4], (DEPTH, CONV_W, 2 * ML_DIM), CONV_W ** -0.5),
        "mlstm_conv_b": nrm(ks[5], (DEPTH, 2 * ML_DIM), 0.02),
        "mlstm_gate_b": nrm(ks[6], (DEPTH, 4, ML_HEADS), 0.1) + gate_mask * forget_bias[None, :],
        "w_proj_attn": nrm(ks[7], (DEPTH, ATTN_DIM, D_MODEL), ATTN_DIM ** -0.5),
        "w_proj_mlstm": nrm(ks[8], (DEPTH, ML_DIM, D_MODEL), ML_DIM ** -0.5),
        "w_out": nrm(ks[9], (DEPTH, D_MODEL, D_MODEL), D_MODEL ** -0.5),
        "norm_ffn_w": 1.0 + nrm(ks[10], (DEPTH, D_MODEL), 0.02),
        "router_w": nrm(ks[11], (DEPTH, D_MODEL, N_EXPERTS), D_MODEL ** -0.5),
        "router_b": nrm(ks[12], (DEPTH, N_EXPERTS), 0.01),
        "expert_w_gu": nrm(ks[13], (DEPTH, N_EXPERTS, D_MODEL, 2 * D_FF), D_MODEL ** -0.5),
        "expert_b_gu": nrm(ks[14], (DEPTH, N_EXPERTS, 2 * D_FF), 0.02),
        "expert_w_down": nrm(ks[15], (DEPTH, N_EXPERTS, D_FF, D_MODEL), D_FF ** -0.5),
        "expert_b_down": nrm(ks[16], (DEPTH, N_EXPERTS, D_MODEL), 0.02),
        "norm_final_w": 1.0 + nrm(ks[17], (D_MODEL,), 0.02),
    }


def reference(x_prompt, x_sample, norm_mix_w, w_in, mlstm_conv_w, mlstm_conv_b, mlstm_gate_b,
              w_proj_attn, w_proj_mlstm, w_out, norm_ffn_w, router_w, router_b, expert_w_gu,
              expert_b_gu, expert_w_down, expert_b_down, norm_final_w):
    def run(x):
        for l in range(DEPTH):
            x = encoder_layer(x, norm_mix_w[l], w_in[l], mlstm_conv_w[l], mlstm_conv_b[l], mlstm_gate_b[l],
                              w_proj_attn[l], w_proj_mlstm[l], w_out[l], norm_ffn_w[l], router_w[l],
                              router_b[l], expert_w_gu[l], expert_b_gu[l], expert_w_down[l],
                              expert_b_down[l])
        return rmsnorm(x, norm_final_w)

    y_prompt = run(x_prompt)
    y_sample = run(x_sample)
    return (y_prompt, y_sample)
```

```python
import functools

import jax
import jax.numpy as jnp
from jax import lax
from jax.experimental import pallas as pl
from jax.experimental.pallas import tpu as pltpu

F32 = jnp.float32
BF16 = jnp.bfloat16

D_MODEL = 1024
ATTN_PATTERNS = ((128, 1), (512, 4), (2048, 16))
N_GROUPS = 3
ATTN_HEADS = 8
ATTN_HEAD_DIM = 64
ATTN_DIM = ATTN_HEADS * ATTN_HEAD_DIM
ATTN_BLOCK = 64
ROPE_THETA = 10000.0
ML_DIM = D_MODEL
ML_HEADS = 4
ML_HEAD_DIM = ML_DIM // ML_HEADS
ML_CHUNK = 128
CONV_W = 5
ATTN_QKV_COLS = N_GROUPS * 3 * ATTN_DIM
ML_QKV_COLS = 3 * ML_DIM
ML_O_COLS = ML_DIM
ML_GATE_COLS = 4 * ML_HEADS
MERGE_COLS = 2 * D_MODEL
N_EXPERTS = 32
TOP_K = 4
D_FF = D_MODEL
SWIGLU_LIMIT = 7.0
SWIGLU_ALPHA = 1.702
RMS_EPS = 1e-5
NEG_INF = -1e30

LANES = 128
COL_TILE = 512
N_MAIN = ML_QKV_COLS + ML_O_COLS + MERGE_COLS + ATTN_QKV_COLS
N_COL_TILES = N_MAIN // COL_TILE
ATTN_TILE0 = (ML_QKV_COLS + ML_O_COLS + MERGE_COLS) // COL_TILE
ML_EXT = ML_HEAD_DIM + LANES


def _params(sem, vmem_mb):
    return pltpu.CompilerParams(dimension_semantics=sem, vmem_limit_bytes=vmem_mb * 1024 * 1024)


def _sigmoid(x):
    return 1.0 / (1.0 + jnp.exp(-x))


def _inproj_kernel(x_ref, nw_ref, w_ref, wg_ref, cos_ref, sin_ref, out_ref, gates_ref, h_scr):
    j = pl.program_id(1)

    @pl.when(j == 0)
    def _():
        x = x_ref[...]
        ms = jnp.mean(x * x, axis=-1, keepdims=True)
        h = (x * lax.rsqrt(ms + RMS_EPS) * nw_ref[...]).astype(BF16)
        h_scr[...] = h
        gates_ref[...] = jnp.dot(h, wg_ref[...], preferred_element_type=F32)

    acc = jnp.dot(h_scr[...], w_ref[...], preferred_element_type=F32)
    jj = j - ATTN_TILE0
    is_rope = jnp.logical_and(jj >= 0, jj % 3 != 2)

    @pl.when(is_rope)
    def _():
        scale = jnp.where(jj % 3 == 0, ATTN_HEAD_DIM ** -0.5, 1.0).astype(F32)
        c = jnp.concatenate([cos_ref[...]] * (COL_TILE // LANES), axis=1)
        s = jnp.concatenate([sin_ref[...]] * (COL_TILE // LANES), axis=1)
        lane = lax.broadcasted_iota(jnp.int32, acc.shape, 1)
        first = (lane % ATTN_HEAD_DIM) < (ATTN_HEAD_DIM // 2)
        half = ATTN_HEAD_DIM // 2
        sw = jnp.where(first, pltpu.roll(acc, COL_TILE - half, 1), pltpu.roll(acc, half, 1))
        out_ref[...] = ((acc * c + sw * s) * scale).astype(BF16)

    @pl.when(jnp.logical_not(is_rope))
    def _():
        out_ref[...] = acc.astype(BF16)


def _inproj(x2d, nw, w_main, w_gate, cos_t, sin_t, seq):
    t = x2d.shape[0]
    tm = min(2048, seq)
    tiles_per_seq = seq // tm
    return pl.pallas_call(
        _inproj_kernel,
        grid=(t // tm, N_COL_TILES),
        in_specs=[
            pl.BlockSpec((tm, D_MODEL), lambda i, j: (i, 0)),
            pl.BlockSpec((1, D_MODEL), lambda i, j: (0, 0)),
            pl.BlockSpec((D_MODEL, COL_TILE), lambda i, j: (0, j)),
            pl.BlockSpec((D_MODEL, LANES), lambda i, j: (0, 0)),
            pl.BlockSpec((tm, LANES), lambda i, j: (i % tiles_per_seq, 0)),
            pl.BlockSpec((tm, LANES), lambda i, j: (i % tiles_per_seq, 0)),
        ],
        out_specs=[
            pl.BlockSpec((tm, COL_TILE), lambda i, j: (i, j)),
            pl.BlockSpec((tm, LANES), lambda i, j: (i, 0)),
        ],
        out_shape=[
            jax.ShapeDtypeStruct((t, N_MAIN), BF16),
            jax.ShapeDtypeStruct((t, LANES), F32),
        ],
        scratch_shapes=[pltpu.VMEM((tm, D_MODEL), BF16)],
        compiler_params=_params(("arbitrary", "arbitrary"), 56),
        name="inproj",
    )(x2d, nw, w_main, w_gate, cos_t, sin_t)


def _attn_kernel(*refs, sub_len, qt, has_prev):
    if has_prev:
        (q_ref, kp_ref, kc_ref, kn_ref, vp_ref, vc_ref, vn_ref, op_ref, lp_ref, o_ref, l_ref) = refs
    else:
        (q_ref, kp_ref, kc_ref, kn_ref, vp_ref, vc_ref, vn_ref, o_ref, l_ref) = refs
    n = pl.program_id(2)
    kt = qt + 2 * ATTN_BLOCK
    q = q_ref[0]
    k = jnp.concatenate([kp_ref[0], kc_ref[0], kn_ref[0]], axis=0)
    v = jnp.concatenate([vp_ref[0], vc_ref[0], vn_ref[0]], axis=0)
    qpos = n * qt + lax.broadcasted_iota(jnp.int32, (qt, kt), 0)
    kpos = n * qt - ATTN_BLOCK + lax.broadcasted_iota(jnp.int32, (qt, kt), 1)
    valid = (jnp.abs(kpos - qpos) <= ATTN_BLOCK) & (kpos >= 0) & (kpos < sub_len)
    lane = lax.broadcasted_iota(jnp.int32, (qt, LANES), 1)
    lo = lane < ATTN_HEAD_DIM
    lse_out = jnp.zeros((qt, LANES), F32)
    for p in range(ATTN_HEADS // 2):
        sl = slice(p * LANES, (p + 1) * LANES)
        qp, kp_, vp_ = q[:, sl], k[:, sl], v[:, sl]
        w_prev, w_cur, pvs = [], [], []
        for half in range(2):
            head = 2 * p + half
            hm = lo if half == 0 else jnp.logical_not(lo)
            qm = jnp.where(hm, qp, jnp.zeros_like(qp))
            s = lax.dot_general(qm, kp_, (((1,), (1,)), ((), ())), preferred_element_type=F32)
            s = jnp.where(valid, s, NEG_INF)
            mx = jnp.max(s, axis=-1, keepdims=True)
            e = jnp.exp(s - mx)
            den = jnp.sum(e, axis=-1, keepdims=True)
            pvs.append(jnp.dot(e.astype(BF16), vp_, preferred_element_type=F32))
            lse = mx + jnp.log(den)
            if has_prev:
                lpv = lp_ref[0][:, head:head + 1]
                m = jnp.maximum(lpv, lse)
                a = jnp.exp(lpv - m)
                b = jnp.exp(lse - m)
                tot = a + b
                w_prev.append(a / tot)
                w_cur.append(b / (tot * den))
                lse = m + jnp.log(tot)
            else:
                w_cur.append(1.0 / den)
            lse_out = jnp.where(lane == head, lse, lse_out)
        cur = jnp.where(lo, pvs[0] * w_cur[0], pvs[1] * w_cur[1])
        if has_prev:
            cur = cur + op_ref[0][:, sl].astype(F32) * jnp.where(lo, w_prev[0], w_prev[1])
        o_ref[0, :, sl] = cur.astype(BF16)
    l_ref[0] = lse_out


def _attn_group(proj3, g, dilation, prev):
    b, s, _ = proj3.shape
    sub_len = s // dilation
    qt = min(256, sub_len)
    nblk = sub_len // ATTN_BLOCK
    qb = qt // ATTN_BLOCK
    pv = proj3.reshape(b, sub_len, dilation * N_MAIN)
    col = ATTN_TILE0 + 3 * g

    def cur(which):
        return pl.BlockSpec((1, qt, COL_TILE), lambda bi, r, n: (bi, n, r * N_COL_TILES + col + which))

    def before(which):
        return pl.BlockSpec((1, ATTN_BLOCK, COL_TILE),
                            lambda bi, r, n: (bi, jnp.maximum(n * qb - 1, 0), r * N_COL_TILES + col + which))

    def after(which):
        return pl.BlockSpec((1, ATTN_BLOCK, COL_TILE),
                            lambda bi, r, n: (bi, jnp.minimum((n + 1) * qb, nblk - 1),
                                              r * N_COL_TILES + col + which))

    o_spec = pl.BlockSpec((1, qt, ATTN_DIM), lambda bi, r, n: (bi, n, r))
    l_spec = pl.BlockSpec((1, qt, LANES), lambda bi, r, n: (bi, n, r))
    in_specs = [cur(0), before(1), cur(1), after(1), before(2), cur(2), after(2)]
    args = [pv] * 7
    if prev is not None:
        in_specs += [o_spec, l_spec]
        args += [prev[0].reshape(b, sub_len, dilation * ATTN_DIM), prev[1].reshape(b, sub_len, dilation * LANES)]
    o, l = pl.pallas_call(
        functools.partial(_attn_kernel, sub_len=sub_len, qt=qt, has_prev=prev is not None),
        grid=(b, dilation, sub_len // qt),
        in_specs=in_specs,
        out_specs=[o_spec, l_spec],
        out_shape=[
            jax.ShapeDtypeStruct((b, sub_len, dilation * ATTN_DIM), BF16),
            jax.ShapeDtypeStruct((b, sub_len, dilation * LANES), F32),
        ],
        compiler_params=_params(("arbitrary", "arbitrary", "arbitrary"), 48),
        name=f"attn_g{g}",
    )(*args)
    return o.reshape(b, s, ATTN_DIM), l.reshape(b, s, LANES)


CONV_HALO = 16


def _conv_kernel(xp_ref, xc_ref, xn_ref, w_ref, b_ref, o_ref, buf, *, tm):
    i = pl.program_id(1)
    c = pl.program_id(2)
    last = pl.num_programs(1) - 1
    buf[0:CONV_HALO, :] = jnp.where(i > 0, xp_ref[0].astype(F32), 0.0)
    buf[CONV_HALO:CONV_HALO + tm, :] = xc_ref[0].astype(F32)
    buf[CONV_HALO + tm:, :] = jnp.where(i < last, xn_ref[0].astype(F32), 0.0)
    w = w_ref[...]
    y = jnp.broadcast_to(b_ref[...], (tm, COL_TILE))
    for tap in range(CONV_W):
        y = y + w[tap:tap + 1, :] * buf[pl.ds(CONV_HALO - CONV_W // 2 + tap, tm), :]
    y = y * _sigmoid(y)
    scale = jnp.where(c >= ML_DIM // COL_TILE, ML_HEAD_DIM ** -0.5, 1.0).astype(F32)
    o_ref[0] = (y * scale).astype(BF16)


def _conv_qk(proj3, conv_w, conv_b):
    b, s, _ = proj3.shape
    tm = min(1024, s)
    hb = tm // CONV_HALO
    nh = s // CONV_HALO
    n_tiles = 2 * ML_DIM // COL_TILE
    return pl.pallas_call(
        functools.partial(_conv_kernel, tm=tm),
        grid=(b, s // tm, n_tiles),
        in_specs=[
            pl.BlockSpec((1, CONV_HALO, COL_TILE), lambda bi, i, c: (bi, jnp.maximum(i * hb - 1, 0), c)),
            pl.BlockSpec((1, tm, COL_TILE), lambda bi, i, c: (bi, i, c)),
            pl.BlockSpec((1, CONV_HALO, COL_TILE), lambda bi, i, c: (bi, jnp.minimum((i + 1) * hb, nh - 1), c)),
            pl.BlockSpec((CONV_W, COL_TILE), lambda bi, i, c: (0, c)),
            pl.BlockSpec((1, COL_TILE), lambda bi, i, c: (0, c)),
        ],
        out_specs=pl.BlockSpec((1, tm, COL_TILE), lambda bi, i, c: (bi, i, c)),
        out_shape=jax.ShapeDtypeStruct((b, s, 2 * ML_DIM), BF16),
        scratch_shapes=[pltpu.VMEM((tm + 2 * CONV_HALO, COL_TILE), F32)],
        compiler_params=_params(("arbitrary", "arbitrary", "arbitrary"), 32),
        name="conv_qk",
    )(proj3, proj3, proj3, conv_w, conv_b)


def _gate_kernel(g_ref, b_ref, o_ref):
    g = g_ref[0] + b_ref[...]
    width = g.shape[1]
    logsig = jnp.minimum(g, 0.0) - jnp.log(1.0 + jnp.exp(-jnp.abs(g)))
    lane = lax.broadcasted_iota(jnp.int32, g.shape, 1) % ML_CHUNK
    pre = logsig
    suf = logsig
    step = 1
    while step < ML_CHUNK:
        pre = pre + jnp.where(lane >= step, pltpu.roll(pre, step, 1), 0.0)
        suf = suf + jnp.where(lane < ML_CHUNK - step, pltpu.roll(suf, width - step, 1), 0.0)
        step *= 2
    row = lax.broadcasted_iota(jnp.int32, g.shape, 0)
    is_f_fwd = (row >= ML_HEADS) & (row < 2 * ML_HEADS)
    is_f_bwd = row >= 3 * ML_HEADS
    o_ref[0] = jnp.where(is_f_fwd, pre, jnp.where(is_f_bwd, suf, g))


def _gate_prep(gates_t, gate_b):
    b, rows, s = gates_t.shape
    sb = min(2048, s)
    return pl.pallas_call(
        _gate_kernel,
        grid=(b, s // sb),
        in_specs=[
            pl.BlockSpec((1, rows, sb), lambda bi, i: (bi, 0, i)),
            pl.BlockSpec((rows, 1), lambda bi, i: (0, 0)),
        ],
        out_specs=pl.BlockSpec((1, rows, sb), lambda bi, i: (bi, 0, i)),
        out_shape=jax.ShapeDtypeStruct((b, rows, s), F32),
        compiler_params=_params(("arbitrary", "arbitrary"), 32),
        name="gate_prep",
    )(gates_t, gate_b)


def _mlstm_kernel(qf_ref, kf_ref, vf_ref, qb_ref, kb_ref, vb_ref, grf_ref, grb_ref, gcf_ref, gcb_ref,
                  hf_ref, hb_ref, s_scr, m_scr):
    c = pl.program_id(1)

    @pl.when(c == 0)
    def _():
        s_scr[...] = jnp.zeros_like(s_scr)
        m_scr[...] = jnp.zeros_like(m_scr)

    t_i = lax.broadcasted_iota(jnp.int32, (ML_CHUNK, ML_CHUNK), 0)
    s_i = lax.broadcasted_iota(jnp.int32, (ML_CHUNK, ML_CHUNK), 1)
    ones_col = jnp.where(lax.broadcasted_iota(jnp.int32, (ML_CHUNK, LANES), 1) == 0, 1.0, 0.0).astype(BF16)
    dirs = ((qf_ref, kf_ref, vf_ref, grf_ref, gcf_ref, hf_ref), (qb_ref, kb_ref, vb_ref, grb_ref, gcb_ref, hb_ref))
    for dirn, (q_ref, k_ref, v_ref, gr_ref, gc_ref, h_ref) in enumerate(dirs):
        mask = (s_i <= t_i) if dirn == 0 else (s_i >= t_i)
        gr = gr_ref[0]
        gc = gc_ref[0]
        for head in range(ML_HEADS):
            idx = dirn * ML_HEADS + head
            ii = dirn * 2 * ML_HEADS + head
            bi = ii + ML_HEADS
            hs = slice(head * ML_HEAD_DIM, (head + 1) * ML_HEAD_DIM)
            i_row, b_row = gr[ii:ii + 1, :], gr[bi:bi + 1, :]
            i_col, b_col = gc[:, ii:ii + 1], gc[:, bi:bi + 1]
            b_last = b_col[ML_CHUNK - 1:ML_CHUNK, :] if dirn == 0 else b_col[0:1, :]
            m = m_scr[idx][0:1, 0:1]
            q = q_ref[0, :, hs]
            k = k_ref[0, :, hs]
            v_ext = jnp.concatenate([v_ref[0, :, hs], ones_col], axis=1)
            state = s_scr[idx]

            dmat = jnp.where(mask, b_col - b_row + i_row, NEG_INF)
            inter = b_col + m
            m_t = jnp.maximum(inter, jnp.max(dmat, axis=-1, keepdims=True))
            w_intra = jnp.exp(dmat - m_t)
            w_inter = jnp.exp(inter - m_t)
            qk = lax.dot_general(q, k, (((1,), (1,)), ((), ())), preferred_element_type=F32) * w_intra
            num = (w_inter * jnp.dot(q, state.astype(BF16), preferred_element_type=F32)
                   + jnp.dot(qk.astype(BF16), v_ext, preferred_element_type=F32))
            nq = num[:, ML_HEAD_DIM:ML_HEAD_DIM + 1]
            den = jnp.maximum(jnp.abs(nq), jnp.exp(-m_t))
            h_ref[0, :, hs] = (num[:, :ML_HEAD_DIM] / den).astype(BF16)

            log_w = b_last - b_col + i_col
            m_new = jnp.maximum(b_last + m, jnp.max(log_w, axis=0, keepdims=True))
            decay = jnp.exp(b_last + m - m_new)
            wk = (k.astype(F32) * jnp.exp(log_w - m_new)).astype(BF16)
            upd = lax.dot_general(wk, v_ext, (((0,), (0,)), ((), ())), preferred_element_type=F32)
            s_scr[idx] = decay * state + upd
            m_scr[idx] = jnp.broadcast_to(m_new, m_scr.shape[1:])


def _mlstm(proj3, qk, g_rows, g_cols):
    b, s, _ = proj3.shape
    nc = s // ML_CHUNK
    v_tile = 2 * ML_DIM // ML_DIM
    n_rows = g_rows.shape[1]

    def fwd(col):
        return pl.BlockSpec((1, ML_CHUNK, ML_DIM), lambda bi, c: (bi, c, col))

    def bwd(col):
        return pl.BlockSpec((1, ML_CHUNK, ML_DIM), lambda bi, c: (bi, nc - 1 - c, col))

    return pl.pallas_call(
        _mlstm_kernel,
        grid=(b, nc),
        in_specs=[
            fwd(0), fwd(1), fwd(v_tile), bwd(0), bwd(1), bwd(v_tile),
            pl.BlockSpec((1, n_rows, ML_CHUNK), lambda bi, c: (bi, 0, c)),
            pl.BlockSpec((1, n_rows, ML_CHUNK), lambda bi, c: (bi, 0, nc - 1 - c)),
            pl.BlockSpec((1, ML_CHUNK, n_rows), lambda bi, c: (bi, c, 0)),
            pl.BlockSpec((1, ML_CHUNK, n_rows), lambda bi, c: (bi, nc - 1 - c, 0)),
        ],
        out_specs=[fwd(0), bwd(0)],
        out_shape=[jax.ShapeDtypeStruct((b, s, ML_DIM), BF16)] * 2,
        scratch_shapes=[
            pltpu.VMEM((2 * ML_HEADS, ML_HEAD_DIM, ML_EXT), F32),
            pltpu.VMEM((2 * ML_HEADS, 8, LANES), F32),
        ],
        compiler_params=_params(("arbitrary", "arbitrary"), 32),
        name="mlstm",
    )(qk, qk, proj3, qk, qk, proj3, g_rows, g_rows, g_cols, g_cols)


def _merge_kernel(x_ref, ao_ref, hf_ref, hb_ref, mo_ref, mg_ref, wpa_ref, wpm_ref, wo_ref, nfw_ref,
                  rw_ref, rb_ref, x1_ref, h2_ref, comb_ref):
    y_attn = jnp.dot(ao_ref[...], wpa_ref[...], preferred_element_type=F32)
    hsum = hf_ref[...].astype(F32) + hb_ref[...].astype(F32)
    ml = (_sigmoid(mo_ref[...].astype(F32)) * hsum).astype(BF16)
    y_ml = jnp.dot(ml, wpm_ref[...], preferred_element_type=F32)
    gates = _sigmoid(mg_ref[...].astype(F32))
    mixed = (gates[:, :D_MODEL] * y_attn + gates[:, D_MODEL:] * y_ml).astype(BF16)
    x1 = x_ref[...] + jnp.dot(mixed, wo_ref[...], preferred_element_type=F32)
    x1_ref[...] = x1
    ms = jnp.mean(x1 * x1, axis=-1, keepdims=True)
    h2 = x1 * lax.rsqrt(ms + RMS_EPS) * nfw_ref[...]
    h2_ref[...] = h2.astype(BF16)

    logits = lax.dot_general(rw_ref[...], h2, (((1,), (1,)), ((), ())), preferred_element_type=F32,
                             precision=lax.Precision.HIGHEST) + rb_ref[...]
    row = lax.broadcasted_iota(jnp.int32, logits.shape, 0)
    rest = logits
    vals, sels = [], []
    for _ in range(TOP_K):
        mx = jnp.max(rest, axis=0, keepdims=True)
        first = jnp.min(jnp.where(rest == mx, row, N_EXPERTS), axis=0, keepdims=True)
        sel = row == first
        vals.append(mx)
        sels.append(sel)
        rest = jnp.where(sel, -jnp.inf, rest)
    exps = [jnp.exp(v - vals[0]) for v in vals]
    tot = exps[0] + exps[1] + exps[2] + exps[3]
    comb = jnp.zeros_like(logits)
    for sel, e in zip(sels, exps):
        comb = comb + jnp.where(sel, e / tot, 0.0)
    comb_ref[...] = comb


def _merge(x2d, attn_o, h_f, h_b, proj, wpa, wpm, wo, nfw, rw_t, rb):
    t = x2d.shape[0]
    tm = min(512, t)
    ml_o_tile = ML_QKV_COLS // ML_O_COLS
    merge_tile = (ML_QKV_COLS + ML_O_COLS) // MERGE_COLS
    row = lambda width: pl.BlockSpec((tm, width), lambda i: (i, 0))
    full = lambda a: pl.BlockSpec(a.shape, lambda i: (0, 0))
    return pl.pallas_call(
        _merge_kernel,
        grid=(t // tm,),
        in_specs=[
            row(D_MODEL), row(ATTN_DIM), row(ML_DIM), row(ML_DIM),
            pl.BlockSpec((tm, ML_O_COLS), lambda i: (i, ml_o_tile)),
            pl.BlockSpec((tm, MERGE_COLS), lambda i: (i, merge_tile)),
            full(wpa), full(wpm), full(wo), full(nfw), full(rw_t), full(rb),
        ],
        out_specs=[row(D_MODEL), row(D_MODEL), pl.BlockSpec((N_EXPERTS, tm), lambda i: (0, i))],
        out_shape=[
            jax.ShapeDtypeStruct((t, D_MODEL), F32),
            jax.ShapeDtypeStruct((t, D_MODEL), BF16),
            jax.ShapeDtypeStruct((N_EXPERTS, t), F32),
        ],
        compiler_params=_params(("arbitrary",), 48),
        name="merge",
    )(x2d, attn_o, h_f, h_b, proj, proj, wpa, wpm, wo, nfw, rw_t, rb)


def _moe_kernel(x1_ref, h2_ref, comb_ref, wgu_ref, bgu_ref, wd_ref, bd_ref, nw_ref, o_ref, acc):
    e = pl.program_id(1)

    @pl.when(e == 0)
    def _():
        acc[...] = jnp.zeros_like(acc)

    gu = jnp.dot(h2_ref[...], wgu_ref[0], preferred_element_type=F32) + bgu_ref[0]
    gate = jnp.minimum(gu[:, :D_FF], SWIGLU_LIMIT)
    up = jnp.clip(gu[:, D_FF:], -SWIGLU_LIMIT, SWIGLU_LIMIT)
    hid = (up + 1.0) * gate * _sigmoid(SWIGLU_ALPHA * gate)
    y = jnp.dot(hid.astype(BF16), wd_ref[0], preferred_element_type=F32) + bd_ref[0]
    comb = comb_ref[...]
    lane = lax.broadcasted_iota(jnp.int32, comb.shape, 1)
    weight = jnp.sum(jnp.where(lane == e, comb, 0.0), axis=1, keepdims=True)
    acc[...] += weight * y

    @pl.when(e == pl.num_programs(1) - 1)
    def _():
        x = x1_ref[...] + acc[...]
        ms = jnp.mean(x * x, axis=-1, keepdims=True)
        o_ref[...] = x * lax.rsqrt(ms + RMS_EPS) * nw_ref[...]


def _moe(x1, h2, comb, wgu, bgu, wd, bd, nw):
    t = x1.shape[0]
    tm = min(512, t)
    return pl.pallas_call(
        _moe_kernel,
        grid=(t // tm, N_EXPERTS),
        in_specs=[
            pl.BlockSpec((tm, D_MODEL), lambda i, e: (i, 0)),
            pl.BlockSpec((tm, D_MODEL), lambda i, e: (i, 0)),
            pl.BlockSpec((tm, N_EXPERTS), lambda i, e: (i, 0)),
            pl.BlockSpec((1, D_MODEL, 2 * D_FF), lambda i, e: (e, 0, 0)),
            pl.BlockSpec((1, 1, 2 * D_FF), lambda i, e: (e, 0, 0)),
            pl.BlockSpec((1, D_FF, D_MODEL), lambda i, e: (e, 0, 0)),
            pl.BlockSpec((1, 1, D_MODEL), lambda i, e: (e, 0, 0)),
            pl.BlockSpec((1, D_MODEL), lambda i, e: (0, 0)),
        ],
        out_specs=pl.BlockSpec((tm, D_MODEL), lambda i, e: (i, 0)),
        out_shape=jax.ShapeDtypeStruct((t, D_MODEL), F32),
        scratch_shapes=[pltpu.VMEM((tm, D_MODEL), F32)],
        compiler_params=_params(("arbitrary", "arbitrary"), 56),
        name="moe",
    )(x1, h2, comb, wgu, bgu, wd, bd, nw)


def _rope_tables(seq):
    half = ATTN_HEAD_DIM // 2
    inv_freq = ROPE_THETA ** (-jnp.arange(half, dtype=F32) / half)
    ang = jnp.arange(seq, dtype=F32)[:, None] * inv_freq[None, :]
    cos, sin = jnp.cos(ang), jnp.sin(ang)
    reps = LANES // ATTN_HEAD_DIM
    cos_t = jnp.tile(jnp.concatenate([cos, cos], axis=1), (1, reps))
    sin_t = jnp.tile(jnp.concatenate([-sin, sin], axis=1), (1, reps))
    return cos_t, sin_t


def _pack_layer(w_in, conv_w, conv_b, gate_b, wpa, wpm, wo, nmw, nfw, rw, rb, wgu, bgu, wd, bd):
    c0 = ATTN_QKV_COLS
    c1 = c0 + ML_QKV_COLS
    c2 = c1 + ML_O_COLS
    c3 = c2 + ML_GATE_COLS
    w_main = jnp.concatenate([w_in[:, c0:c2], w_in[:, c3:], w_in[:, :c0]], axis=1).astype(BF16)
    w_gate = jnp.pad(w_in[:, c2:c3], ((0, 0), (0, LANES - ML_GATE_COLS))).astype(BF16)
    return dict(
        w_main=w_main, w_gate=w_gate, conv_w=conv_w, conv_b=conv_b.reshape(1, -1),
        gate_b=gate_b.reshape(-1, 1), wpa=wpa.astype(BF16), wpm=wpm.astype(BF16), wo=wo.astype(BF16),
        nmw=nmw.reshape(1, -1), nfw=nfw.reshape(1, -1), rw_t=rw.T, rb=rb.reshape(-1, 1),
        wgu=wgu.astype(BF16), bgu=bgu.reshape(N_EXPERTS, 1, -1), wd=wd.astype(BF16),
        bd=bd.reshape(N_EXPERTS, 1, -1))


def _layer(x, p, final_w, tables):
    b, s, _ = x.shape
    x2d = x.reshape(b * s, D_MODEL)
    proj, gates = _inproj(x2d, p["nmw"], p["w_main"], p["w_gate"], tables[0], tables[1], s)
    proj3 = proj.reshape(b, s, N_MAIN)

    prev = None
    for g, (_, dilation) in enumerate(ATTN_PATTERNS):
        prev = _attn_group(proj3, g, dilation, prev)
    attn_o = prev[0].reshape(b * s, ATTN_DIM)

    qk = _conv_qk(proj3, p["conv_w"], p["conv_b"])
    gates_t = gates[:, :ML_GATE_COLS].reshape(b, s, ML_GATE_COLS).transpose(0, 2, 1)
    g_rows = _gate_prep(gates_t, p["gate_b"])
    g_cols = g_rows.transpose(0, 2, 1)
    h_f, h_b = _mlstm(proj3, qk, g_rows, g_cols)

    x1, h2, comb_t = _merge(x2d, attn_o, h_f.reshape(b * s, ML_DIM), h_b.reshape(b * s, ML_DIM), proj,
                            p["wpa"], p["wpm"], p["wo"], p["nfw"], p["rw_t"], p["rb"])
    y = _moe(x1, h2, comb_t.T, p["wgu"], p["bgu"], p["wd"], p["bd"], final_w)
    return y.reshape(b, s, D_MODEL)


def kernel(x_prompt, x_sample, norm_mix_w, w_in, mlstm_conv_w, mlstm_conv_b, mlstm_gate_b, w_proj_attn,
           w_proj_mlstm, w_out, norm_ffn_w, router_w, router_b, expert_w_gu, expert_b_gu, expert_w_down,
           expert_b_down, norm_final_w):
    depth = w_in.shape[0]
    assert depth == 1, "the final RMSNorm is fused into the last layer's MoE kernel"
    p = _pack_layer(w_in[0], mlstm_conv_w[0], mlstm_conv_b[0], mlstm_gate_b[0], w_proj_attn[0],
                    w_proj_mlstm[0], w_out[0], norm_mix_w[0], norm_ffn_w[0], router_w[0], router_b[0],
                    expert_w_gu[0], expert_b_gu[0], expert_w_down[0], expert_b_down[0])
    final_w = norm_final_w.reshape(1, -1)
    outs = []
    for x in (x_prompt, x_sample):
        outs.append(_layer(x, p, final_w, _rope_tables(x.shape[1])))
    return tuple(outs)
```

```python
import functools

import jax
import jax.numpy as jnp
from jax import lax
from jax.experimental import pallas as pl
from jax.experimental.pallas import tpu as pltpu

F32 = jnp.float32
BF16 = jnp.bfloat16

D_MODEL = 1024
ATTN_PATTERNS = ((128, 1), (512, 4), (2048, 16))
N_GROUPS = 3
ATTN_HEADS = 8
ATTN_HEAD_DIM = 64
ATTN_DIM = ATTN_HEADS * ATTN_HEAD_DIM
ATTN_BLOCK = 64
ROPE_THETA = 10000.0
ML_DIM = D_MODEL
ML_HEADS = 4
ML_HEAD_DIM = ML_DIM // ML_HEADS
ML_CHUNK = 128
CONV_W = 5
ATTN_QKV_COLS = N_GROUPS * 3 * ATTN_DIM
ML_QKV_COLS = 3 * ML_DIM
ML_O_COLS = ML_DIM
ML_GATE_COLS = 4 * ML_HEADS
MERGE_COLS = 2 * D_MODEL
N_EXPERTS = 32
TOP_K = 4
D_FF = D_MODEL
SWIGLU_LIMIT = 7.0
SWIGLU_ALPHA = 1.702
RMS_EPS = 1e-5
NEG_INF = -1e30

LANES = 128
COL_TILE = 512
N_PACKED = ML_QKV_COLS + ML_O_COLS + MERGE_COLS + ATTN_QKV_COLS
N_COL_TILES = N_PACKED // COL_TILE
ATTN_TILE0 = (ML_QKV_COLS + ML_O_COLS + MERGE_COLS) // COL_TILE
GROUP_TILES = 3 * ATTN_DIM // COL_TILE
N_MAIN_TILES = ATTN_TILE0 + GROUP_TILES
N_MAIN = N_MAIN_TILES * COL_TILE
ML_EXT = ML_HEAD_DIM + LANES


def _params(sem, vmem_mb):
    return pltpu.CompilerParams(dimension_semantics=sem, vmem_limit_bytes=vmem_mb * 1024 * 1024)


def _sigmoid(x):
    return 1.0 / (1.0 + jnp.exp(-x))


def _inproj_kernel(x_ref, nw_ref, w_ref, wg_ref, cos_ref, sin_ref, out_ref, gates_ref, g1_ref, g2_ref,
                   h_scr, slab):
    j = pl.program_id(1)
    tm = x_ref.shape[0]
    n_slabs = COL_TILE // LANES

    @pl.when(j == 0)
    def _():
        x = x_ref[...]
        ms = jnp.mean(x * x, axis=-1, keepdims=True)
        h = (x * lax.rsqrt(ms + RMS_EPS) * nw_ref[...]).astype(BF16)
        h_scr[...] = h
        gates_ref[...] = jnp.dot(h, wg_ref[...], preferred_element_type=F32)

    acc = jnp.dot(h_scr[...], w_ref[...], preferred_element_type=F32)

    def rope(a, scale):
        c = jnp.concatenate([cos_ref[...]] * n_slabs, axis=1)
        s = jnp.concatenate([sin_ref[...]] * n_slabs, axis=1)
        lane = lax.broadcasted_iota(jnp.int32, a.shape, 1)
        half = ATTN_HEAD_DIM // 2
        first = (lane % ATTN_HEAD_DIM) < half
        sw = jnp.where(first, pltpu.roll(a, COL_TILE - half, 1), pltpu.roll(a, half, 1))
        return (a * c + sw * s) * scale

    def deinterleave(val, dst_ref, dilation):
        rows = tm // dilation
        for s in range(n_slabs):
            slab[s] = val[:, s * LANES:(s + 1) * LANES]
        for r in range(dilation):
            piece = jnp.concatenate([slab[s, pl.ds(r, rows, stride=dilation), :] for s in range(n_slabs)], axis=1)
            dst_ref[0, r] = piece.astype(BF16)

    @pl.when(j < ATTN_TILE0)
    def _():
        out_ref[...] = acc.astype(BF16)

    dst = (out_ref, g1_ref, g2_ref)
    for g, (_, dilation) in enumerate(ATTN_PATTERNS):
        for which in range(GROUP_TILES):
            @pl.when(j == ATTN_TILE0 + GROUP_TILES * g + which)
            def _(g=g, dilation=dilation, which=which):
                val = acc if which == 2 else rope(acc, ATTN_HEAD_DIM ** -0.5 if which == 0 else 1.0)
                if dilation == 1:
                    out_ref[...] = val.astype(BF16)
                else:
                    deinterleave(val, dst[g], dilation)


def _inproj(x2d, nw, w_main, w_gate, cos_t, sin_t, seq):
    t = x2d.shape[0]
    tm = min(1024, seq)
    tiles_per_seq = seq // tm

    def group_out(g, dilation):
        first = ATTN_TILE0 + GROUP_TILES * g
        return (pl.BlockSpec((1, dilation, tm // dilation, COL_TILE),
                             lambda i, j: (i, 0, 0, jnp.clip(j - first, 0, GROUP_TILES - 1))),
                jax.ShapeDtypeStruct((t // tm, dilation, tm // dilation, 3 * ATTN_DIM), BF16))

    (g1_spec, g1_shape), (g2_spec, g2_shape) = [group_out(g, ATTN_PATTERNS[g][1]) for g in (1, 2)]
    return pl.pallas_call(
        _inproj_kernel,
        grid=(t // tm, N_COL_TILES),
        in_specs=[
            pl.BlockSpec((tm, D_MODEL), lambda i, j: (i, 0)),
            pl.BlockSpec((1, D_MODEL), lambda i, j: (0, 0)),
            pl.BlockSpec((D_MODEL, COL_TILE), lambda i, j: (0, j)),
            pl.BlockSpec((D_MODEL, LANES), lambda i, j: (0, 0)),
            pl.BlockSpec((tm, LANES), lambda i, j: (i % tiles_per_seq, 0)),
            pl.BlockSpec((tm, LANES), lambda i, j: (i % tiles_per_seq, 0)),
        ],
        out_specs=[
            pl.BlockSpec((tm, COL_TILE), lambda i, j: (i, jnp.minimum(j, N_MAIN_TILES - 1))),
            pl.BlockSpec((tm, LANES), lambda i, j: (i, 0)),
            g1_spec, g2_spec,
        ],
        out_shape=[
            jax.ShapeDtypeStruct((t, N_MAIN), BF16),
            jax.ShapeDtypeStruct((t, LANES), F32),
            g1_shape, g2_shape,
        ],
        scratch_shapes=[pltpu.VMEM((tm, D_MODEL), BF16), pltpu.VMEM((COL_TILE // LANES, tm, LANES), F32)],
        compiler_params=_params(("arbitrary", "arbitrary"), 56),
        name="inproj",
    )(x2d, nw, w_main, w_gate, cos_t, sin_t)


def _window_mask(n, qt, sub_len):
    kt = qt + 2 * ATTN_BLOCK
    qpos = n * qt + lax.broadcasted_iota(jnp.int32, (qt, kt), 0)
    kpos = n * qt - ATTN_BLOCK + lax.broadcasted_iota(jnp.int32, (qt, kt), 1)
    return (jnp.abs(kpos - qpos) <= ATTN_BLOCK) & (kpos >= 0) & (kpos < sub_len)


def _attend(q, k, v, valid):
    qt = q.shape[0]
    lane = lax.broadcasted_iota(jnp.int32, (qt, LANES), 1)
    lo = lane < ATTN_HEAD_DIM
    lse_out = jnp.zeros((qt, LANES), F32)
    outs = []
    for p in range(ATTN_HEADS // 2):
        sl = slice(p * LANES, (p + 1) * LANES)
        qp, kp_, vp_ = q[:, sl], k[:, sl], v[:, sl]
        halves = []
        for half in range(2):
            hm = lo if half == 0 else jnp.logical_not(lo)
            qm = jnp.where(hm, qp, jnp.zeros_like(qp))
            s = lax.dot_general(qm, kp_, (((1,), (1,)), ((), ())), preferred_element_type=F32)
            s = jnp.where(valid, s, NEG_INF)
            mx = jnp.max(s, axis=-1, keepdims=True)
            e = jnp.exp(s - mx)
            den = jnp.sum(e, axis=-1, keepdims=True)
            halves.append(jnp.dot(e.astype(BF16), vp_, preferred_element_type=F32) * (1.0 / den))
            lse_out = jnp.where(lane == 2 * p + half, mx + jnp.log(den), lse_out)
        outs.append(jnp.where(lo, halves[0], halves[1]))
    return outs, lse_out


def _merge_groups(o_cur, lse_cur, o_prev, lse_prev):
    rows = lse_cur.shape[0]
    lane = lax.broadcasted_iota(jnp.int32, (rows, LANES), 1)
    lo = lane < ATTN_HEAD_DIM
    lse_out = jnp.zeros((rows, LANES), F32)
    outs = []
    for p in range(ATTN_HEADS // 2):
        w_prev, w_cur = [], []
        for half in range(2):
            head = 2 * p + half
            lc = lse_cur[:, head:head + 1]
            lp = lse_prev[:, head:head + 1]
            m = jnp.maximum(lp, lc)
            a = jnp.exp(lp - m)
            b = jnp.exp(lc - m)
            tot = a + b
            w_prev.append(a / tot)
            w_cur.append(b / tot)
            lse_out = jnp.where(lane == head, m + jnp.log(tot), lse_out)
        outs.append(o_cur[p] * jnp.where(lo, w_cur[0], w_cur[1]) + o_prev[p] * jnp.where(lo, w_prev[0], w_prev[1]))
    return outs, lse_out


def _attn_kernel(q_ref, kp_ref, kc_ref, kn_ref, vp_ref, vc_ref, vn_ref, o_ref, l_ref, *, sub_len, qt):
    valid = _window_mask(pl.program_id(1), qt, sub_len)
    k = jnp.concatenate([kp_ref[0], kc_ref[0], kn_ref[0]], axis=0)
    v = jnp.concatenate([vp_ref[0], vc_ref[0], vn_ref[0]], axis=0)
    outs, lse = _attend(q_ref[0], k, v, valid)
    for p, o in enumerate(outs):
        o_ref[0, :, p * LANES:(p + 1) * LANES] = o.astype(BF16)
    l_ref[0] = lse


MERGE_ROWS = 256


def _attn_dilated_kernel(q_ref, kp_ref, kc_ref, kn_ref, vp_ref, vc_ref, vn_ref, op_ref, lp_ref, o_ref, l_ref,
                         o_slab, l_slab, *, sub_len, qt, dilation):
    valid = _window_mask(pl.program_id(1), qt, sub_len)

    def residue(r, c):
        k = jnp.concatenate([kp_ref[0, r], kc_ref[0, r], kn_ref[0, r]], axis=0)
        v = jnp.concatenate([vp_ref[0, r], vc_ref[0, r], vn_ref[0, r]], axis=0)
        outs, lse = _attend(q_ref[0, r], k, v, valid)
        for p, o in enumerate(outs):
            o_slab[p, pl.ds(r, qt, stride=dilation), :] = o
        l_slab[pl.ds(r, qt, stride=dilation), :] = lse
        return c

    lax.fori_loop(0, dilation, residue, 0)

    def merge(i, c):
        rows = pl.ds(pl.multiple_of(i * MERGE_ROWS, MERGE_ROWS), MERGE_ROWS)
        prev = op_ref[rows, :].astype(F32)
        o_cur = [o_slab[p, rows, :] for p in range(ATTN_HEADS // 2)]
        o_prev = [prev[:, p * LANES:(p + 1) * LANES] for p in range(ATTN_HEADS // 2)]
        outs, lse = _merge_groups(o_cur, l_slab[rows, :], o_prev, lp_ref[rows, :])
        for p, o in enumerate(outs):
            o_ref[rows, p * LANES:(p + 1) * LANES] = o.astype(BF16)
        l_ref[rows, :] = lse
        return c

    lax.fori_loop(0, qt * dilation // MERGE_ROWS, merge, 0)


def _attn_plain(proj3):
    b, s, _ = proj3.shape
    qt = min(256, s)
    nblk = s // ATTN_BLOCK
    qb = qt // ATTN_BLOCK

    def cur(which):
        return pl.BlockSpec((1, qt, COL_TILE), lambda bi, n: (bi, n, ATTN_TILE0 + which))

    def before(which):
        return pl.BlockSpec((1, ATTN_BLOCK, COL_TILE),
                            lambda bi, n: (bi, jnp.maximum(n * qb - 1, 0), ATTN_TILE0 + which))

    def after(which):
        return pl.BlockSpec((1, ATTN_BLOCK, COL_TILE),
                            lambda bi, n: (bi, jnp.minimum((n + 1) * qb, nblk - 1), ATTN_TILE0 + which))

    o, l = pl.pallas_call(
        functools.partial(_attn_kernel, sub_len=s, qt=qt),
        grid=(b, s // qt),
        in_specs=[cur(0), before(1), cur(1), after(1), before(2), cur(2), after(2)],
        out_specs=[pl.BlockSpec((1, qt, ATTN_DIM), lambda bi, n: (bi, n, 0)),
                   pl.BlockSpec((1, qt, LANES), lambda bi, n: (bi, n, 0))],
        out_shape=[jax.ShapeDtypeStruct((b, s, ATTN_DIM), BF16), jax.ShapeDtypeStruct((b, s, LANES), F32)],
        compiler_params=_params(("arbitrary", "arbitrary"), 48),
        name="attn_g0",
    )(*([proj3] * 7))
    return o.reshape(b * s, ATTN_DIM), l.reshape(b * s, LANES)


def _attn_dilated(qkv, g, dilation, prev, b, s):
    n_tiles, _, rows, _ = qkv.shape
    tiles_per_seq = n_tiles // b
    sub_len = s // dilation
    qt = min(256, rows)
    span = qt * dilation
    q_per_tile = rows // qt
    h_per_tile = rows // ATTN_BLOCK
    qb = qt // ATTN_BLOCK
    nblk = sub_len // ATTN_BLOCK

    def cur(which):
        return pl.BlockSpec((1, dilation, qt, COL_TILE),
                            lambda bi, n: (bi * tiles_per_seq + n // q_per_tile, 0, n % q_per_tile, which))

    def halo(which, blk_of):
        def index(bi, n):
            blk = blk_of(n)
            return (bi * tiles_per_seq + blk // h_per_tile, 0, blk % h_per_tile, which)
        return pl.BlockSpec((1, dilation, ATTN_BLOCK, COL_TILE), index)

    before = lambda which: halo(which, lambda n: jnp.maximum(n * qb - 1, 0))
    after = lambda which: halo(which, lambda n: jnp.minimum((n + 1) * qb, nblk - 1))
    steps = sub_len // qt
    o_spec = pl.BlockSpec((span, ATTN_DIM), lambda bi, n: (bi * steps + n, 0))
    l_spec = pl.BlockSpec((span, LANES), lambda bi, n: (bi * steps + n, 0))
    return pl.pallas_call(
        functools.partial(_attn_dilated_kernel, sub_len=sub_len, qt=qt, dilation=dilation),
        grid=(b, steps),
        in_specs=[cur(0), before(1), cur(1), after(1), before(2), cur(2), after(2), o_spec, l_spec],
        out_specs=[o_spec, l_spec],
        out_shape=[jax.ShapeDtypeStruct((b * s, ATTN_DIM), BF16), jax.ShapeDtypeStruct((b * s, LANES), F32)],
        scratch_shapes=[pltpu.VMEM((ATTN_HEADS // 2, span, LANES), F32), pltpu.VMEM((span, LANES), F32)],
        compiler_params=_params(("arbitrary", "arbitrary"), 56),
        name=f"attn_g{g}",
    )(*([qkv] * 7), prev[0], prev[1])


CONV_HALO = 16


def _conv_kernel(xp_ref, xc_ref, xn_ref, w_ref, b_ref, o_ref, buf, *, tm):
    i = pl.program_id(1)
    c = pl.program_id(2)
    last = pl.num_programs(1) - 1
    buf[0:CONV_HALO, :] = jnp.where(i > 0, xp_ref[0].astype(F32), 0.0)
    buf[CONV_HALO:CONV_HALO + tm, :] = xc_ref[0].astype(F32)
    buf[CONV_HALO + tm:, :] = jnp.where(i < last, xn_ref[0].astype(F32), 0.0)
    w = w_ref[...]
    y = jnp.broadcast_to(b_ref[...], (tm, COL_TILE))
    for tap in range(CONV_W):
        y = y + w[tap:tap + 1, :] * buf[pl.ds(CONV_HALO - CONV_W // 2 + tap, tm), :]
    y = y * _sigmoid(y)
    scale = jnp.where(c >= ML_DIM // COL_TILE, ML_HEAD_DIM ** -0.5, 1.0).astype(F32)
    o_ref[0] = (y * scale).astype(BF16)


def _conv_qk(proj3, conv_w, conv_b):
    b, s, _ = proj3.shape
    tm = min(1024, s)
    hb = tm // CONV_HALO
    nh = s // CONV_HALO
    n_tiles = 2 * ML_DIM // COL_TILE
    return pl.pallas_call(
        functools.partial(_conv_kernel, tm=tm),
        grid=(b, s // tm, n_tiles),
        in_specs=[
            pl.BlockSpec((1, CONV_HALO, COL_TILE), lambda bi, i, c: (bi, jnp.maximum(i * hb - 1, 0), c)),
            pl.BlockSpec((1, tm, COL_TILE), lambda bi, i, c: (bi, i, c)),
            pl.BlockSpec((1, CONV_HALO, COL_TILE), lambda bi, i, c: (bi, jnp.minimum((i + 1) * hb, nh - 1), c)),
            pl.BlockSpec((CONV_W, COL_TILE), lambda bi, i, c: (0, c)),
            pl.BlockSpec((1, COL_TILE), lambda bi, i, c: (0, c)),
        ],
        out_specs=pl.BlockSpec((1, tm, COL_TILE), lambda bi, i, c: (bi, i, c)),
        out_shape=jax.ShapeDtypeStruct((b, s, 2 * ML_DIM), BF16),
        scratch_shapes=[pltpu.VMEM((tm + 2 * CONV_HALO, COL_TILE), F32)],
        compiler_params=_params(("arbitrary", "arbitrary", "arbitrary"), 32),
        name="conv_qk",
    )(proj3, proj3, proj3, conv_w, conv_b)


def _gate_kernel(g_ref, b_ref, o_ref):
    g = g_ref[0] + b_ref[...]
    width = g.shape[1]
    logsig = jnp.minimum(g, 0.0) - jnp.log(1.0 + jnp.exp(-jnp.abs(g)))
    lane = lax.broadcasted_iota(jnp.int32, g.shape, 1) % ML_CHUNK
    pre = logsig
    suf = logsig
    step = 1
    while step < ML_CHUNK:
        pre = pre + jnp.where(lane >= step, pltpu.roll(pre, step, 1), 0.0)
        suf = suf + jnp.where(lane < ML_CHUNK - step, pltpu.roll(suf, width - step, 1), 0.0)
        step *= 2
    row = lax.broadcasted_iota(jnp.int32, g.shape, 0)
    is_f_fwd = (row >= ML_HEADS) & (row < 2 * ML_HEADS)
    is_f_bwd = row >= 3 * ML_HEADS
    o_ref[0] = jnp.where(is_f_fwd, pre, jnp.where(is_f_bwd, suf, g))


def _gate_prep(gates_t, gate_b):
    b, rows, s = gates_t.shape
    sb = min(2048, s)
    return pl.pallas_call(
        _gate_kernel,
        grid=(b, s // sb),
        in_specs=[
            pl.BlockSpec((1, rows, sb), lambda bi, i: (bi, 0, i)),
            pl.BlockSpec((rows, 1), lambda bi, i: (0, 0)),
        ],
        out_specs=pl.BlockSpec((1, rows, sb), lambda bi, i: (bi, 0, i)),
        out_shape=jax.ShapeDtypeStruct((b, rows, s), F32),
        compiler_params=_params(("arbitrary", "arbitrary"), 32),
        name="gate_prep",
    )(gates_t, gate_b)


def _mlstm_kernel(qf_ref, kf_ref, vf_ref, qb_ref, kb_ref, vb_ref, grf_ref, grb_ref, gcf_ref, gcb_ref,
                  hf_ref, hb_ref, s_scr, m_scr):
    c = pl.program_id(1)

    @pl.when(c == 0)
    def _():
        s_scr[...] = jnp.zeros_like(s_scr)
        m_scr[...] = jnp.zeros_like(m_scr)

    t_i = lax.broadcasted_iota(jnp.int32, (ML_CHUNK, ML_CHUNK), 0)
    s_i = lax.broadcasted_iota(jnp.int32, (ML_CHUNK, ML_CHUNK), 1)
    ones_col = jnp.where(lax.broadcasted_iota(jnp.int32, (ML_CHUNK, LANES), 1) == 0, 1.0, 0.0).astype(BF16)
    dirs = ((qf_ref, kf_ref, vf_ref, grf_ref, gcf_ref, hf_ref), (qb_ref, kb_ref, vb_ref, grb_ref, gcb_ref, hb_ref))
    for dirn, (q_ref, k_ref, v_ref, gr_ref, gc_ref, h_ref) in enumerate(dirs):
        mask = (s_i <= t_i) if dirn == 0 else (s_i >= t_i)
        gr = gr_ref[0]
        gc = gc_ref[0]
        for head in range(ML_HEADS):
            idx = dirn * ML_HEADS + head
            ii = dirn * 2 * ML_HEADS + head
            bi = ii + ML_HEADS
            hs = slice(head * ML_HEAD_DIM, (head + 1) * ML_HEAD_DIM)
            i_row, b_row = gr[ii:ii + 1, :], gr[bi:bi + 1, :]
            i_col, b_col = gc[:, ii:ii + 1], gc[:, bi:bi + 1]
            b_last = b_col[ML_CHUNK - 1:ML_CHUNK, :] if dirn == 0 else b_col[0:1, :]
            m = m_scr[idx][0:1, 0:1]
            q = q_ref[0, :, hs]
            k = k_ref[0, :, hs]
            v_ext = jnp.concatenate([v_ref[0, :, hs], ones_col], axis=1)
            state = s_scr[idx]

            dmat = jnp.where(mask, b_col - b_row + i_row, NEG_INF)
            inter = b_col + m
            m_t = jnp.maximum(inter, jnp.max(dmat, axis=-1, keepdims=True))
            w_intra = jnp.exp(dmat - m_t)
            w_inter = jnp.exp(inter - m_t)
            qk = lax.dot_general(q, k, (((1,), (1,)), ((), ())), preferred_element_type=F32) * w_intra
            num = (w_inter * jnp.dot(q, state.astype(BF16), preferred_element_type=F32)
                   + jnp.dot(qk.astype(BF16), v_ext, preferred_element_type=F32))
            nq = num[:, ML_HEAD_DIM:ML_HEAD_DIM + 1]
            den = jnp.maximum(jnp.abs(nq), jnp.exp(-m_t))
            h_ref[0, :, hs] = (num[:, :ML_HEAD_DIM] / den).astype(BF16)

            log_w = b_last - b_col + i_col
            m_new = jnp.maximum(b_last + m, jnp.max(log_w, axis=0, keepdims=True))
            decay = jnp.exp(b_last + m - m_new)
            wk = (k.astype(F32) * jnp.exp(log_w - m_new)).astype(BF16)
            upd = lax.dot_general(wk, v_ext, (((0,), (0,)), ((), ())), preferred_element_type=F32)
            s_scr[idx] = decay * state + upd
            m_scr[idx] = jnp.broadcast_to(m_new, m_scr.shape[1:])


def _mlstm(proj3, qk, g_rows, g_cols):
    b, s, _ = proj3.shape
    nc = s // ML_CHUNK
    v_tile = 2 * ML_DIM // ML_DIM
    n_rows = g_rows.shape[1]

    def fwd(col):
        return pl.BlockSpec((1, ML_CHUNK, ML_DIM), lambda bi, c: (bi, c, col))

    def bwd(col):
        return pl.BlockSpec((1, ML_CHUNK, ML_DIM), lambda bi, c: (bi, nc - 1 - c, col))

    return pl.pallas_call(
        _mlstm_kernel,
        grid=(b, nc),
        in_specs=[
            fwd(0), fwd(1), fwd(v_tile), bwd(0), bwd(1), bwd(v_tile),
            pl.BlockSpec((1, n_rows, ML_CHUNK), lambda bi, c: (bi, 0, c)),
            pl.BlockSpec((1, n_rows, ML_CHUNK), lambda bi, c: (bi, 0, nc - 1 - c)),
            pl.BlockSpec((1, ML_CHUNK, n_rows), lambda bi, c: (bi, c, 0)),
            pl.BlockSpec((1, ML_CHUNK, n_rows), lambda bi, c: (bi, nc - 1 - c, 0)),
        ],
        out_specs=[fwd(0), bwd(0)],
        out_shape=[jax.ShapeDtypeStruct((b, s, ML_DIM), BF16)] * 2,
        scratch_shapes=[
            pltpu.VMEM((2 * ML_HEADS, ML_HEAD_DIM, ML_EXT), F32),
            pltpu.VMEM((2 * ML_HEADS, 8, LANES), F32),
        ],
        compiler_params=_params(("arbitrary", "arbitrary"), 32),
        name="mlstm",
    )(qk, qk, proj3, qk, qk, proj3, g_rows, g_rows, g_cols, g_cols)


def _pack_bf16_pair(lo, hi):
    lo_bits = lax.bitcast_convert_type(lo.astype(BF16).astype(F32), jnp.uint32)
    hi_bits = lax.bitcast_convert_type(hi.astype(BF16).astype(F32), jnp.uint32)
    return (hi_bits & jnp.uint32(0xFFFF0000)) | (lo_bits >> 16)


def _unpack_bf16_pair(packed):
    lo = lax.bitcast_convert_type(packed << 16, F32)
    hi = lax.bitcast_convert_type(packed & jnp.uint32(0xFFFF0000), F32)
    return lo, hi


def _merge_kernel(x_ref, ao_ref, hf_ref, hb_ref, mo_ref, mg_ref, wpa_ref, wpm_ref, wo_ref, nfw_ref,
                  rw_ref, rb_ref, x1_ref, h2_ref, idx_ref, wt_ref):
    y_attn = jnp.dot(ao_ref[...], wpa_ref[...], preferred_element_type=F32)
    hsum = hf_ref[...].astype(F32) + hb_ref[...].astype(F32)
    ml = (_sigmoid(mo_ref[...].astype(F32)) * hsum).astype(BF16)
    y_ml = jnp.dot(ml, wpm_ref[...], preferred_element_type=F32)
    gates = _sigmoid(mg_ref[...].astype(F32))
    mixed = (gates[:, :D_MODEL] * y_attn + gates[:, D_MODEL:] * y_ml).astype(BF16)
    x1 = x_ref[...] + jnp.dot(mixed, wo_ref[...], preferred_element_type=F32)
    x1_ref[...] = x1
    ms = jnp.mean(x1 * x1, axis=-1, keepdims=True)
    h2 = x1 * lax.rsqrt(ms + RMS_EPS) * nfw_ref[...]
    h2_ref[...] = _pack_bf16_pair(h2[:, :D_MODEL // 2], h2[:, D_MODEL // 2:])

    logits = lax.dot_general(rw_ref[...], h2, (((1,), (1,)), ((), ())), preferred_element_type=F32,
                             precision=lax.Precision.HIGHEST) + rb_ref[...]
    row = lax.broadcasted_iota(jnp.int32, logits.shape, 0)
    rest = logits
    vals = []
    for k in range(TOP_K):
        mx = jnp.max(rest, axis=0, keepdims=True)
        first = jnp.min(jnp.where(rest == mx, row, N_EXPERTS), axis=0, keepdims=True)
        vals.append(mx)
        idx_ref[k:k + 1, :] = first
        rest = jnp.where(row == first, -jnp.inf, rest)
    exps = [jnp.exp(v - vals[0]) for v in vals]
    tot = exps[0] + exps[1] + exps[2] + exps[3]
    for k in range(TOP_K):
        wt_ref[k:k + 1, :] = exps[k] / tot


def _merge(x2d, attn_o, h_f, h_b, proj, wpa, wpm, wo, nfw, rw_t, rb):
    t = x2d.shape[0]
    tm = min(512, t)
    ml_o_tile = ML_QKV_COLS // ML_O_COLS
    merge_tile = (ML_QKV_COLS + ML_O_COLS) // MERGE_COLS
    row = lambda width: pl.BlockSpec((tm, width), lambda i: (i, 0))
    full = lambda a: pl.BlockSpec(a.shape, lambda i: (0, 0))
    return pl.pallas_call(
        _merge_kernel,
        grid=(t // tm,),
        in_specs=[
            row(D_MODEL), row(ATTN_DIM), row(ML_DIM), row(ML_DIM),
            pl.BlockSpec((tm, ML_O_COLS), lambda i: (i, ml_o_tile)),
            pl.BlockSpec((tm, MERGE_COLS), lambda i: (i, merge_tile)),
            full(wpa), full(wpm), full(wo), full(nfw), full(rw_t), full(rb),
        ],
        out_specs=[row(D_MODEL), row(D_MODEL // 2), pl.BlockSpec((TOP_K, tm), lambda i: (0, i)),
                   pl.BlockSpec((TOP_K, tm), lambda i: (0, i))],
        out_shape=[
            jax.ShapeDtypeStruct((t, D_MODEL), F32),
            jax.ShapeDtypeStruct((t, D_MODEL // 2), jnp.uint32),
            jax.ShapeDtypeStruct((TOP_K, t), jnp.int32),
            jax.ShapeDtypeStruct((TOP_K, t), F32),
        ],
        compiler_params=_params(("arbitrary",), 48),
        name="merge",
    )(x2d, attn_o, h_f, h_b, proj, proj, wpa, wpm, wo, nfw, rw_t, rb)


ROUTE_TILE = 512


def _route_kernel(idx_ref, pos_ref, cnt_ref, base):
    @pl.when(pl.program_id(0) == 0)
    def _():
        base[...] = jnp.zeros_like(base)

    idx = idx_ref[...]
    row = lax.broadcasted_iota(jnp.int32, (N_EXPERTS, ROUTE_TILE), 0)
    onehot = jnp.zeros((N_EXPERTS, ROUTE_TILE), F32)
    for k in range(TOP_K):
        onehot = onehot + jnp.where(row == idx[k:k + 1, :], 1.0, 0.0)
    s_i = lax.broadcasted_iota(jnp.int32, (ROUTE_TILE, ROUTE_TILE), 0)
    t_i = lax.broadcasted_iota(jnp.int32, (ROUTE_TILE, ROUTE_TILE), 1)
    upper = jnp.where(s_i <= t_i, 1.0, 0.0).astype(BF16)
    incl = jnp.dot(onehot.astype(BF16), upper, preferred_element_type=F32)
    before = base[:, 0:1]
    count = incl + before
    for k in range(TOP_K):
        mine = jnp.sum(jnp.where(row == idx[k:k + 1, :], count, 0.0), axis=0, keepdims=True)
        pos_ref[k:k + 1, :] = (mine - 1.0).astype(jnp.int32)
    total = before + incl[:, ROUTE_TILE - 1:ROUTE_TILE]
    base[...] = jnp.broadcast_to(total, base.shape)
    cnt_ref[...] = jnp.broadcast_to(total, cnt_ref.shape)


def _route(idx):
    t = idx.shape[1]
    return pl.pallas_call(
        _route_kernel,
        grid=(t // ROUTE_TILE,),
        in_specs=[pl.BlockSpec((TOP_K, ROUTE_TILE), lambda i: (0, i))],
        out_specs=[pl.BlockSpec((TOP_K, ROUTE_TILE), lambda i: (0, i)),
                   pl.BlockSpec((N_EXPERTS, LANES), lambda i: (0, 0))],
        out_shape=[jax.ShapeDtypeStruct((TOP_K, t), jnp.int32),
                   jax.ShapeDtypeStruct((N_EXPERTS, LANES), F32)],
        scratch_shapes=[pltpu.VMEM((N_EXPERTS, LANES), F32)],
        compiler_params=_params(("arbitrary",), 32),
        name="route",
    )(idx)


GROUP_TILE = 512
DISPATCH_TILE = 1024
ROW_WORDS = D_MODEL // 2


def _dispatch_kernel(slot_hbm, ztile_ref, nu_ref, *refs, steps):
    srcs = refs[:len(steps)]
    dst_hbm, slot_smem, zbuf, sem = refs[len(steps):]
    i = pl.program_id(0)
    n_slots = DISPATCH_TILE * TOP_K
    n_tiles = dst_hbm.shape[0] // GROUP_TILE
    load = pltpu.make_async_copy(slot_hbm.at[pl.ds(i * n_slots, n_slots)], slot_smem, sem.at[0])
    load.start()

    @pl.when(i == 0)
    def _():
        zbuf[...] = jnp.zeros_like(zbuf)

        def fill_tile(tile):
            pltpu.make_async_copy(zbuf, dst_hbm.at[pl.ds(tile * GROUP_TILE, GROUP_TILE), :], sem.at[1]).start()

        def fill_last(e, n):
            tile = ztile_ref[e]

            @pl.when(tile >= 0)
            def _():
                fill_tile(tile)

            return n + jnp.where(tile >= 0, 1, 0)

        def fill_unused(tile, c):
            fill_tile(tile)
            return c

        n_fill = lax.fori_loop(0, N_EXPERTS, fill_last, 0)
        lax.fori_loop(nu_ref[0], n_tiles, fill_unused, 0)

        def drain(_, c):
            pltpu.make_async_copy(zbuf, dst_hbm.at[pl.ds(0, GROUP_TILE), :], sem.at[1]).wait()
            return c

        lax.fori_loop(0, n_fill + n_tiles - nu_ref[0], drain, 0)

    load.wait()

    first = 0
    for src_hbm, n_steps in zip(srcs, steps):
        @pl.when((i >= first) & (i < first + n_steps))
        def _(src_hbm=src_hbm, first=first):
            def issue(t, c):
                src = src_hbm.at[pl.ds((i - first) * DISPATCH_TILE + t, 1), :]
                for k in range(TOP_K):
                    slot = slot_smem[t * TOP_K + k]
                    pltpu.make_async_copy(src, dst_hbm.at[pl.ds(slot, 1), :], sem.at[1]).start()
                return c

            lax.fori_loop(0, DISPATCH_TILE, issue, 0)

        first += n_steps
    pltpu.make_async_copy(dst_hbm.at[pl.ds(0, n_slots), :], dst_hbm.at[pl.ds(0, n_slots), :], sem.at[1]).wait()


def _dispatch(slots, ztile, n_used, sources, n_rows):
    steps = tuple(h.shape[0] // DISPATCH_TILE for h in sources)
    any_spec = pl.BlockSpec(memory_space=pl.ANY)
    smem_spec = pl.BlockSpec(memory_space=pltpu.SMEM)
    return pl.pallas_call(
        functools.partial(_dispatch_kernel, steps=steps),
        grid=(sum(steps),),
        in_specs=[any_spec, smem_spec, smem_spec] + [any_spec] * len(sources),
        out_specs=any_spec,
        out_shape=jax.ShapeDtypeStruct((n_rows, ROW_WORDS), jnp.uint32),
        scratch_shapes=[
            pltpu.SMEM((DISPATCH_TILE * TOP_K,), jnp.int32),
            pltpu.VMEM((GROUP_TILE, ROW_WORDS), jnp.uint32),
            pltpu.SemaphoreType.DMA((2,)),
        ],
        compiler_params=_params(("arbitrary",), 32),
        name="dispatch",
    )(slots, ztile, n_used, *sources)


def _expert_kernel(te_ref, ts_ref, nu_ref, x_ref, wgu_ref, bgu_ref, wd_ref, bd_ref, y_ref):
    j = pl.program_id(0)

    @pl.when(j < nu_ref[0])
    def _():
        lo, hi = _unpack_bf16_pair(x_ref[...])
        half = D_MODEL // 2
        gu = (jnp.dot(lo.astype(BF16), wgu_ref[0, :half, :], preferred_element_type=F32)
              + jnp.dot(hi.astype(BF16), wgu_ref[0, half:, :], preferred_element_type=F32) + bgu_ref[0])
        gate = jnp.minimum(gu[:, :D_FF], SWIGLU_LIMIT)
        up = jnp.clip(gu[:, D_FF:], -SWIGLU_LIMIT, SWIGLU_LIMIT)
        hid = (up + 1.0) * gate * _sigmoid(SWIGLU_ALPHA * gate)
        y = jnp.dot(hid.astype(BF16), wd_ref[0], preferred_element_type=F32) + bd_ref[0]
        y_ref[...] = _pack_bf16_pair(y[:, :half], y[:, half:])

    @pl.when(j >= nu_ref[0])
    def _():
        y_ref[...] = jnp.zeros_like(y_ref)


def _experts(tile_expert, tile_src, n_used, xs, wgu, bgu, wd, bd):
    n_rows = xs.shape[0]
    row_spec = pl.BlockSpec((GROUP_TILE, ROW_WORDS), lambda j, te, ts, nu: (ts[j], 0))
    out_spec = pl.BlockSpec((GROUP_TILE, ROW_WORDS), lambda j, te, ts, nu: (j, 0))
    return pl.pallas_call(
        _expert_kernel,
        grid_spec=pltpu.PrefetchScalarGridSpec(
            num_scalar_prefetch=3,
            grid=(n_rows // GROUP_TILE,),
            in_specs=[
                row_spec,
                pl.BlockSpec((1, D_MODEL, 2 * D_FF), lambda j, te, ts, nu: (te[j], 0, 0)),
                pl.BlockSpec((1, 1, 2 * D_FF), lambda j, te, ts, nu: (te[j], 0, 0)),
                pl.BlockSpec((1, D_FF, D_MODEL), lambda j, te, ts, nu: (te[j], 0, 0)),
                pl.BlockSpec((1, 1, D_MODEL), lambda j, te, ts, nu: (te[j], 0, 0)),
            ],
            out_specs=out_spec,
        ),
        out_shape=jax.ShapeDtypeStruct((n_rows, ROW_WORDS), jnp.uint32),
        compiler_params=_params(("arbitrary",), 56),
        name="experts",
    )(tile_expert, tile_src, n_used, xs, wgu, bgu, wd, bd)


COMBINE_TILE = 256


def _combine_kernel(slot_hbm, ys_hbm, x1_ref, wt_ref, nw_ref, o_ref, slot_smem, buf, sem):
    i = pl.program_id(0)
    n_slots = COMBINE_TILE * TOP_K
    load = pltpu.make_async_copy(slot_hbm.at[pl.ds(i * n_slots, n_slots)], slot_smem, sem.at[0])
    load.start()
    load.wait()

    def issue(t, c):
        for k in range(TOP_K):
            slot = slot_smem[t * TOP_K + k]
            pltpu.make_async_copy(ys_hbm.at[pl.ds(slot, 1), :], buf.at[k, pl.ds(t, 1), :], sem.at[1]).start()
        return c

    lax.fori_loop(0, COMBINE_TILE, issue, 0)
    pltpu.make_async_copy(ys_hbm.at[pl.ds(0, n_slots), :], ys_hbm.at[pl.ds(0, n_slots), :], sem.at[1]).wait()

    half = D_MODEL // 2
    wt = wt_ref[...]
    acc_lo = x1_ref[:, :half]
    acc_hi = x1_ref[:, half:]
    for k in range(TOP_K):
        lo, hi = _unpack_bf16_pair(buf[k])
        acc_lo = acc_lo + wt[:, k:k + 1] * lo
        acc_hi = acc_hi + wt[:, k:k + 1] * hi
    ms = (jnp.sum(acc_lo * acc_lo, axis=-1, keepdims=True)
          + jnp.sum(acc_hi * acc_hi, axis=-1, keepdims=True)) * (1.0 / D_MODEL)
    inv = lax.rsqrt(ms + RMS_EPS)
    o_ref[:, :half] = acc_lo * inv * nw_ref[:, :half]
    o_ref[:, half:] = acc_hi * inv * nw_ref[:, half:]


def _combine(slots, ys, x1, wt, nw):
    t = x1.shape[0]
    tm = COMBINE_TILE
    return pl.pallas_call(
        _combine_kernel,
        grid=(t // tm,),
        in_specs=[
            pl.BlockSpec(memory_space=pl.ANY),
            pl.BlockSpec(memory_space=pl.ANY),
            pl.BlockSpec((tm, D_MODEL), lambda i: (i, 0)),
            pl.BlockSpec((tm, TOP_K), lambda i: (i, 0)),
            pl.BlockSpec((1, D_MODEL), lambda i: (0, 0)),
        ],
        out_specs=pl.BlockSpec((tm, D_MODEL), lambda i: (i, 0)),
        out_shape=jax.ShapeDtypeStruct((t, D_MODEL), F32),
        scratch_shapes=[
            pltpu.SMEM((tm * TOP_K,), jnp.int32),
            pltpu.VMEM((TOP_K, tm, ROW_WORDS), jnp.uint32),
            pltpu.SemaphoreType.DMA((2,)),
        ],
        compiler_params=_params(("arbitrary",), 32),
        name="combine",
    )(slots, ys, x1, wt, nw)


def _moe(parts, p, final_w):
    sizes = [x1.shape[0] for x1, _, _, _ in parts]
    idx = jnp.concatenate([q[2] for q in parts], axis=1)
    t_all = idx.shape[1]
    pos, cnt = _route(idx)
    counts = cnt[:, 0].astype(jnp.int32)
    tiles_e = (counts + GROUP_TILE - 1) // GROUP_TILE
    tile_end = jnp.cumsum(tiles_e)
    tile_start = tile_end - tiles_e
    n_used = tile_end[-1]
    n_tiles = (t_all * TOP_K) // GROUP_TILE + N_EXPERTS
    n_rows = n_tiles * GROUP_TILE
    slots = ((tile_start * GROUP_TILE)[idx] + pos).T.reshape(-1)
    tile_id = jnp.arange(n_tiles, dtype=jnp.int32)
    tile_expert = jnp.minimum(jnp.searchsorted(tile_end, tile_id, side="right"), N_EXPERTS - 1).astype(jnp.int32)
    tile_src = jnp.minimum(tile_id, n_used - 1).astype(jnp.int32)
    tile_expert = tile_expert[tile_src]
    ztile = jnp.where(counts > 0, tile_end - 1, -1).astype(jnp.int32)

    n_used = n_used.reshape(1).astype(jnp.int32)
    xs = _dispatch(slots, ztile, n_used, [q[1] for q in parts], n_rows)
    ys = _experts(tile_expert, tile_src, n_used, xs, p["wgu"], p["bgu"], p["wd"], p["bd"])
    outs = []
    off = 0
    for (x1, _, _, wt), t in zip(parts, sizes):
        outs.append(_combine(lax.slice(slots, (off * TOP_K,), ((off + t) * TOP_K,)), ys, x1, wt.T, final_w))
        off += t
    return outs


def _rope_tables(seq):
    half = ATTN_HEAD_DIM // 2
    inv_freq = ROPE_THETA ** (-jnp.arange(half, dtype=F32) / half)
    ang = jnp.arange(seq, dtype=F32)[:, None] * inv_freq[None, :]
    cos, sin = jnp.cos(ang), jnp.sin(ang)
    reps = LANES // ATTN_HEAD_DIM
    cos_t = jnp.tile(jnp.concatenate([cos, cos], axis=1), (1, reps))
    sin_t = jnp.tile(jnp.concatenate([-sin, sin], axis=1), (1, reps))
    return cos_t, sin_t


def _pack_layer(w_in, conv_w, conv_b, gate_b, wpa, wpm, wo, nmw, nfw, rw, rb, wgu, bgu, wd, bd):
    c0 = ATTN_QKV_COLS
    c1 = c0 + ML_QKV_COLS
    c2 = c1 + ML_O_COLS
    c3 = c2 + ML_GATE_COLS
    w_main = jnp.concatenate([w_in[:, c0:c2], w_in[:, c3:], w_in[:, :c0]], axis=1).astype(BF16)
    w_gate = jnp.pad(w_in[:, c2:c3], ((0, 0), (0, LANES - ML_GATE_COLS))).astype(BF16)
    return dict(
        w_main=w_main, w_gate=w_gate, conv_w=conv_w, conv_b=conv_b.reshape(1, -1),
        gate_b=gate_b.reshape(-1, 1), wpa=wpa.astype(BF16), wpm=wpm.astype(BF16), wo=wo.astype(BF16),
        nmw=nmw.reshape(1, -1), nfw=nfw.reshape(1, -1), rw_t=rw.T, rb=rb.reshape(-1, 1),
        wgu=wgu.astype(BF16), bgu=bgu.reshape(N_EXPERTS, 1, -1), wd=wd.astype(BF16),
        bd=bd.reshape(N_EXPERTS, 1, -1))


def _layer(x, p, tables):
    b, s, _ = x.shape
    x2d = x.reshape(b * s, D_MODEL)
    proj, gates, qkv1, qkv2 = _inproj(x2d, p["nmw"], p["w_main"], p["w_gate"], tables[0], tables[1], s)
    proj3 = proj.reshape(b, s, N_MAIN)

    merged = _attn_plain(proj3)
    for g, qkv in ((1, qkv1), (2, qkv2)):
        merged = _attn_dilated(qkv, g, ATTN_PATTERNS[g][1], merged, b, s)
    attn_o = merged[0]

    qk = _conv_qk(proj3, p["conv_w"], p["conv_b"])
    gates_t = gates[:, :ML_GATE_COLS].reshape(b, s, ML_GATE_COLS).transpose(0, 2, 1)
    g_rows = _gate_prep(gates_t, p["gate_b"])
    g_cols = g_rows.transpose(0, 2, 1)
    h_f, h_b = _mlstm(proj3, qk, g_rows, g_cols)

    return _merge(x2d, attn_o, h_f.reshape(b * s, ML_DIM), h_b.reshape(b * s, ML_DIM), proj,
                  p["wpa"], p["wpm"], p["wo"], p["nfw"], p["rw_t"], p["rb"])


def kernel(x_prompt, x_sample, norm_mix_w, w_in, mlstm_conv_w, mlstm_conv_b, mlstm_gate_b, w_proj_attn,
           w_proj_mlstm, w_out, norm_ffn_w, router_w, router_b, expert_w_gu, expert_b_gu, expert_w_down,
           expert_b_down, norm_final_w):
    depth = w_in.shape[0]
    assert depth == 1, "the final RMSNorm is fused into the last layer's MoE kernel"
    p = _pack_layer(w_in[0], mlstm_conv_w[0], mlstm_conv_b[0], mlstm_gate_b[0], w_proj_attn[0],
                    w_proj_mlstm[0], w_out[0], norm_mix_w[0], norm_ffn_w[0], router_w[0], router_b[0],
                    expert_w_gu[0], expert_b_gu[0], expert_w_down[0], expert_b_down[0])
    final_w = norm_final_w.reshape(1, -1)
    xs = (x_prompt, x_sample)
    parts = [_layer(x, p, _rope_tables(x.shape[1])) for x in xs]
    outs = _moe(parts, p, final_w)
    return tuple(o.reshape(x.shape) for o, x in zip(outs, xs))
```

```python
import functools

import jax
import jax.numpy as jnp
from jax import lax
from jax.experimental import pallas as pl
from jax.experimental.pallas import tpu as pltpu

F32 = jnp.float32
BF16 = jnp.bfloat16

D_MODEL = 1024
ATTN_PATTERNS = ((128, 1), (512, 4), (2048, 16))
N_GROUPS = 3
ATTN_HEADS = 8
ATTN_HEAD_DIM = 64
ATTN_DIM = ATTN_HEADS * ATTN_HEAD_DIM
ATTN_BLOCK = 64
ROPE_THETA = 10000.0
ML_DIM = D_MODEL
ML_HEADS = 4
ML_HEAD_DIM = ML_DIM // ML_HEADS
ML_CHUNK = 128
CONV_W = 5
ATTN_QKV_COLS = N_GROUPS * 3 * ATTN_DIM
ML_QKV_COLS = 3 * ML_DIM
ML_O_COLS = ML_DIM
ML_GATE_COLS = 4 * ML_HEADS
MERGE_COLS = 2 * D_MODEL
N_EXPERTS = 32
TOP_K = 4
D_FF = D_MODEL
SWIGLU_LIMIT = 7.0
SWIGLU_ALPHA = 1.702
RMS_EPS = 1e-5
NEG_INF = -1e30

LANES = 128
COL_TILE = 512
N_PACKED = ML_QKV_COLS + ML_O_COLS + MERGE_COLS + ATTN_QKV_COLS
N_COL_TILES = N_PACKED // COL_TILE
ATTN_TILE0 = (ML_QKV_COLS + ML_O_COLS + MERGE_COLS) // COL_TILE
GROUP_TILES = 3 * ATTN_DIM // COL_TILE
N_MAIN_TILES = ATTN_TILE0 + GROUP_TILES
N_MAIN = N_MAIN_TILES * COL_TILE
ML_EXT = ML_HEAD_DIM + LANES


def _params(sem, vmem_mb):
    return pltpu.CompilerParams(dimension_semantics=sem, vmem_limit_bytes=vmem_mb * 1024 * 1024)


def _sigmoid(x):
    return 1.0 / (1.0 + jnp.exp(-x))


INPROJ_CHUNK = 512


def _inproj_kernel(x_ref, nw_ref, w_ref, wg_ref, cos_ref, sin_ref, out_ref, gates_ref, g1_ref, g2_ref,
                   h_scr, slab):
    j = pl.program_id(1)
    tm = x_ref.shape[0]
    n_slabs = COL_TILE // LANES

    @pl.when(j == 0)
    def _():
        x = x_ref[...]
        ms = jnp.mean(x * x, axis=-1, keepdims=True)
        h = (x * lax.rsqrt(ms + RMS_EPS) * nw_ref[...]).astype(BF16)
        h_scr[...] = h
        gates_ref[...] = jnp.dot(h, wg_ref[...], preferred_element_type=F32)

    chunk = min(INPROJ_CHUNK, tm)

    def chunks():
        for c in range(tm // chunk):
            rows = slice(c * chunk, (c + 1) * chunk)
            yield c, rows, jnp.dot(h_scr[rows, :], w_ref[...], preferred_element_type=F32)

    def rope(a, rows):
        c = jnp.concatenate([cos_ref[rows, :]] * n_slabs, axis=1)
        s = jnp.concatenate([sin_ref[rows, :]] * n_slabs, axis=1)
        lane = lax.broadcasted_iota(jnp.int32, a.shape, 1)
        half = ATTN_HEAD_DIM // 2
        first = (lane % ATTN_HEAD_DIM) < half
        sw = jnp.where(first, pltpu.roll(a, COL_TILE - half, 1), pltpu.roll(a, half, 1))
        return a * c + sw * s

    def deinterleave(val, c, rows, dst_ref, dilation):
        n = chunk // dilation
        for s in range(n_slabs):
            slab[s, rows, :] = val[:, s * LANES:(s + 1) * LANES]
        for r in range(dilation):
            piece = jnp.concatenate(
                [slab[s, pl.ds(c * chunk + r, n, stride=dilation), :] for s in range(n_slabs)], axis=1)
            dst_ref[0, r, c * n:(c + 1) * n, :] = piece.astype(BF16)

    @pl.when(j < ATTN_TILE0)
    def _():
        for _, rows, acc in chunks():
            out_ref[rows, :] = acc.astype(BF16)

    dst = (out_ref, g1_ref, g2_ref)
    for g, (_, dilation) in enumerate(ATTN_PATTERNS):
        for which in range(GROUP_TILES):
            @pl.when(j == ATTN_TILE0 + GROUP_TILES * g + which)
            def _(g=g, dilation=dilation, which=which):
                for c, rows, acc in chunks():
                    val = acc if which == 2 else rope(acc, rows)
                    if dilation == 1:
                        out_ref[rows, :] = val.astype(BF16)
                    else:
                        deinterleave(val, c, rows, dst[g], dilation)


def _inproj(x2d, nw, w_main, w_gate, cos_t, sin_t, seq):
    t = x2d.shape[0]
    tm = min(1024, seq)
    tiles_per_seq = seq // tm

    def group_out(g, dilation):
        first = ATTN_TILE0 + GROUP_TILES * g
        return (pl.BlockSpec((1, dilation, tm // dilation, COL_TILE),
                             lambda i, j: (i, 0, 0, jnp.clip(j - first, 0, GROUP_TILES - 1))),
                jax.ShapeDtypeStruct((t // tm, dilation, tm // dilation, 3 * ATTN_DIM), BF16))

    (g1_spec, g1_shape), (g2_spec, g2_shape) = [group_out(g, ATTN_PATTERNS[g][1]) for g in (1, 2)]
    return pl.pallas_call(
        _inproj_kernel,
        grid=(t // tm, N_COL_TILES),
        in_specs=[
            pl.BlockSpec((tm, D_MODEL), lambda i, j: (i, 0)),
            pl.BlockSpec((1, D_MODEL), lambda i, j: (0, 0)),
            pl.BlockSpec((D_MODEL, COL_TILE), lambda i, j: (0, j)),
            pl.BlockSpec((D_MODEL, LANES), lambda i, j: (0, 0)),
            pl.BlockSpec((tm, LANES), lambda i, j: (i % tiles_per_seq, 0)),
            pl.BlockSpec((tm, LANES), lambda i, j: (i % tiles_per_seq, 0)),
        ],
        out_specs=[
            pl.BlockSpec((tm, COL_TILE), lambda i, j: (i, jnp.minimum(j, N_MAIN_TILES - 1))),
            pl.BlockSpec((tm, LANES), lambda i, j: (i, 0)),
            g1_spec, g2_spec,
        ],
        out_shape=[
            jax.ShapeDtypeStruct((t, N_MAIN), BF16),
            jax.ShapeDtypeStruct((t, LANES), F32),
            g1_shape, g2_shape,
        ],
        scratch_shapes=[pltpu.VMEM((tm, D_MODEL), BF16), pltpu.VMEM((COL_TILE // LANES, tm, LANES), F32)],
        compiler_params=_params(("arbitrary", "arbitrary"), 56),
        name="inproj",
    )(x2d, nw, w_main, w_gate, cos_t, sin_t)


ATTN_QUERY_TILE = 128


def _window_mask(n, qt, sub_len):
    kt = qt + 2 * ATTN_BLOCK
    qpos = n * qt + lax.broadcasted_iota(jnp.int32, (qt, kt), 0)
    kpos = n * qt - ATTN_BLOCK + lax.broadcasted_iota(jnp.int32, (qt, kt), 1)
    return (jnp.abs(kpos - qpos) <= ATTN_BLOCK) & (kpos >= 0) & (kpos < sub_len)


def _attend(q, k, v, valid):
    qt = q.shape[0]
    pairs = range(ATTN_HEADS // 2)
    lane = lax.broadcasted_iota(jnp.int32, (qt, LANES), 1)
    lo = lane < ATTN_HEAD_DIM
    valid2 = jnp.concatenate([valid, valid], axis=0)
    scores = []
    for p in pairs:
        qp = q[:, p * LANES:(p + 1) * LANES]
        zero = jnp.zeros_like(qp)
        stacked = jnp.concatenate([jnp.where(lo, qp, zero), jnp.where(lo, zero, qp)], axis=0)
        scores.append(lax.dot_general(stacked, k[:, p * LANES:(p + 1) * LANES], (((1,), (1,)), ((), ())),
                                      preferred_element_type=F32))
    probs, inv_den, lses = [], [], []
    for p in pairs:
        s = jnp.where(valid2, scores[p], NEG_INF)
        mx = jnp.max(s, axis=-1, keepdims=True)
        e = jnp.exp(s - mx)
        den = jnp.sum(e, axis=-1, keepdims=True)
        probs.append(e.astype(BF16))
        inv_den.append(1.0 / den)
        lses.append(mx + jnp.log(den))
    pvs = [jnp.dot(probs[p], v[:, p * LANES:(p + 1) * LANES], preferred_element_type=F32) for p in pairs]
    lse_out = jnp.zeros((qt, LANES), F32)
    outs = []
    for p in pairs:
        o = pvs[p] * inv_den[p]
        outs.append(jnp.where(lo, o[:qt], o[qt:]))
        lse_out = jnp.where(lane == 2 * p, lses[p][:qt], jnp.where(lane == 2 * p + 1, lses[p][qt:], lse_out))
    return outs, lse_out


def _head_expander():
    row = lax.broadcasted_iota(jnp.int32, (LANES, ATTN_DIM), 0)
    col = lax.broadcasted_iota(jnp.int32, (LANES, ATTN_DIM), 1)
    return jnp.where(col // ATTN_HEAD_DIM == row, 1.0, 0.0).astype(BF16)


def _merge_groups(o_cur, lse_cur, o_prev, lse_prev, expander):
    m = jnp.maximum(lse_prev, lse_cur)
    a = jnp.exp(lse_prev - m)
    tot = a + jnp.exp(lse_cur - m)
    w_prev = a / tot
    hi = w_prev.astype(BF16)
    lo = (w_prev - hi.astype(F32)).astype(BF16)
    spread = (jnp.dot(hi, expander, preferred_element_type=F32) + jnp.dot(lo, expander, preferred_element_type=F32))
    outs = [o_cur[p] + spread[:, p * LANES:(p + 1) * LANES] * (o_prev[p] - o_cur[p])
            for p in range(ATTN_HEADS // 2)]
    lane = lax.broadcasted_iota(jnp.int32, m.shape, 1)
    return outs, jnp.where(lane < ATTN_HEADS, m + jnp.log(tot), 0.0)


def _attn_kernel(q_ref, kp_ref, kc_ref, kn_ref, vp_ref, vc_ref, vn_ref, o_ref, l_ref, *, sub_len, qt):
    valid = _window_mask(pl.program_id(1), qt, sub_len)
    k = jnp.concatenate([kp_ref[0], kc_ref[0], kn_ref[0]], axis=0)
    v = jnp.concatenate([vp_ref[0], vc_ref[0], vn_ref[0]], axis=0)
    outs, lse = _attend(q_ref[0], k, v, valid)
    for p, o in enumerate(outs):
        o_ref[0, :, p * LANES:(p + 1) * LANES] = o.astype(BF16)
    l_ref[0] = lse


MERGE_ROWS = 256


def _attn_dilated_kernel(q_ref, kp_ref, kc_ref, kn_ref, vp_ref, vc_ref, vn_ref, op_ref, lp_ref, o_ref, l_ref,
                         o_slab, l_slab, *, sub_len, qt, dilation):
    valid = _window_mask(pl.program_id(1), qt, sub_len)

    def residue(r, c):
        k = jnp.concatenate([kp_ref[0, r], kc_ref[0, r], kn_ref[0, r]], axis=0)
        v = jnp.concatenate([vp_ref[0, r], vc_ref[0, r], vn_ref[0, r]], axis=0)
        outs, lse = _attend(q_ref[0, r], k, v, valid)
        for p, o in enumerate(outs):
            o_slab[p, pl.ds(r, qt, stride=dilation), :] = o
        l_slab[pl.ds(r, qt, stride=dilation), :] = lse
        return c

    lax.fori_loop(0, dilation, residue, 0)
    expander = _head_expander()

    def merge(i, c):
        rows = pl.ds(pl.multiple_of(i * MERGE_ROWS, MERGE_ROWS), MERGE_ROWS)
        prev = op_ref[rows, :].astype(F32)
        o_cur = [o_slab[p, rows, :] for p in range(ATTN_HEADS // 2)]
        o_prev = [prev[:, p * LANES:(p + 1) * LANES] for p in range(ATTN_HEADS // 2)]
        outs, lse = _merge_groups(o_cur, l_slab[rows, :], o_prev, lp_ref[rows, :], expander)
        for p, o in enumerate(outs):
            o_ref[rows, p * LANES:(p + 1) * LANES] = o.astype(BF16)
        l_ref[rows, :] = lse
        return c

    lax.fori_loop(0, qt * dilation // MERGE_ROWS, merge, 0)


def _attn_plain(proj3):
    b, s, _ = proj3.shape
    qt = min(ATTN_QUERY_TILE, s)
    nblk = s // ATTN_BLOCK
    qb = qt // ATTN_BLOCK

    def cur(which):
        return pl.BlockSpec((1, qt, COL_TILE), lambda bi, n: (bi, n, ATTN_TILE0 + which))

    def before(which):
        return pl.BlockSpec((1, ATTN_BLOCK, COL_TILE),
                            lambda bi, n: (bi, jnp.maximum(n * qb - 1, 0), ATTN_TILE0 + which))

    def after(which):
        return pl.BlockSpec((1, ATTN_BLOCK, COL_TILE),
                            lambda bi, n: (bi, jnp.minimum((n + 1) * qb, nblk - 1), ATTN_TILE0 + which))

    o, l = pl.pallas_call(
        functools.partial(_attn_kernel, sub_len=s, qt=qt),
        grid=(b, s // qt),
        in_specs=[cur(0), before(1), cur(1), after(1), before(2), cur(2), after(2)],
        out_specs=[pl.BlockSpec((1, qt, ATTN_DIM), lambda bi, n: (bi, n, 0)),
                   pl.BlockSpec((1, qt, LANES), lambda bi, n: (bi, n, 0))],
        out_shape=[jax.ShapeDtypeStruct((b, s, ATTN_DIM), BF16), jax.ShapeDtypeStruct((b, s, LANES), F32)],
        compiler_params=_params(("arbitrary", "arbitrary"), 48),
        name="attn_g0",
    )(*([proj3] * 7))
    return o.reshape(b * s, ATTN_DIM), l.reshape(b * s, LANES)


def _attn_dilated(qkv, g, dilation, prev, b, s):
    n_tiles, _, rows, _ = qkv.shape
    tiles_per_seq = n_tiles // b
    sub_len = s // dilation
    qt = min(ATTN_QUERY_TILE, rows)
    span = qt * dilation
    q_per_tile = rows // qt
    h_per_tile = rows // ATTN_BLOCK
    qb = qt // ATTN_BLOCK
    nblk = sub_len // ATTN_BLOCK

    def cur(which):
        return pl.BlockSpec((1, dilation, qt, COL_TILE),
                            lambda bi, n: (bi * tiles_per_seq + n // q_per_tile, 0, n % q_per_tile, which))

    def halo(which, blk_of):
        def index(bi, n):
            blk = blk_of(n)
            return (bi * tiles_per_seq + blk // h_per_tile, 0, blk % h_per_tile, which)
        return pl.BlockSpec((1, dilation, ATTN_BLOCK, COL_TILE), index)

    before = lambda which: halo(which, lambda n: jnp.maximum(n * qb - 1, 0))
    after = lambda which: halo(which, lambda n: jnp.minimum((n + 1) * qb, nblk - 1))
    steps = sub_len // qt
    o_spec = pl.BlockSpec((span, ATTN_DIM), lambda bi, n: (bi * steps + n, 0))
    l_spec = pl.BlockSpec((span, LANES), lambda bi, n: (bi * steps + n, 0))
    return pl.pallas_call(
        functools.partial(_attn_dilated_kernel, sub_len=sub_len, qt=qt, dilation=dilation),
        grid=(b, steps),
        in_specs=[cur(0), before(1), cur(1), after(1), before(2), cur(2), after(2), o_spec, l_spec],
        out_specs=[o_spec, l_spec],
        out_shape=[jax.ShapeDtypeStruct((b * s, ATTN_DIM), BF16), jax.ShapeDtypeStruct((b * s, LANES), F32)],
        scratch_shapes=[pltpu.VMEM((ATTN_HEADS // 2, span, LANES), F32), pltpu.VMEM((span, LANES), F32)],
        compiler_params=_params(("arbitrary", "arbitrary"), 56),
        name=f"attn_g{g}",
    )(*([qkv] * 7), prev[0], prev[1])


CONV_HALO = 16


Q_TILES = ML_DIM // COL_TILE


def _conv_kernel(xp_ref, xc_ref, xn_ref, w_ref, b_ref, q_ref, kt_ref, buf, *, tm):
    i = pl.program_id(1)
    c = pl.program_id(2)
    last = pl.num_programs(1) - 1
    buf[0:CONV_HALO, :] = jnp.where(i > 0, xp_ref[0].astype(F32), 0.0)
    buf[CONV_HALO:CONV_HALO + tm, :] = xc_ref[0].astype(F32)
    buf[CONV_HALO + tm:, :] = jnp.where(i < last, xn_ref[0].astype(F32), 0.0)
    w = w_ref[...]
    y = jnp.broadcast_to(b_ref[...], (tm, COL_TILE))
    for tap in range(CONV_W):
        y = y + w[tap:tap + 1, :] * buf[pl.ds(CONV_HALO - CONV_W // 2 + tap, tm), :]
    y = y * _sigmoid(y)

    @pl.when(c < Q_TILES)
    def _():
        q_ref[0] = y.astype(BF16)

    @pl.when(c >= Q_TILES)
    def _():
        kt_ref[0] = (y * ML_HEAD_DIM ** -0.5).T.astype(BF16)


def _conv_qk(proj3, conv_w, conv_b):
    b, s, _ = proj3.shape
    tm = min(1024, s)
    hb = tm // CONV_HALO
    nh = s // CONV_HALO
    return pl.pallas_call(
        functools.partial(_conv_kernel, tm=tm),
        grid=(b, s // tm, 2 * Q_TILES),
        in_specs=[
            pl.BlockSpec((1, CONV_HALO, COL_TILE), lambda bi, i, c: (bi, jnp.maximum(i * hb - 1, 0), c)),
            pl.BlockSpec((1, tm, COL_TILE), lambda bi, i, c: (bi, i, c)),
            pl.BlockSpec((1, CONV_HALO, COL_TILE), lambda bi, i, c: (bi, jnp.minimum((i + 1) * hb, nh - 1), c)),
            pl.BlockSpec((CONV_W, COL_TILE), lambda bi, i, c: (0, c)),
            pl.BlockSpec((1, COL_TILE), lambda bi, i, c: (0, c)),
        ],
        out_specs=[
            pl.BlockSpec((1, tm, COL_TILE), lambda bi, i, c: (bi, i, jnp.minimum(c, Q_TILES - 1))),
            pl.BlockSpec((1, COL_TILE, tm), lambda bi, i, c: (bi, jnp.maximum(c - Q_TILES, 0), i)),
        ],
        out_shape=[jax.ShapeDtypeStruct((b, s, ML_DIM), BF16), jax.ShapeDtypeStruct((b, ML_DIM, s), BF16)],
        scratch_shapes=[pltpu.VMEM((tm + 2 * CONV_HALO, COL_TILE), F32)],
        compiler_params=_params(("arbitrary", "arbitrary", "arbitrary"), 32),
        name="conv_qk",
    )(proj3, proj3, proj3, conv_w, conv_b)


def _gate_kernel(g_ref, b_ref, o_ref):
    g = g_ref[0] + b_ref[...]
    width = g.shape[1]
    logsig = jnp.minimum(g, 0.0) - jnp.log(1.0 + jnp.exp(-jnp.abs(g)))
    lane = lax.broadcasted_iota(jnp.int32, g.shape, 1) % ML_CHUNK
    pre = logsig
    suf = logsig
    step = 1
    while step < ML_CHUNK:
        pre = pre + jnp.where(lane >= step, pltpu.roll(pre, step, 1), 0.0)
        suf = suf + jnp.where(lane < ML_CHUNK - step, pltpu.roll(suf, width - step, 1), 0.0)
        step *= 2
    row = lax.broadcasted_iota(jnp.int32, g.shape, 0)
    is_f_fwd = (row >= ML_HEADS) & (row < 2 * ML_HEADS)
    is_f_bwd = row >= 3 * ML_HEADS
    o_ref[0] = jnp.where(is_f_fwd, pre, jnp.where(is_f_bwd, suf, g))


def _gate_prep(gates_t, gate_b):
    b, rows, s = gates_t.shape
    sb = min(2048, s)
    return pl.pallas_call(
        _gate_kernel,
        grid=(b, s // sb),
        in_specs=[
            pl.BlockSpec((1, rows, sb), lambda bi, i: (bi, 0, i)),
            pl.BlockSpec((rows, 1), lambda bi, i: (0, 0)),
        ],
        out_specs=pl.BlockSpec((1, rows, sb), lambda bi, i: (bi, 0, i)),
        out_shape=jax.ShapeDtypeStruct((b, rows, s), F32),
        compiler_params=_params(("arbitrary", "arbitrary"), 32),
        name="gate_prep",
    )(gates_t, gate_b)


def _mlstm_kernel(qf_ref, kf_ref, vf_ref, qb_ref, kb_ref, vb_ref, grf_ref, grb_ref, gcf_ref, gcb_ref,
                  hf_ref, hb_ref, s_scr, m_scr):
    c = pl.program_id(1)

    @pl.when(c == 0)
    def _():
        s_scr[...] = jnp.zeros_like(s_scr)
        m_scr[...] = jnp.zeros_like(m_scr)

    t_i = lax.broadcasted_iota(jnp.int32, (ML_CHUNK, ML_CHUNK), 0)
    s_i = lax.broadcasted_iota(jnp.int32, (ML_CHUNK, ML_CHUNK), 1)
    ones_col = jnp.where(lax.broadcasted_iota(jnp.int32, (ML_CHUNK, LANES), 1) == 0, 1.0, 0.0).astype(BF16)
    dirs = ((qf_ref, kf_ref, vf_ref, grf_ref, gcf_ref, hf_ref), (qb_ref, kb_ref, vb_ref, grb_ref, gcb_ref, hb_ref))
    chains = [(dirn, head) + refs for dirn, refs in enumerate(dirs) for head in range(ML_HEADS)]

    matmuls = []
    for dirn, head, q_ref, k_ref, v_ref, _, _, _ in chains:
        hs = slice(head * ML_HEAD_DIM, (head + 1) * ML_HEAD_DIM)
        q = q_ref[0, :, hs]
        qk_raw = jnp.dot(q, k_ref[0, hs, :], preferred_element_type=F32)
        q_state = jnp.dot(q, s_scr[dirn * ML_HEADS + head].astype(BF16), preferred_element_type=F32)
        matmuls.append((qk_raw, q_state))

    weights = []
    for dirn, head, _, _, _, gr_ref, gc_ref, _ in chains:
        ii = dirn * 2 * ML_HEADS + head
        bi = ii + ML_HEADS
        mask = (s_i <= t_i) if dirn == 0 else (s_i >= t_i)
        i_row, b_row = gr_ref[0, ii:ii + 1, :], gr_ref[0, bi:bi + 1, :]
        b_col = gc_ref[0, :, bi:bi + 1]
        m = m_scr[dirn * ML_HEADS + head][0:1, 0:1]
        dmat = jnp.where(mask, b_col - b_row + i_row, NEG_INF)
        inter = b_col + m
        m_t = jnp.maximum(inter, jnp.max(dmat, axis=-1, keepdims=True))
        weights.append((jnp.exp(dmat - m_t), jnp.exp(inter - m_t), jnp.exp(-m_t)))

    for (dirn, head, _, _, v_ref, _, _, h_ref), (qk_raw, q_state), (w_intra, w_inter, floor) in zip(
            chains, matmuls, weights):
        hs = slice(head * ML_HEAD_DIM, (head + 1) * ML_HEAD_DIM)
        v_ext = jnp.concatenate([v_ref[0, :, hs], ones_col], axis=1)
        num = w_inter * q_state + jnp.dot((qk_raw * w_intra).astype(BF16), v_ext, preferred_element_type=F32)
        den = jnp.maximum(jnp.abs(num[:, ML_HEAD_DIM:ML_HEAD_DIM + 1]), floor)
        h_ref[0, :, hs] = (num[:, :ML_HEAD_DIM] / den).astype(BF16)

    for dirn, head, _, k_ref, v_ref, gr_ref, _, _ in chains:
        idx = dirn * ML_HEADS + head
        ii = dirn * 2 * ML_HEADS + head
        bi = ii + ML_HEADS
        hs = slice(head * ML_HEAD_DIM, (head + 1) * ML_HEAD_DIM)
        i_row, b_row = gr_ref[0, ii:ii + 1, :], gr_ref[0, bi:bi + 1, :]
        b_last = b_row[:, ML_CHUNK - 1:ML_CHUNK] if dirn == 0 else b_row[:, 0:1]
        m = m_scr[idx][0:1, 0:1]
        log_w = b_last - b_row + i_row
        m_new = jnp.maximum(b_last + m, jnp.max(log_w, axis=-1, keepdims=True))
        decay = jnp.exp(b_last + m - m_new)
        v_ext = jnp.concatenate([v_ref[0, :, hs], ones_col], axis=1)
        wk_t = (k_ref[0, hs, :].astype(F32) * jnp.exp(log_w - m_new)).astype(BF16)
        s_scr[idx] = decay * s_scr[idx] + jnp.dot(wk_t, v_ext, preferred_element_type=F32)
        m_scr[idx] = jnp.broadcast_to(m_new, m_scr.shape[1:])


def _mlstm(proj3, q, k_t, g_rows, g_cols):
    b, s, _ = proj3.shape
    nc = s // ML_CHUNK
    v_tile = 2 * ML_DIM // ML_DIM
    n_rows = g_rows.shape[1]

    def rows(col, reverse):
        return pl.BlockSpec((1, ML_CHUNK, ML_DIM), lambda bi, c: (bi, nc - 1 - c if reverse else c, col))

    def cols(height, reverse):
        return pl.BlockSpec((1, height, ML_CHUNK), lambda bi, c: (bi, 0, nc - 1 - c if reverse else c))

    return pl.pallas_call(
        _mlstm_kernel,
        grid=(b, nc),
        in_specs=[
            rows(0, False), cols(ML_DIM, False), rows(v_tile, False),
            rows(0, True), cols(ML_DIM, True), rows(v_tile, True),
            cols(n_rows, False), cols(n_rows, True),
            pl.BlockSpec((1, ML_CHUNK, n_rows), lambda bi, c: (bi, c, 0)),
            pl.BlockSpec((1, ML_CHUNK, n_rows), lambda bi, c: (bi, nc - 1 - c, 0)),
        ],
        out_specs=[rows(0, False), rows(0, True)],
        out_shape=[jax.ShapeDtypeStruct((b, s, ML_DIM), BF16)] * 2,
        scratch_shapes=[
            pltpu.VMEM((2 * ML_HEADS, ML_HEAD_DIM, ML_EXT), F32),
            pltpu.VMEM((2 * ML_HEADS, 8, LANES), F32),
        ],
        compiler_params=_params(("arbitrary", "arbitrary"), 32),
        name="mlstm",
    )(q, k_t, proj3, q, k_t, proj3, g_rows, g_rows, g_cols, g_cols)


def _pack_bf16_pair(lo, hi):
    lo_bits = lax.bitcast_convert_type(lo.astype(BF16).astype(F32), jnp.uint32)
    hi_bits = lax.bitcast_convert_type(hi.astype(BF16).astype(F32), jnp.uint32)
    return (hi_bits & jnp.uint32(0xFFFF0000)) | (lo_bits >> 16)


def _unpack_bf16_pair(packed):
    lo = lax.bitcast_convert_type(packed << 16, F32)
    hi = lax.bitcast_convert_type(packed & jnp.uint32(0xFFFF0000), F32)
    return lo, hi


def _merge_kernel(x_ref, ao_ref, hf_ref, hb_ref, mo_ref, mg_ref, wpa_ref, wpm_ref, wo_ref, nfw_ref,
                  rw_ref, rb_ref, x1_ref, h2_ref, idx_ref, wt_ref):
    y_attn = jnp.dot(ao_ref[...], wpa_ref[...], preferred_element_type=F32)
    hsum = hf_ref[...].astype(F32) + hb_ref[...].astype(F32)
    ml = (_sigmoid(mo_ref[...].astype(F32)) * hsum).astype(BF16)
    y_ml = jnp.dot(ml, wpm_ref[...], preferred_element_type=F32)
    gates = _sigmoid(mg_ref[...].astype(F32))
    mixed = (gates[:, :D_MODEL] * y_attn + gates[:, D_MODEL:] * y_ml).astype(BF16)
    x1 = x_ref[...] + jnp.dot(mixed, wo_ref[...], preferred_element_type=F32)
    x1_ref[...] = x1
    ms = jnp.mean(x1 * x1, axis=-1, keepdims=True)
    h2 = x1 * lax.rsqrt(ms + RMS_EPS) * nfw_ref[...]
    h2_ref[...] = _pack_bf16_pair(h2[:, :D_MODEL // 2], h2[:, D_MODEL // 2:])

    logits = lax.dot_general(rw_ref[...], h2, (((1,), (1,)), ((), ())), preferred_element_type=F32,
                             precision=lax.Precision.HIGHEST) + rb_ref[...]
    row = lax.broadcasted_iota(jnp.int32, logits.shape, 0)
    rest = logits
    vals = []
    for k in range(TOP_K):
        mx = jnp.max(rest, axis=0, keepdims=True)
        first = jnp.min(jnp.where(rest == mx, row, N_EXPERTS), axis=0, keepdims=True)
        vals.append(mx)
        idx_ref[k:k + 1, :] = first
        rest = jnp.where(row == first, -jnp.inf, rest)
    exps = [jnp.exp(v - vals[0]) for v in vals]
    tot = exps[0] + exps[1] + exps[2] + exps[3]
    for k in range(TOP_K):
        wt_ref[k:k + 1, :] = exps[k] / tot


def _merge(x2d, attn_o, h_f, h_b, proj, wpa, wpm, wo, nfw, rw_t, rb):
    t = x2d.shape[0]
    tm = min(512, t)
    ml_o_tile = ML_QKV_COLS // ML_O_COLS
    merge_tile = (ML_QKV_COLS + ML_O_COLS) // MERGE_COLS
    row = lambda width: pl.BlockSpec((tm, width), lambda i: (i, 0))
    full = lambda a: pl.BlockSpec(a.shape, lambda i: (0, 0))
    return pl.pallas_call(
        _merge_kernel,
        grid=(t // tm,),
        in_specs=[
            row(D_MODEL), row(ATTN_DIM), row(ML_DIM), row(ML_DIM),
            pl.BlockSpec((tm, ML_O_COLS), lambda i: (i, ml_o_tile)),
            pl.BlockSpec((tm, MERGE_COLS), lambda i: (i, merge_tile)),
            full(wpa), full(wpm), full(wo), full(nfw), full(rw_t), full(rb),
        ],
        out_specs=[row(D_MODEL), row(D_MODEL // 2), pl.BlockSpec((TOP_K, tm), lambda i: (0, i)),
                   pl.BlockSpec((TOP_K, tm), lambda i: (0, i))],
        out_shape=[
            jax.ShapeDtypeStruct((t, D_MODEL), F32),
            jax.ShapeDtypeStruct((t, D_MODEL // 2), jnp.uint32),
            jax.ShapeDtypeStruct((TOP_K, t), jnp.int32),
            jax.ShapeDtypeStruct((TOP_K, t), F32),
        ],
        compiler_params=_params(("arbitrary",), 48),
        name="merge",
    )(x2d, attn_o, h_f, h_b, proj, proj, wpa, wpm, wo, nfw, rw_t, rb)


ROUTE_TILE = 512


def _route_kernel(idx_ref, pos_ref, cnt_ref, base):
    @pl.when(pl.program_id(0) == 0)
    def _():
        base[...] = jnp.zeros_like(base)

    idx = idx_ref[...]
    row = lax.broadcasted_iota(jnp.int32, (N_EXPERTS, ROUTE_TILE), 0)
    onehot = jnp.zeros((N_EXPERTS, ROUTE_TILE), F32)
    for k in range(TOP_K):
        onehot = onehot + jnp.where(row == idx[k:k + 1, :], 1.0, 0.0)
    s_i = lax.broadcasted_iota(jnp.int32, (ROUTE_TILE, ROUTE_TILE), 0)
    t_i = lax.broadcasted_iota(jnp.int32, (ROUTE_TILE, ROUTE_TILE), 1)
    upper = jnp.where(s_i <= t_i, 1.0, 0.0).astype(BF16)
    incl = jnp.dot(onehot.astype(BF16), upper, preferred_element_type=F32)
    before = base[:, 0:1]
    count = incl + before
    for k in range(TOP_K):
        mine = jnp.sum(jnp.where(row == idx[k:k + 1, :], count, 0.0), axis=0, keepdims=True)
        pos_ref[k:k + 1, :] = (mine - 1.0).astype(jnp.int32)
    total = before + incl[:, ROUTE_TILE - 1:ROUTE_TILE]
    base[...] = jnp.broadcast_to(total, base.shape)
    cnt_ref[...] = jnp.broadcast_to(total, cnt_ref.shape)


def _route(idx):
    t = idx.shape[1]
    return pl.pallas_call(
        _route_kernel,
        grid=(t // ROUTE_TILE,),
        in_specs=[pl.BlockSpec((TOP_K, ROUTE_TILE), lambda i: (0, i))],
        out_specs=[pl.BlockSpec((TOP_K, ROUTE_TILE), lambda i: (0, i)),
                   pl.BlockSpec((N_EXPERTS, LANES), lambda i: (0, 0))],
        out_shape=[jax.ShapeDtypeStruct((TOP_K, t), jnp.int32),
                   jax.ShapeDtypeStruct((N_EXPERTS, LANES), F32)],
        scratch_shapes=[pltpu.VMEM((N_EXPERTS, LANES), F32)],
        compiler_params=_params(("arbitrary",), 32),
        name="route",
    )(idx)


GROUP_TILE = 512
DISPATCH_TILE = 1024
ROW_WORDS = D_MODEL // 2


def _dispatch_kernel(slot_hbm, ztile_ref, nu_ref, *refs, steps):
    srcs = refs[:len(steps)]
    dst_hbm, slot_smem, zbuf, sem = refs[len(steps):]
    i = pl.program_id(0)
    n_slots = DISPATCH_TILE * TOP_K
    n_tiles = dst_hbm.shape[0] // GROUP_TILE
    load = pltpu.make_async_copy(slot_hbm.at[pl.ds(i * n_slots, n_slots)], slot_smem, sem.at[0])
    load.start()

    @pl.when(i == 0)
    def _():
        zbuf[...] = jnp.zeros_like(zbuf)

        def fill_tile(tile):
            pltpu.make_async_copy(zbuf, dst_hbm.at[pl.ds(tile * GROUP_TILE, GROUP_TILE), :], sem.at[1]).start()

        def fill_last(e, n):
            tile = ztile_ref[e]

            @pl.when(tile >= 0)
            def _():
                fill_tile(tile)

            return n + jnp.where(tile >= 0, 1, 0)

        def fill_unused(tile, c):
            fill_tile(tile)
            return c

        n_fill = lax.fori_loop(0, N_EXPERTS, fill_last, 0)
        lax.fori_loop(nu_ref[0], n_tiles, fill_unused, 0)

        def drain(_, c):
            pltpu.make_async_copy(zbuf, dst_hbm.at[pl.ds(0, GROUP_TILE), :], sem.at[1]).wait()
            return c

        lax.fori_loop(0, n_fill + n_tiles - nu_ref[0], drain, 0)

    load.wait()

    first = 0
    for src_ref, n_steps in zip(srcs, steps):
        @pl.when((i >= first) & (i < first + n_steps))
        def _(src_ref=src_ref):
            def issue(t, c):
                src = src_ref.at[pl.ds(t, 1), :]
                for k in range(TOP_K):
                    slot = slot_smem[t * TOP_K + k]
                    pltpu.make_async_copy(src, dst_hbm.at[pl.ds(slot, 1), :], sem.at[1]).start()
                return c

            lax.fori_loop(0, DISPATCH_TILE, issue, 0)

        first += n_steps
    pltpu.make_async_copy(dst_hbm.at[pl.ds(0, n_slots), :], dst_hbm.at[pl.ds(0, n_slots), :], sem.at[1]).wait()


def _dispatch(slots, ztile, n_used, sources, n_rows):
    steps = tuple(h.shape[0] // DISPATCH_TILE for h in sources)
    any_spec = pl.BlockSpec(memory_space=pl.ANY)
    smem_spec = pl.BlockSpec(memory_space=pltpu.SMEM)
    src_specs = []
    first = 0
    for n_steps in steps:
        src_specs.append(pl.BlockSpec((DISPATCH_TILE, ROW_WORDS),
                                      lambda i, first=first, n_steps=n_steps: (jnp.clip(i - first, 0, n_steps - 1), 0)))
        first += n_steps
    return pl.pallas_call(
        functools.partial(_dispatch_kernel, steps=steps),
        grid=(sum(steps),),
        in_specs=[any_spec, smem_spec, smem_spec] + src_specs,
        out_specs=any_spec,
        out_shape=jax.ShapeDtypeStruct((n_rows, ROW_WORDS), jnp.uint32),
        scratch_shapes=[
            pltpu.SMEM((DISPATCH_TILE * TOP_K,), jnp.int32),
            pltpu.VMEM((GROUP_TILE, ROW_WORDS), jnp.uint32),
            pltpu.SemaphoreType.DMA((2,)),
        ],
        compiler_params=_params(("arbitrary",), 32),
        name="dispatch",
    )(slots, ztile, n_used, *sources)


def _expert_kernel(te_ref, ts_ref, nu_ref, x_ref, wgu_ref, bgu_ref, wd_ref, bd_ref, y_ref):
    j = pl.program_id(0)

    @pl.when(j < nu_ref[0])
    def _():
        lo, hi = _unpack_bf16_pair(x_ref[...])
        half = D_MODEL // 2
        gu = (jnp.dot(lo.astype(BF16), wgu_ref[0, :half, :], preferred_element_type=F32)
              + jnp.dot(hi.astype(BF16), wgu_ref[0, half:, :], preferred_element_type=F32) + bgu_ref[0])
        gate = jnp.minimum(gu[:, :D_FF], SWIGLU_LIMIT)
        up = jnp.clip(gu[:, D_FF:], -SWIGLU_LIMIT, SWIGLU_LIMIT)
        hid = (up + 1.0) * gate * _sigmoid(SWIGLU_ALPHA * gate)
        y = jnp.dot(hid.astype(BF16), wd_ref[0], preferred_element_type=F32) + bd_ref[0]
        y_ref[...] = _pack_bf16_pair(y[:, :half], y[:, half:])

    @pl.when(j >= nu_ref[0])
    def _():
        y_ref[...] = jnp.zeros_like(y_ref)


def _experts(tile_expert, tile_src, n_used, xs, wgu, bgu, wd, bd):
    n_rows = xs.shape[0]
    row_spec = pl.BlockSpec((GROUP_TILE, ROW_WORDS), lambda j, te, ts, nu: (ts[j], 0))
    out_spec = pl.BlockSpec((GROUP_TILE, ROW_WORDS), lambda j, te, ts, nu: (j, 0))
    return pl.pallas_call(
        _expert_kernel,
        grid_spec=pltpu.PrefetchScalarGridSpec(
            num_scalar_prefetch=3,
            grid=(n_rows // GROUP_TILE,),
            in_specs=[
                row_spec,
                pl.BlockSpec((1, D_MODEL, 2 * D_FF), lambda j, te, ts, nu: (te[j], 0, 0)),
                pl.BlockSpec((1, 1, 2 * D_FF), lambda j, te, ts, nu: (te[j], 0, 0)),
                pl.BlockSpec((1, D_FF, D_MODEL), lambda j, te, ts, nu: (te[j], 0, 0)),
                pl.BlockSpec((1, 1, D_MODEL), lambda j, te, ts, nu: (te[j], 0, 0)),
            ],
            out_specs=out_spec,
        ),
        out_shape=jax.ShapeDtypeStruct((n_rows, ROW_WORDS), jnp.uint32),
        compiler_params=_params(("arbitrary",), 56),
        name="experts",
    )(tile_expert, tile_src, n_used, xs, wgu, bgu, wd, bd)


COMBINE_TILE = 256


def _combine_kernel(slot_hbm, ys_hbm, x1_ref, wt_ref, nw_ref, o_ref, slot_smem, buf, sem):
    i = pl.program_id(0)
    n_slots = COMBINE_TILE * TOP_K
    load = pltpu.make_async_copy(slot_hbm.at[pl.ds(i * n_slots, n_slots)], slot_smem, sem.at[0])
    load.start()
    load.wait()

    def issue(t, c):
        for k in range(TOP_K):
            slot = slot_smem[t * TOP_K + k]
            pltpu.make_async_copy(ys_hbm.at[pl.ds(slot, 1), :], buf.at[k, pl.ds(t, 1), :], sem.at[1]).start()
        return c

    lax.fori_loop(0, COMBINE_TILE, issue, 0)
    pltpu.make_async_copy(ys_hbm.at[pl.ds(0, n_slots), :], ys_hbm.at[pl.ds(0, n_slots), :], sem.at[1]).wait()

    half = D_MODEL // 2
    wt = wt_ref[...]
    acc_lo = x1_ref[:, :half]
    acc_hi = x1_ref[:, half:]
    for k in range(TOP_K):
        lo, hi = _unpack_bf16_pair(buf[k])
        acc_lo = acc_lo + wt[:, k:k + 1] * lo
        acc_hi = acc_hi + wt[:, k:k + 1] * hi
    ms = (jnp.sum(acc_lo * acc_lo, axis=-1, keepdims=True)
          + jnp.sum(acc_hi * acc_hi, axis=-1, keepdims=True)) * (1.0 / D_MODEL)
    inv = lax.rsqrt(ms + RMS_EPS)
    o_ref[:, :half] = acc_lo * inv * nw_ref[:, :half]
    o_ref[:, half:] = acc_hi * inv * nw_ref[:, half:]


def _combine(slots, ys, x1, wt, nw):
    t = x1.shape[0]
    tm = COMBINE_TILE
    return pl.pallas_call(
        _combine_kernel,
        grid=(t // tm,),
        in_specs=[
            pl.BlockSpec(memory_space=pl.ANY),
            pl.BlockSpec(memory_space=pl.ANY),
            pl.BlockSpec((tm, D_MODEL), lambda i: (i, 0)),
            pl.BlockSpec((tm, TOP_K), lambda i: (i, 0)),
            pl.BlockSpec((1, D_MODEL), lambda i: (0, 0)),
        ],
        out_specs=pl.BlockSpec((tm, D_MODEL), lambda i: (i, 0)),
        out_shape=jax.ShapeDtypeStruct((t, D_MODEL), F32),
        scratch_shapes=[
            pltpu.SMEM((tm * TOP_K,), jnp.int32),
            pltpu.VMEM((TOP_K, tm, ROW_WORDS), jnp.uint32),
            pltpu.SemaphoreType.DMA((2,)),
        ],
        compiler_params=_params(("arbitrary",), 32),
        name="combine",
    )(slots, ys, x1, wt, nw)


def _moe(parts, p, final_w):
    sizes = [x1.shape[0] for x1, _, _, _ in parts]
    idx = jnp.concatenate([q[2] for q in parts], axis=1)
    t_all = idx.shape[1]
    pos, cnt = _route(idx)
    counts = cnt[:, 0].astype(jnp.int32)
    tiles_e = (counts + GROUP_TILE - 1) // GROUP_TILE
    tile_end = jnp.cumsum(tiles_e)
    tile_start = tile_end - tiles_e
    n_used = tile_end[-1]
    n_tiles = (t_all * TOP_K) // GROUP_TILE + N_EXPERTS
    n_rows = n_tiles * GROUP_TILE
    first_row = tile_start * GROUP_TILE
    experts = jnp.arange(N_EXPERTS, dtype=jnp.int32)[:, None, None]
    base = jnp.sum(jnp.where(idx[None] == experts, first_row[:, None, None], 0), axis=0)
    slots = (base + pos).T.reshape(-1)
    tile_id = jnp.arange(n_tiles, dtype=jnp.int32)
    tile_src = jnp.minimum(tile_id, n_used - 1).astype(jnp.int32)
    tile_expert = jnp.sum(tile_src[:, None] >= tile_end[None, :], axis=1).astype(jnp.int32)
    tile_expert = jnp.minimum(tile_expert, N_EXPERTS - 1)
    ztile = jnp.where(counts > 0, tile_end - 1, -1).astype(jnp.int32)

    n_used = n_used.reshape(1).astype(jnp.int32)
    xs = _dispatch(slots, ztile, n_used, [q[1] for q in parts], n_rows)
    ys = _experts(tile_expert, tile_src, n_used, xs, p["wgu"], p["bgu"], p["wd"], p["bd"])
    outs = []
    off = 0
    for (x1, _, _, wt), t in zip(parts, sizes):
        outs.append(_combine(lax.slice(slots, (off * TOP_K,), ((off + t) * TOP_K,)), ys, x1, wt.T, final_w))
        off += t
    return outs


def _rope_tables(seq):
    half = ATTN_HEAD_DIM // 2
    inv_freq = ROPE_THETA ** (-jnp.arange(half, dtype=F32) / half)
    ang = jnp.arange(seq, dtype=F32)[:, None] * inv_freq[None, :]
    cos, sin = jnp.cos(ang), jnp.sin(ang)
    reps = LANES // ATTN_HEAD_DIM
    cos_t = jnp.tile(jnp.concatenate([cos, cos], axis=1), (1, reps))
    sin_t = jnp.tile(jnp.concatenate([-sin, sin], axis=1), (1, reps))
    return cos_t, sin_t


def _pack_layer(w_in, conv_w, conv_b, gate_b, wpa, wpm, wo, nmw, nfw, rw, rb, wgu, bgu, wd, bd):
    c0 = ATTN_QKV_COLS
    c1 = c0 + ML_QKV_COLS
    c2 = c1 + ML_O_COLS
    c3 = c2 + ML_GATE_COLS
    w_attn = w_in[:, :c0].reshape(D_MODEL, N_GROUPS, 3, ATTN_DIM)
    w_attn = w_attn * jnp.array([ATTN_HEAD_DIM ** -0.5, 1.0, 1.0], F32)[None, None, :, None]
    w_main = jnp.concatenate([w_in[:, c0:c2], w_in[:, c3:], w_attn.reshape(D_MODEL, c0)], axis=1).astype(BF16)
    w_gate = jnp.pad(w_in[:, c2:c3], ((0, 0), (0, LANES - ML_GATE_COLS))).astype(BF16)
    return dict(
        w_main=w_main, w_gate=w_gate, conv_w=conv_w, conv_b=conv_b.reshape(1, -1),
        gate_b=gate_b.reshape(-1, 1), wpa=wpa.astype(BF16), wpm=wpm.astype(BF16), wo=wo.astype(BF16),
        nmw=nmw.reshape(1, -1), nfw=nfw.reshape(1, -1), rw_t=rw.T, rb=rb.reshape(-1, 1),
        wgu=wgu.astype(BF16), bgu=bgu.reshape(N_EXPERTS, 1, -1), wd=wd.astype(BF16),
        bd=bd.reshape(N_EXPERTS, 1, -1))


def _layer(x, p, tables):
    b, s, _ = x.shape
    x2d = x.reshape(b * s, D_MODEL)
    proj, gates, qkv1, qkv2 = _inproj(x2d, p["nmw"], p["w_main"], p["w_gate"], tables[0], tables[1], s)
    proj3 = proj.reshape(b, s, N_MAIN)

    merged = _attn_plain(proj3)
    for g, qkv in ((1, qkv1), (2, qkv2)):
        merged = _attn_dilated(qkv, g, ATTN_PATTERNS[g][1], merged, b, s)
    attn_o = merged[0]

    ml_q, ml_kt = _conv_qk(proj3, p["conv_w"], p["conv_b"])
    gates_t = gates[:, :ML_GATE_COLS].reshape(b, s, ML_GATE_COLS).transpose(0, 2, 1)
    g_rows = _gate_prep(gates_t, p["gate_b"])
    g_cols = g_rows.transpose(0, 2, 1)
    h_f, h_b = _mlstm(proj3, ml_q, ml_kt, g_rows, g_cols)

    return _merge(x2d, attn_o, h_f.reshape(b * s, ML_DIM), h_b.reshape(b * s, ML_DIM), proj,
                  p["wpa"], p["wpm"], p["wo"], p["nfw"], p["rw_t"], p["rb"])


def kernel(x_prompt, x_sample, norm_mix_w, w_in, mlstm_conv_w, mlstm_conv_b, mlstm_gate_b, w_proj_attn,
           w_proj_mlstm, w_out, norm_ffn_w, router_w, router_b, expert_w_gu, expert_b_gu, expert_w_down,
           expert_b_down, norm_final_w):
    depth = w_in.shape[0]
    assert depth == 1, "the final RMSNorm is fused into the last layer's MoE kernel"
    p = _pack_layer(w_in[0], mlstm_conv_w[0], mlstm_conv_b[0], mlstm_gate_b[0], w_proj_attn[0],
                    w_proj_mlstm[0], w_out[0], norm_mix_w[0], norm_ffn_w[0], router_w[0], router_b[0],
                    expert_w_gu[0], expert_b_gu[0], expert_w_down[0], expert_b_down[0])
    final_w = norm_final_w.reshape(1, -1)
    xs = (x_prompt, x_sample)
    parts = [_layer(x, p, _rope_tables(x.shape[1])) for x in xs]
    outs = _moe(parts, p, final_w)
    return tuple(o.reshape(x.shape) for o, x in zip(outs, xs))
```

```python
import functools

import jax
import jax.numpy as jnp
from jax import lax
from jax.experimental import pallas as pl
from jax.experimental.pallas import tpu as pltpu

F32 = jnp.float32
BF16 = jnp.bfloat16

D_MODEL = 1024
ATTN_PATTERNS = ((128, 1), (512, 4), (2048, 16))
N_GROUPS = 3
ATTN_HEADS = 8
ATTN_HEAD_DIM = 64
ATTN_DIM = ATTN_HEADS * ATTN_HEAD_DIM
ATTN_BLOCK = 64
ROPE_THETA = 10000.0
ML_DIM = D_MODEL
ML_HEADS = 4
ML_HEAD_DIM = ML_DIM // ML_HEADS
ML_CHUNK = 128
CONV_W = 5
ATTN_QKV_COLS = N_GROUPS * 3 * ATTN_DIM
ML_QKV_COLS = 3 * ML_DIM
ML_O_COLS = ML_DIM
ML_GATE_COLS = 4 * ML_HEADS
MERGE_COLS = 2 * D_MODEL
N_EXPERTS = 32
TOP_K = 4
D_FF = D_MODEL
SWIGLU_LIMIT = 7.0
SWIGLU_ALPHA = 1.702
RMS_EPS = 1e-5
NEG_INF = -1e30

LANES = 128
COL_TILE = 512
N_PACKED = ML_QKV_COLS + ML_O_COLS + MERGE_COLS + ATTN_QKV_COLS
N_COL_TILES = N_PACKED // COL_TILE
ATTN_TILE0 = (ML_QKV_COLS + ML_O_COLS + MERGE_COLS) // COL_TILE
GROUP_TILES = 3 * ATTN_DIM // COL_TILE
N_MAIN_TILES = ATTN_TILE0 + GROUP_TILES
N_MAIN = N_MAIN_TILES * COL_TILE
ML_EXT = ML_HEAD_DIM + LANES


def _params(sem, vmem_mb):
    return pltpu.CompilerParams(dimension_semantics=sem, vmem_limit_bytes=vmem_mb * 1024 * 1024)


def _sigmoid(x):
    return 1.0 / (1.0 + jnp.exp(-x))


INPROJ_CHUNK = 512
WIDE_TILE = 3 * ATTN_DIM
N_WIDE_TILES = N_PACKED // WIDE_TILE
N_PLAIN_WIDE = ATTN_TILE0 * COL_TILE // WIDE_TILE


def _inproj_kernel(x_ref, nw_ref, w_ref, wg_ref, cos_ref, sin_ref, out_ref, gates_ref, g1_ref, g2_ref,
                   h_scr, slab):
    j = pl.program_id(1)
    tm = x_ref.shape[0]
    n_slabs = WIDE_TILE // LANES

    @pl.when(j == 0)
    def _():
        x = x_ref[...]
        ms = jnp.mean(x * x, axis=-1, keepdims=True)
        h = (x * lax.rsqrt(ms + RMS_EPS) * nw_ref[...]).astype(BF16)
        h_scr[...] = h
        gates_ref[...] = jnp.dot(h, wg_ref[...], preferred_element_type=F32)

    chunk = min(INPROJ_CHUNK, tm)

    def chunks():
        for c in range(tm // chunk):
            rows = slice(c * chunk, (c + 1) * chunk)
            yield c, rows, jnp.dot(h_scr[rows, :], w_ref[...], preferred_element_type=F32)

    def rope(a, rows):
        reps = ATTN_DIM // LANES
        c = jnp.concatenate([cos_ref[rows, :]] * reps, axis=1)
        s = jnp.concatenate([sin_ref[rows, :]] * reps, axis=1)
        lane = lax.broadcasted_iota(jnp.int32, a.shape, 1)
        half = ATTN_HEAD_DIM // 2
        first = (lane % ATTN_HEAD_DIM) < half
        sw = jnp.where(first, pltpu.roll(a, ATTN_DIM - half, 1), pltpu.roll(a, half, 1))
        return a * c + sw * s

    def rotated(acc, rows):
        return jnp.concatenate([rope(acc[:, :ATTN_DIM], rows), rope(acc[:, ATTN_DIM:2 * ATTN_DIM], rows),
                                acc[:, 2 * ATTN_DIM:]], axis=1)

    def deinterleave(val, c, dst_ref, dilation):
        n = chunk // dilation
        for s in range(n_slabs):
            slab[s] = val[:, s * LANES:(s + 1) * LANES]
        for r in range(dilation):
            piece = jnp.concatenate([slab[s, pl.ds(r, n, stride=dilation), :] for s in range(n_slabs)], axis=1)
            dst_ref[0, r, c * n:(c + 1) * n, :] = piece.astype(BF16)

    @pl.when(j < N_PLAIN_WIDE)
    def _():
        for _, rows, acc in chunks():
            out_ref[rows, :] = acc.astype(BF16)

    dst = (out_ref, g1_ref, g2_ref)
    for g, (_, dilation) in enumerate(ATTN_PATTERNS):
        @pl.when(j == N_PLAIN_WIDE + g)
        def _(g=g, dilation=dilation):
            for c, rows, acc in chunks():
                val = rotated(acc, rows)
                if dilation == 1:
                    out_ref[rows, :] = val.astype(BF16)
                else:
                    deinterleave(val, c, dst[g], dilation)


def _inproj(x2d, nw, w_main, w_gate, cos_t, sin_t, seq):
    t = x2d.shape[0]
    tm = min(1024, seq)
    tiles_per_seq = seq // tm
    chunk = min(INPROJ_CHUNK, tm)

    def group_out(dilation):
        return (pl.BlockSpec((1, dilation, tm // dilation, WIDE_TILE), lambda i, j: (i, 0, 0, 0)),
                jax.ShapeDtypeStruct((t // tm, dilation, tm // dilation, WIDE_TILE), BF16))

    (g1_spec, g1_shape), (g2_spec, g2_shape) = [group_out(ATTN_PATTERNS[g][1]) for g in (1, 2)]
    return pl.pallas_call(
        _inproj_kernel,
        grid=(t // tm, N_WIDE_TILES),
        in_specs=[
            pl.BlockSpec((tm, D_MODEL), lambda i, j: (i, 0)),
            pl.BlockSpec((1, D_MODEL), lambda i, j: (0, 0)),
            pl.BlockSpec((D_MODEL, WIDE_TILE), lambda i, j: (0, j)),
            pl.BlockSpec((D_MODEL, LANES), lambda i, j: (0, 0)),
            pl.BlockSpec((tm, LANES), lambda i, j: (i % tiles_per_seq, 0)),
            pl.BlockSpec((tm, LANES), lambda i, j: (i % tiles_per_seq, 0)),
        ],
        out_specs=[
            pl.BlockSpec((tm, WIDE_TILE), lambda i, j: (i, jnp.minimum(j, N_PLAIN_WIDE))),
            pl.BlockSpec((tm, LANES), lambda i, j: (i, 0)),
            g1_spec, g2_spec,
        ],
        out_shape=[
            jax.ShapeDtypeStruct((t, N_MAIN), BF16),
            jax.ShapeDtypeStruct((t, LANES), F32),
            g1_shape, g2_shape,
        ],
        scratch_shapes=[pltpu.VMEM((tm, D_MODEL), BF16), pltpu.VMEM((WIDE_TILE // LANES, chunk, LANES), F32)],
        compiler_params=_params(("arbitrary", "arbitrary"), 56),
        name="inproj",
    )(x2d, nw, w_main, w_gate, cos_t, sin_t)


ATTN_QUERY_TILE = 128


def _window_mask(n, qt, sub_len):
    kt = qt + 2 * ATTN_BLOCK
    qpos = n * qt + lax.broadcasted_iota(jnp.int32, (qt, kt), 0)
    kpos = n * qt - ATTN_BLOCK + lax.broadcasted_iota(jnp.int32, (qt, kt), 1)
    return (jnp.abs(kpos - qpos) <= ATTN_BLOCK) & (kpos >= 0) & (kpos < sub_len)


def _attend(q, k, v, valid):
    qt = q.shape[0]
    pairs = range(ATTN_HEADS // 2)
    lane = lax.broadcasted_iota(jnp.int32, (qt, LANES), 1)
    lo = lane < ATTN_HEAD_DIM
    valid2 = jnp.concatenate([valid, valid], axis=0)
    scores = []
    for p in pairs:
        qp = q[:, p * LANES:(p + 1) * LANES]
        zero = jnp.zeros_like(qp)
        stacked = jnp.concatenate([jnp.where(lo, qp, zero), jnp.where(lo, zero, qp)], axis=0)
        scores.append(lax.dot_general(stacked, k[:, p * LANES:(p + 1) * LANES], (((1,), (1,)), ((), ())),
                                      preferred_element_type=F32))
    probs, inv_den, lses = [], [], []
    for p in pairs:
        s = jnp.where(valid2, scores[p], NEG_INF)
        mx = jnp.max(s, axis=-1, keepdims=True)
        e = jnp.exp(s - mx)
        den = jnp.sum(e, axis=-1, keepdims=True)
        probs.append(e.astype(BF16))
        inv_den.append(1.0 / den)
        lses.append(mx + jnp.log(den))
    pvs = [jnp.dot(probs[p], v[:, p * LANES:(p + 1) * LANES], preferred_element_type=F32) for p in pairs]
    lse_out = jnp.zeros((qt, LANES), F32)
    outs = []
    for p in pairs:
        o = pvs[p] * inv_den[p]
        outs.append(jnp.where(lo, o[:qt], o[qt:]))
        lse_out = jnp.where(lane == 2 * p, lses[p][:qt], jnp.where(lane == 2 * p + 1, lses[p][qt:], lse_out))
    return outs, lse_out


def _head_expander():
    row = lax.broadcasted_iota(jnp.int32, (LANES, ATTN_DIM), 0)
    col = lax.broadcasted_iota(jnp.int32, (LANES, ATTN_DIM), 1)
    return jnp.where(col // ATTN_HEAD_DIM == row, 1.0, 0.0).astype(BF16)


def _merge_groups(o_cur, lse_cur, o_prev, lse_prev, expander):
    m = jnp.maximum(lse_prev, lse_cur)
    a = jnp.exp(lse_prev - m)
    tot = a + jnp.exp(lse_cur - m)
    w_prev = a / tot
    hi = w_prev.astype(BF16)
    lo = (w_prev - hi.astype(F32)).astype(BF16)
    spread = (jnp.dot(hi, expander, preferred_element_type=F32) + jnp.dot(lo, expander, preferred_element_type=F32))
    outs = [o_cur[p] + spread[:, p * LANES:(p + 1) * LANES] * (o_prev[p] - o_cur[p])
            for p in range(ATTN_HEADS // 2)]
    lane = lax.broadcasted_iota(jnp.int32, m.shape, 1)
    return outs, jnp.where(lane < ATTN_HEADS, m + jnp.log(tot), 0.0)


def _attn_kernel(q_ref, kp_ref, kc_ref, kn_ref, vp_ref, vc_ref, vn_ref, o_ref, l_ref, *, sub_len, qt):
    valid = _window_mask(pl.program_id(1), qt, sub_len)
    k = jnp.concatenate([kp_ref[0], kc_ref[0], kn_ref[0]], axis=0)
    v = jnp.concatenate([vp_ref[0], vc_ref[0], vn_ref[0]], axis=0)
    outs, lse = _attend(q_ref[0], k, v, valid)
    for p, o in enumerate(outs):
        o_ref[0, :, p * LANES:(p + 1) * LANES] = o.astype(BF16)
    l_ref[0] = lse


MERGE_ROWS = 256


def _attn_dilated_kernel(q_ref, kp_ref, kc_ref, kn_ref, vp_ref, vc_ref, vn_ref, op_ref, lp_ref, o_ref, l_ref,
                         o_slab, l_slab, *, sub_len, qt, dilation):
    valid = _window_mask(pl.program_id(1), qt, sub_len)

    def residue(r, c):
        k = jnp.concatenate([kp_ref[0, r], kc_ref[0, r], kn_ref[0, r]], axis=0)
        v = jnp.concatenate([vp_ref[0, r], vc_ref[0, r], vn_ref[0, r]], axis=0)
        outs, lse = _attend(q_ref[0, r], k, v, valid)
        for p, o in enumerate(outs):
            o_slab[p, pl.ds(r, qt, stride=dilation), :] = o
        l_slab[pl.ds(r, qt, stride=dilation), :] = lse
        return c

    lax.fori_loop(0, dilation, residue, 0)
    expander = _head_expander()

    def merge(i, c):
        rows = pl.ds(pl.multiple_of(i * MERGE_ROWS, MERGE_ROWS), MERGE_ROWS)
        prev = op_ref[rows, :].astype(F32)
        o_cur = [o_slab[p, rows, :] for p in range(ATTN_HEADS // 2)]
        o_prev = [prev[:, p * LANES:(p + 1) * LANES] for p in range(ATTN_HEADS // 2)]
        outs, lse = _merge_groups(o_cur, l_slab[rows, :], o_prev, lp_ref[rows, :], expander)
        for p, o in enumerate(outs):
            o_ref[rows, p * LANES:(p + 1) * LANES] = o.astype(BF16)
        l_ref[rows, :] = lse
        return c

    lax.fori_loop(0, qt * dilation // MERGE_ROWS, merge, 0)


def _attn_plain(proj3):
    b, s, _ = proj3.shape
    qt = min(ATTN_QUERY_TILE, s)
    nblk = s // ATTN_BLOCK
    qb = qt // ATTN_BLOCK

    def cur(which):
        return pl.BlockSpec((1, qt, COL_TILE), lambda bi, n: (bi, n, ATTN_TILE0 + which))

    def before(which):
        return pl.BlockSpec((1, ATTN_BLOCK, COL_TILE),
                            lambda bi, n: (bi, jnp.maximum(n * qb - 1, 0), ATTN_TILE0 + which))

    def after(which):
        return pl.BlockSpec((1, ATTN_BLOCK, COL_TILE),
                            lambda bi, n: (bi, jnp.minimum((n + 1) * qb, nblk - 1), ATTN_TILE0 + which))

    o, l = pl.pallas_call(
        functools.partial(_attn_kernel, sub_len=s, qt=qt),
        grid=(b, s // qt),
        in_specs=[cur(0), before(1), cur(1), after(1), before(2), cur(2), after(2)],
        out_specs=[pl.BlockSpec((1, qt, ATTN_DIM), lambda bi, n: (bi, n, 0)),
                   pl.BlockSpec((1, qt, LANES), lambda bi, n: (bi, n, 0))],
        out_shape=[jax.ShapeDtypeStruct((b, s, ATTN_DIM), BF16), jax.ShapeDtypeStruct((b, s, LANES), F32)],
        compiler_params=_params(("arbitrary", "arbitrary"), 48),
        name="attn_g0",
    )(*([proj3] * 7))
    return o.reshape(b * s, ATTN_DIM), l.reshape(b * s, LANES)


def _attn_dilated(qkv, g, dilation, prev, b, s):
    n_tiles, _, rows, _ = qkv.shape
    tiles_per_seq = n_tiles // b
    sub_len = s // dilation
    qt = min(ATTN_QUERY_TILE, rows)
    span = qt * dilation
    q_per_tile = rows // qt
    h_per_tile = rows // ATTN_BLOCK
    qb = qt // ATTN_BLOCK
    nblk = sub_len // ATTN_BLOCK

    def cur(which):
        return pl.BlockSpec((1, dilation, qt, COL_TILE),
                            lambda bi, n: (bi * tiles_per_seq + n // q_per_tile, 0, n % q_per_tile, which))

    def halo(which, blk_of):
        def index(bi, n):
            blk = blk_of(n)
            return (bi * tiles_per_seq + blk // h_per_tile, 0, blk % h_per_tile, which)
        return pl.BlockSpec((1, dilation, ATTN_BLOCK, COL_TILE), index)

    before = lambda which: halo(which, lambda n: jnp.maximum(n * qb - 1, 0))
    after = lambda which: halo(which, lambda n: jnp.minimum((n + 1) * qb, nblk - 1))
    steps = sub_len // qt
    o_spec = pl.BlockSpec((span, ATTN_DIM), lambda bi, n: (bi * steps + n, 0))
    l_spec = pl.BlockSpec((span, LANES), lambda bi, n: (bi * steps + n, 0))
    return pl.pallas_call(
        functools.partial(_attn_dilated_kernel, sub_len=sub_len, qt=qt, dilation=dilation),
        grid=(b, steps),
        in_specs=[cur(0), before(1), cur(1), after(1), before(2), cur(2), after(2), o_spec, l_spec],
        out_specs=[o_spec, l_spec],
        out_shape=[jax.ShapeDtypeStruct((b * s, ATTN_DIM), BF16), jax.ShapeDtypeStruct((b * s, LANES), F32)],
        scratch_shapes=[pltpu.VMEM((ATTN_HEADS // 2, span, LANES), F32), pltpu.VMEM((span, LANES), F32)],
        compiler_params=_params(("arbitrary", "arbitrary"), 56),
        name=f"attn_g{g}",
    )(*([qkv] * 7), prev[0], prev[1])


CONV_HALO = 16


Q_TILES = ML_DIM // COL_TILE


def _conv_kernel(xp_ref, xc_ref, xn_ref, w_ref, b_ref, q_ref, kt_ref, buf, *, tm):
    i = pl.program_id(1)
    c = pl.program_id(2)
    last = pl.num_programs(1) - 1
    buf[0:CONV_HALO, :] = jnp.where(i > 0, xp_ref[0].astype(F32), 0.0)
    buf[CONV_HALO:CONV_HALO + tm, :] = xc_ref[0].astype(F32)
    buf[CONV_HALO + tm:, :] = jnp.where(i < last, xn_ref[0].astype(F32), 0.0)
    w = w_ref[...]
    y = jnp.broadcast_to(b_ref[...], (tm, COL_TILE))
    for tap in range(CONV_W):
        y = y + w[tap:tap + 1, :] * buf[pl.ds(CONV_HALO - CONV_W // 2 + tap, tm), :]
    y = y * _sigmoid(y)

    @pl.when(c < Q_TILES)
    def _():
        q_ref[0] = y.astype(BF16)

    @pl.when(c >= Q_TILES)
    def _():
        kt_ref[0] = (y * ML_HEAD_DIM ** -0.5).T.astype(BF16)


def _conv_qk(proj3, conv_w, conv_b):
    b, s, _ = proj3.shape
    tm = min(1024, s)
    hb = tm // CONV_HALO
    nh = s // CONV_HALO
    return pl.pallas_call(
        functools.partial(_conv_kernel, tm=tm),
        grid=(b, s // tm, 2 * Q_TILES),
        in_specs=[
            pl.BlockSpec((1, CONV_HALO, COL_TILE), lambda bi, i, c: (bi, jnp.maximum(i * hb - 1, 0), c)),
            pl.BlockSpec((1, tm, COL_TILE), lambda bi, i, c: (bi, i, c)),
            pl.BlockSpec((1, CONV_HALO, COL_TILE), lambda bi, i, c: (bi, jnp.minimum((i + 1) * hb, nh - 1), c)),
            pl.BlockSpec((CONV_W, COL_TILE), lambda bi, i, c: (0, c)),
            pl.BlockSpec((1, COL_TILE), lambda bi, i, c: (0, c)),
        ],
        out_specs=[
            pl.BlockSpec((1, tm, COL_TILE), lambda bi, i, c: (bi, i, jnp.minimum(c, Q_TILES - 1))),
            pl.BlockSpec((1, COL_TILE, tm), lambda bi, i, c: (bi, jnp.maximum(c - Q_TILES, 0), i)),
        ],
        out_shape=[jax.ShapeDtypeStruct((b, s, ML_DIM), BF16), jax.ShapeDtypeStruct((b, ML_DIM, s), BF16)],
        scratch_shapes=[pltpu.VMEM((tm + 2 * CONV_HALO, COL_TILE), F32)],
        compiler_params=_params(("arbitrary", "arbitrary", "arbitrary"), 32),
        name="conv_qk",
    )(proj3, proj3, proj3, conv_w, conv_b)


def _gate_kernel(g_ref, b_ref, o_ref):
    g = g_ref[0] + b_ref[...]
    width = g.shape[1]
    logsig = jnp.minimum(g, 0.0) - jnp.log(1.0 + jnp.exp(-jnp.abs(g)))
    lane = lax.broadcasted_iota(jnp.int32, g.shape, 1) % ML_CHUNK
    pre = logsig
    suf = logsig
    step = 1
    while step < ML_CHUNK:
        pre = pre + jnp.where(lane >= step, pltpu.roll(pre, step, 1), 0.0)
        suf = suf + jnp.where(lane < ML_CHUNK - step, pltpu.roll(suf, width - step, 1), 0.0)
        step *= 2
    row = lax.broadcasted_iota(jnp.int32, g.shape, 0)
    is_f_fwd = (row >= ML_HEADS) & (row < 2 * ML_HEADS)
    is_f_bwd = row >= 3 * ML_HEADS
    o_ref[0] = jnp.where(is_f_fwd, pre, jnp.where(is_f_bwd, suf, g))


def _gate_prep(gates_t, gate_b):
    b, rows, s = gates_t.shape
    sb = min(2048, s)
    return pl.pallas_call(
        _gate_kernel,
        grid=(b, s // sb),
        in_specs=[
            pl.BlockSpec((1, rows, sb), lambda bi, i: (bi, 0, i)),
            pl.BlockSpec((rows, 1), lambda bi, i: (0, 0)),
        ],
        out_specs=pl.BlockSpec((1, rows, sb), lambda bi, i: (bi, 0, i)),
        out_shape=jax.ShapeDtypeStruct((b, rows, s), F32),
        compiler_params=_params(("arbitrary", "arbitrary"), 32),
        name="gate_prep",
    )(gates_t, gate_b)


ML_CHAIN_GROUP = 4


def _mlstm_kernel(qf_ref, kf_ref, vf_ref, qb_ref, kb_ref, vb_ref, grf_ref, grb_ref, gcf_ref, gcb_ref,
                  hf_ref, hb_ref, s_scr, m_scr):
    c = pl.program_id(1)

    @pl.when(c == 0)
    def _():
        def clear(i, carry):
            s_scr[i] = jnp.zeros(s_scr.shape[1:], F32)
            return carry

        lax.fori_loop(0, s_scr.shape[0], clear, 0)
        m_scr[...] = jnp.zeros_like(m_scr)

    t_i = lax.broadcasted_iota(jnp.int32, (ML_CHUNK, ML_CHUNK), 0)
    s_i = lax.broadcasted_iota(jnp.int32, (ML_CHUNK, ML_CHUNK), 1)
    ones_col = jnp.where(lax.broadcasted_iota(jnp.int32, (ML_CHUNK, LANES), 1) == 0, 1.0, 0.0).astype(BF16)
    dirs = ((qf_ref, kf_ref, vf_ref, grf_ref, gcf_ref, hf_ref), (qb_ref, kb_ref, vb_ref, grb_ref, gcb_ref, hb_ref))
    all_chains = [(dirn, head) + refs for dirn, refs in enumerate(dirs) for head in range(ML_HEADS)]

    for first in range(0, len(all_chains), ML_CHAIN_GROUP):
        chains = all_chains[first:first + ML_CHAIN_GROUP]
        matmuls = []
        for dirn, head, q_ref, k_ref, v_ref, _, _, _ in chains:
            hs = slice(head * ML_HEAD_DIM, (head + 1) * ML_HEAD_DIM)
            q = q_ref[0, :, hs]
            qk_raw = jnp.dot(q, k_ref[0, hs, :], preferred_element_type=F32)
            q_state = jnp.dot(q, s_scr[dirn * ML_HEADS + head].astype(BF16), preferred_element_type=F32)
            matmuls.append((qk_raw, q_state))

        weights = []
        for dirn, head, _, _, _, gr_ref, gc_ref, _ in chains:
            ii = dirn * 2 * ML_HEADS + head
            bi = ii + ML_HEADS
            mask = (s_i <= t_i) if dirn == 0 else (s_i >= t_i)
            i_row, b_row = gr_ref[0, ii:ii + 1, :], gr_ref[0, bi:bi + 1, :]
            b_col = gc_ref[0, :, bi:bi + 1]
            m = m_scr[dirn * ML_HEADS + head][0:1, 0:1]
            dmat = jnp.where(mask, b_col - b_row + i_row, NEG_INF)
            inter = b_col + m
            m_t = jnp.maximum(inter, jnp.max(dmat, axis=-1, keepdims=True))
            weights.append((jnp.exp(dmat - m_t), jnp.exp(inter - m_t), jnp.exp(-m_t)))

        for (dirn, head, _, _, v_ref, _, _, h_ref), (qk_raw, q_state), (w_intra, w_inter, floor) in zip(
                chains, matmuls, weights):
            hs = slice(head * ML_HEAD_DIM, (head + 1) * ML_HEAD_DIM)
            v_ext = jnp.concatenate([v_ref[0, :, hs], ones_col], axis=1)
            num = w_inter * q_state + jnp.dot((qk_raw * w_intra).astype(BF16), v_ext, preferred_element_type=F32)
            den = jnp.maximum(jnp.abs(num[:, ML_HEAD_DIM:ML_HEAD_DIM + 1]), floor)
            h_ref[0, :, hs] = (num[:, :ML_HEAD_DIM] / den).astype(BF16)

        for dirn, head, _, k_ref, v_ref, gr_ref, _, _ in chains:
            idx = dirn * ML_HEADS + head
            ii = dirn * 2 * ML_HEADS + head
            bi = ii + ML_HEADS
            hs = slice(head * ML_HEAD_DIM, (head + 1) * ML_HEAD_DIM)
            i_row, b_row = gr_ref[0, ii:ii + 1, :], gr_ref[0, bi:bi + 1, :]
            b_last = b_row[:, ML_CHUNK - 1:ML_CHUNK] if dirn == 0 else b_row[:, 0:1]
            m = m_scr[idx][0:1, 0:1]
            log_w = b_last - b_row + i_row
            m_new = jnp.maximum(b_last + m, jnp.max(log_w, axis=-1, keepdims=True))
            decay = jnp.exp(b_last + m - m_new)
            v_ext = jnp.concatenate([v_ref[0, :, hs], ones_col], axis=1)
            wk_t = (k_ref[0, hs, :].astype(F32) * jnp.exp(log_w - m_new)).astype(BF16)
            s_scr[idx] = decay * s_scr[idx] + jnp.dot(wk_t, v_ext, preferred_element_type=F32)
            m_scr[idx] = jnp.broadcast_to(m_new, m_scr.shape[1:])


def _mlstm(proj3, q, k_t, g_rows, g_cols):
    b, s, _ = proj3.shape
    nc = s // ML_CHUNK
    v_tile = 2 * ML_DIM // ML_DIM
    n_rows = g_rows.shape[1]

    def rows(col, reverse):
        return pl.BlockSpec((1, ML_CHUNK, ML_DIM), lambda bi, c: (bi, nc - 1 - c if reverse else c, col))

    def cols(height, reverse):
        return pl.BlockSpec((1, height, ML_CHUNK), lambda bi, c: (bi, 0, nc - 1 - c if reverse else c))

    return pl.pallas_call(
        _mlstm_kernel,
        grid=(b, nc),
        in_specs=[
            rows(0, False), cols(ML_DIM, False), rows(v_tile, False),
            rows(0, True), cols(ML_DIM, True), rows(v_tile, True),
            cols(n_rows, False), cols(n_rows, True),
            pl.BlockSpec((1, ML_CHUNK, n_rows), lambda bi, c: (bi, c, 0)),
            pl.BlockSpec((1, ML_CHUNK, n_rows), lambda bi, c: (bi, nc - 1 - c, 0)),
        ],
        out_specs=[rows(0, False), rows(0, True)],
        out_shape=[jax.ShapeDtypeStruct((b, s, ML_DIM), BF16)] * 2,
        scratch_shapes=[
            pltpu.VMEM((2 * ML_HEADS, ML_HEAD_DIM, ML_EXT), F32),
            pltpu.VMEM((2 * ML_HEADS, 8, LANES), F32),
        ],
        compiler_params=_params(("arbitrary", "arbitrary"), 32),
        name="mlstm",
    )(q, k_t, proj3, q, k_t, proj3, g_rows, g_rows, g_cols, g_cols)


def _pack_bf16_pair(lo, hi):
    lo_bits = lax.bitcast_convert_type(lo.astype(BF16).astype(F32), jnp.uint32)
    hi_bits = lax.bitcast_convert_type(hi.astype(BF16).astype(F32), jnp.uint32)
    return (hi_bits & jnp.uint32(0xFFFF0000)) | (lo_bits >> 16)


def _unpack_bf16_pair(packed):
    lo = lax.bitcast_convert_type(packed << 16, F32)
    hi = lax.bitcast_convert_type(packed & jnp.uint32(0xFFFF0000), F32)
    return lo, hi


def _merge_kernel(x_ref, ao_ref, hf_ref, hb_ref, mo_ref, mg_ref, wpa_ref, wpm_ref, wo_ref, nfw_ref,
                  rw_ref, rb_ref, x1_ref, h2_ref, idx_ref, wt_ref):
    y_attn = jnp.dot(ao_ref[...], wpa_ref[...], preferred_element_type=F32)
    hsum = hf_ref[...].astype(F32) + hb_ref[...].astype(F32)
    ml = (_sigmoid(mo_ref[...].astype(F32)) * hsum).astype(BF16)
    y_ml = jnp.dot(ml, wpm_ref[...], preferred_element_type=F32)
    gates = _sigmoid(mg_ref[...].astype(F32))
    mixed = (gates[:, :D_MODEL] * y_attn + gates[:, D_MODEL:] * y_ml).astype(BF16)
    x1 = x_ref[...] + jnp.dot(mixed, wo_ref[...], preferred_element_type=F32)
    x1_ref[...] = x1
    ms = jnp.mean(x1 * x1, axis=-1, keepdims=True)
    h2 = x1 * lax.rsqrt(ms + RMS_EPS) * nfw_ref[...]
    h2_ref[...] = _pack_bf16_pair(h2[:, :D_MODEL // 2], h2[:, D_MODEL // 2:])

    logits = lax.dot_general(rw_ref[...], h2, (((1,), (1,)), ((), ())), preferred_element_type=F32,
                             precision=lax.Precision.HIGHEST) + rb_ref[...]
    row = lax.broadcasted_iota(jnp.int32, logits.shape, 0)
    rest = logits
    vals = []
    for k in range(TOP_K):
        mx = jnp.max(rest, axis=0, keepdims=True)
        first = jnp.min(jnp.where(rest == mx, row, N_EXPERTS), axis=0, keepdims=True)
        vals.append(mx)
        idx_ref[k:k + 1, :] = first
        rest = jnp.where(row == first, -jnp.inf, rest)
    exps = [jnp.exp(v - vals[0]) for v in vals]
    tot = exps[0] + exps[1] + exps[2] + exps[3]
    for k in range(TOP_K):
        wt_ref[k:k + 1, :] = exps[k] / tot


def _merge(x2d, attn_o, h_f, h_b, proj, wpa, wpm, wo, nfw, rw_t, rb):
    t = x2d.shape[0]
    tm = min(512, t)
    ml_o_tile = ML_QKV_COLS // ML_O_COLS
    merge_tile = (ML_QKV_COLS + ML_O_COLS) // MERGE_COLS
    row = lambda width: pl.BlockSpec((tm, width), lambda i: (i, 0))
    full = lambda a: pl.BlockSpec(a.shape, lambda i: (0, 0))
    return pl.pallas_call(
        _merge_kernel,
        grid=(t // tm,),
        in_specs=[
            row(D_MODEL), row(ATTN_DIM), row(ML_DIM), row(ML_DIM),
            pl.BlockSpec((tm, ML_O_COLS), lambda i: (i, ml_o_tile)),
            pl.BlockSpec((tm, MERGE_COLS), lambda i: (i, merge_tile)),
            full(wpa), full(wpm), full(wo), full(nfw), full(rw_t), full(rb),
        ],
        out_specs=[row(D_MODEL), row(D_MODEL // 2), pl.BlockSpec((TOP_K, tm), lambda i: (0, i)),
                   pl.BlockSpec((TOP_K, tm), lambda i: (0, i))],
        out_shape=[
            jax.ShapeDtypeStruct((t, D_MODEL), F32),
            jax.ShapeDtypeStruct((t, D_MODEL // 2), jnp.uint32),
            jax.ShapeDtypeStruct((TOP_K, t), jnp.int32),
            jax.ShapeDtypeStruct((TOP_K, t), F32),
        ],
        compiler_params=_params(("arbitrary",), 48),
        name="merge",
    )(x2d, attn_o, h_f, h_b, proj, proj, wpa, wpm, wo, nfw, rw_t, rb)


ROUTE_TILE = 512


def _route_kernel(idx_ref, pos_ref, cnt_ref, base):
    @pl.when(pl.program_id(0) == 0)
    def _():
        base[...] = jnp.zeros_like(base)

    idx = idx_ref[...]
    row = lax.broadcasted_iota(jnp.int32, (N_EXPERTS, ROUTE_TILE), 0)
    onehot = jnp.zeros((N_EXPERTS, ROUTE_TILE), F32)
    for k in range(TOP_K):
        onehot = onehot + jnp.where(row == idx[k:k + 1, :], 1.0, 0.0)
    s_i = lax.broadcasted_iota(jnp.int32, (ROUTE_TILE, ROUTE_TILE), 0)
    t_i = lax.broadcasted_iota(jnp.int32, (ROUTE_TILE, ROUTE_TILE), 1)
    upper = jnp.where(s_i <= t_i, 1.0, 0.0).astype(BF16)
    incl = jnp.dot(onehot.astype(BF16), upper, preferred_element_type=F32)
    before = base[:, 0:1]
    count = incl + before
    for k in range(TOP_K):
        mine = jnp.sum(jnp.where(row == idx[k:k + 1, :], count, 0.0), axis=0, keepdims=True)
        pos_ref[k:k + 1, :] = (mine - 1.0).astype(jnp.int32)
    total = before + incl[:, ROUTE_TILE - 1:ROUTE_TILE]
    base[...] = jnp.broadcast_to(total, base.shape)
    cnt_ref[...] = jnp.broadcast_to(total, cnt_ref.shape)


def _route(idx):
    t = idx.shape[1]
    return pl.pallas_call(
        _route_kernel,
        grid=(t // ROUTE_TILE,),
        in_specs=[pl.BlockSpec((TOP_K, ROUTE_TILE), lambda i: (0, i))],
        out_specs=[pl.BlockSpec((TOP_K, ROUTE_TILE), lambda i: (0, i)),
                   pl.BlockSpec((N_EXPERTS, LANES), lambda i: (0, 0))],
        out_shape=[jax.ShapeDtypeStruct((TOP_K, t), jnp.int32),
                   jax.ShapeDtypeStruct((N_EXPERTS, LANES), F32)],
        scratch_shapes=[pltpu.VMEM((N_EXPERTS, LANES), F32)],
        compiler_params=_params(("arbitrary",), 32),
        name="route",
    )(idx)


GROUP_TILE = 512
DISPATCH_TILE = 1024
ROW_WORDS = D_MODEL // 2


def _dispatch_kernel(slot_hbm, ztile_ref, nu_ref, *refs, steps):
    srcs = refs[:len(steps)]
    dst_hbm, slot_smem, zbuf, sem = refs[len(steps):]
    i = pl.program_id(0)
    n_slots = DISPATCH_TILE * TOP_K
    n_tiles = dst_hbm.shape[0] // GROUP_TILE
    load = pltpu.make_async_copy(slot_hbm.at[pl.ds(i * n_slots, n_slots)], slot_smem, sem.at[0])
    load.start()

    @pl.when(i == 0)
    def _():
        zbuf[...] = jnp.zeros_like(zbuf)

        def fill_tile(tile):
            pltpu.make_async_copy(zbuf, dst_hbm.at[pl.ds(tile * GROUP_TILE, GROUP_TILE), :], sem.at[1]).start()

        def fill_last(e, n):
            tile = ztile_ref[e]

            @pl.when(tile >= 0)
            def _():
                fill_tile(tile)

            return n + jnp.where(tile >= 0, 1, 0)

        def fill_unused(tile, c):
            fill_tile(tile)
            return c

        n_fill = lax.fori_loop(0, N_EXPERTS, fill_last, 0)
        lax.fori_loop(nu_ref[0], n_tiles, fill_unused, 0)

        def drain(_, c):
            pltpu.make_async_copy(zbuf, dst_hbm.at[pl.ds(0, GROUP_TILE), :], sem.at[1]).wait()
            return c

        lax.fori_loop(0, n_fill + n_tiles - nu_ref[0], drain, 0)

    load.wait()

    first = 0
    for src_ref, n_steps in zip(srcs, steps):
        @pl.when((i >= first) & (i < first + n_steps))
        def _(src_ref=src_ref):
            def issue(t, c):
                src = src_ref.at[pl.ds(t, 1), :]
                for k in range(TOP_K):
                    slot = slot_smem[t * TOP_K + k]
                    pltpu.make_async_copy(src, dst_hbm.at[pl.ds(slot, 1), :], sem.at[1]).start(priority=k % 2)
                return c

            lax.fori_loop(0, DISPATCH_TILE, issue, 0)

        first += n_steps
    pltpu.make_async_copy(dst_hbm.at[pl.ds(0, n_slots), :], dst_hbm.at[pl.ds(0, n_slots), :], sem.at[1]).wait()


def _dispatch(slots, ztile, n_used, sources, n_rows):
    steps = tuple(h.shape[0] // DISPATCH_TILE for h in sources)
    any_spec = pl.BlockSpec(memory_space=pl.ANY)
    smem_spec = pl.BlockSpec(memory_space=pltpu.SMEM)
    src_specs = []
    first = 0
    for n_steps in steps:
        src_specs.append(pl.BlockSpec((DISPATCH_TILE, ROW_WORDS),
                                      lambda i, first=first, n_steps=n_steps: (jnp.clip(i - first, 0, n_steps - 1), 0)))
        first += n_steps
    return pl.pallas_call(
        functools.partial(_dispatch_kernel, steps=steps),
        grid=(sum(steps),),
        in_specs=[any_spec, smem_spec, smem_spec] + src_specs,
        out_specs=any_spec,
        out_shape=jax.ShapeDtypeStruct((n_rows, ROW_WORDS), jnp.uint32),
        scratch_shapes=[
            pltpu.SMEM((DISPATCH_TILE * TOP_K,), jnp.int32),
            pltpu.VMEM((GROUP_TILE, ROW_WORDS), jnp.uint32),
            pltpu.SemaphoreType.DMA((2,)),
        ],
        compiler_params=_params(("arbitrary",), 32),
        name="dispatch",
    )(slots, ztile, n_used, *sources)


def _expert_kernel(te_ref, ts_ref, nu_ref, x_ref, wgu_ref, bgu_ref, wd_ref, bd_ref, y_ref):
    j = pl.program_id(0)

    @pl.when(j < nu_ref[0])
    def _():
        lo, hi = _unpack_bf16_pair(x_ref[...])
        half = D_MODEL // 2
        gu = (jnp.dot(lo.astype(BF16), wgu_ref[0, :half, :], preferred_element_type=F32)
              + jnp.dot(hi.astype(BF16), wgu_ref[0, half:, :], preferred_element_type=F32) + bgu_ref[0])
        gate = jnp.minimum(gu[:, :D_FF], SWIGLU_LIMIT)
        up = jnp.clip(gu[:, D_FF:], -SWIGLU_LIMIT, SWIGLU_LIMIT)
        hid = (up + 1.0) * gate * _sigmoid(SWIGLU_ALPHA * gate)
        y = jnp.dot(hid.astype(BF16), wd_ref[0], preferred_element_type=F32) + bd_ref[0]
        y_ref[...] = _pack_bf16_pair(y[:, :half], y[:, half:])

    @pl.when(j >= nu_ref[0])
    def _():
        y_ref[...] = jnp.zeros_like(y_ref)


def _experts(tile_expert, tile_src, n_used, xs, wgu, bgu, wd, bd):
    n_rows = xs.shape[0]
    row_spec = pl.BlockSpec((GROUP_TILE, ROW_WORDS), lambda j, te, ts, nu: (ts[j], 0))
    out_spec = pl.BlockSpec((GROUP_TILE, ROW_WORDS), lambda j, te, ts, nu: (j, 0))
    return pl.pallas_call(
        _expert_kernel,
        grid_spec=pltpu.PrefetchScalarGridSpec(
            num_scalar_prefetch=3,
            grid=(n_rows // GROUP_TILE,),
            in_specs=[
                row_spec,
                pl.BlockSpec((1, D_MODEL, 2 * D_FF), lambda j, te, ts, nu: (te[j], 0, 0)),
                pl.BlockSpec((1, 1, 2 * D_FF), lambda j, te, ts, nu: (te[j], 0, 0)),
                pl.BlockSpec((1, D_FF, D_MODEL), lambda j, te, ts, nu: (te[j], 0, 0)),
                pl.BlockSpec((1, 1, D_MODEL), lambda j, te, ts, nu: (te[j], 0, 0)),
            ],
            out_specs=out_spec,
        ),
        out_shape=jax.ShapeDtypeStruct((n_rows, ROW_WORDS), jnp.uint32),
        compiler_params=_params(("arbitrary",), 56),
        name="experts",
    )(tile_expert, tile_src, n_used, xs, wgu, bgu, wd, bd)


COMBINE_TILE = 256


def _combine_kernel(slot_hbm, ys_hbm, x1_ref, wt_ref, nw_ref, o_ref, slot_smem, buf, sem):
    i = pl.program_id(0)
    n_slots = COMBINE_TILE * TOP_K
    load = pltpu.make_async_copy(slot_hbm.at[pl.ds(i * n_slots, n_slots)], slot_smem, sem.at[0])
    load.start()
    load.wait()

    def issue(t, c):
        for k in range(TOP_K):
            slot = slot_smem[t * TOP_K + k]
            pltpu.make_async_copy(ys_hbm.at[pl.ds(slot, 1), :], buf.at[k, pl.ds(t, 1), :],
                                  sem.at[1]).start(priority=k % 2)
        return c

    lax.fori_loop(0, COMBINE_TILE, issue, 0)
    pltpu.make_async_copy(ys_hbm.at[pl.ds(0, n_slots), :], ys_hbm.at[pl.ds(0, n_slots), :], sem.at[1]).wait()

    half = D_MODEL // 2
    wt = wt_ref[...]
    acc_lo = x1_ref[:, :half]
    acc_hi = x1_ref[:, half:]
    for k in range(TOP_K):
        lo, hi = _unpack_bf16_pair(buf[k])
        acc_lo = acc_lo + wt[:, k:k + 1] * lo
        acc_hi = acc_hi + wt[:, k:k + 1] * hi
    ms = (jnp.sum(acc_lo * acc_lo, axis=-1, keepdims=True)
          + jnp.sum(acc_hi * acc_hi, axis=-1, keepdims=True)) * (1.0 / D_MODEL)
    inv = lax.rsqrt(ms + RMS_EPS)
    o_ref[:, :half] = acc_lo * inv * nw_ref[:, :half]
    o_ref[:, half:] = acc_hi * inv * nw_ref[:, half:]


def _combine(slots, ys, x1, wt, nw):
    t = x1.shape[0]
    tm = COMBINE_TILE
    return pl.pallas_call(
        _combine_kernel,
        grid=(t // tm,),
        in_specs=[
            pl.BlockSpec(memory_space=pl.ANY),
            pl.BlockSpec(memory_space=pl.ANY),
            pl.BlockSpec((tm, D_MODEL), lambda i: (i, 0)),
            pl.BlockSpec((tm, TOP_K), lambda i: (i, 0)),
            pl.BlockSpec((1, D_MODEL), lambda i: (0, 0)),
        ],
        out_specs=pl.BlockSpec((tm, D_MODEL), lambda i: (i, 0)),
        out_shape=jax.ShapeDtypeStruct((t, D_MODEL), F32),
        scratch_shapes=[
            pltpu.SMEM((tm * TOP_K,), jnp.int32),
            pltpu.VMEM((TOP_K, tm, ROW_WORDS), jnp.uint32),
            pltpu.SemaphoreType.DMA((2,)),
        ],
        compiler_params=_params(("arbitrary",), 32),
        name="combine",
    )(slots, ys, x1, wt, nw)


def _moe(parts, p, final_w):
    sizes = [x1.shape[0] for x1, _, _, _ in parts]
    idx = jnp.concatenate([q[2] for q in parts], axis=1)
    t_all = idx.shape[1]
    pos, cnt = _route(idx)
    counts = cnt[:, 0].astype(jnp.int32)
    tiles_e = (counts + GROUP_TILE - 1) // GROUP_TILE
    tile_end = jnp.cumsum(tiles_e)
    tile_start = tile_end - tiles_e
    n_used = tile_end[-1]
    n_tiles = (t_all * TOP_K) // GROUP_TILE + N_EXPERTS
    n_rows = n_tiles * GROUP_TILE
    first_row = tile_start * GROUP_TILE
    experts = jnp.arange(N_EXPERTS, dtype=jnp.int32)[:, None, None]
    base = jnp.sum(jnp.where(idx[None] == experts, first_row[:, None, None], 0), axis=0)
    slots = (base + pos).T.reshape(-1)
    tile_id = jnp.arange(n_tiles, dtype=jnp.int32)
    tile_src = jnp.minimum(tile_id, n_used - 1).astype(jnp.int32)
    tile_expert = jnp.sum(tile_src[:, None] >= tile_end[None, :], axis=1).astype(jnp.int32)
    tile_expert = jnp.minimum(tile_expert, N_EXPERTS - 1)
    ztile = jnp.where(counts > 0, tile_end - 1, -1).astype(jnp.int32)

    n_used = n_used.reshape(1).astype(jnp.int32)
    xs = _dispatch(slots, ztile, n_used, [q[1] for q in parts], n_rows)
    ys = _experts(tile_expert, tile_src, n_used, xs, p["wgu"], p["bgu"], p["wd"], p["bd"])
    outs = []
    off = 0
    for (x1, _, _, wt), t in zip(parts, sizes):
        outs.append(_combine(lax.slice(slots, (off * TOP_K,), ((off + t) * TOP_K,)), ys, x1, wt.T, final_w))
        off += t
    return outs


def _rope_tables(seq):
    half = ATTN_HEAD_DIM // 2
    inv_freq = ROPE_THETA ** (-jnp.arange(half, dtype=F32) / half)
    ang = jnp.arange(seq, dtype=F32)[:, None] * inv_freq[None, :]
    cos, sin = jnp.cos(ang), jnp.sin(ang)
    reps = LANES // ATTN_HEAD_DIM
    cos_t = jnp.tile(jnp.concatenate([cos, cos], axis=1), (1, reps))
    sin_t = jnp.tile(jnp.concatenate([-sin, sin], axis=1), (1, reps))
    return cos_t, sin_t


def _pack_layer(w_in, conv_w, conv_b, gate_b, wpa, wpm, wo, nmw, nfw, rw, rb, wgu, bgu, wd, bd):
    c0 = ATTN_QKV_COLS
    c1 = c0 + ML_QKV_COLS
    c2 = c1 + ML_O_COLS
    c3 = c2 + ML_GATE_COLS
    w_attn = w_in[:, :c0].reshape(D_MODEL, N_GROUPS, 3, ATTN_DIM)
    w_attn = w_attn * jnp.array([ATTN_HEAD_DIM ** -0.5, 1.0, 1.0], F32)[None, None, :, None]
    w_main = jnp.concatenate([w_in[:, c0:c2], w_in[:, c3:], w_attn.reshape(D_MODEL, c0)], axis=1).astype(BF16)
    w_gate = jnp.pad(w_in[:, c2:c3], ((0, 0), (0, LANES - ML_GATE_COLS))).astype(BF16)
    return dict(
        w_main=w_main, w_gate=w_gate, conv_w=conv_w, conv_b=conv_b.reshape(1, -1),
        gate_b=gate_b.reshape(-1, 1), wpa=wpa.astype(BF16), wpm=wpm.astype(BF16), wo=wo.astype(BF16),
        nmw=nmw.reshape(1, -1), nfw=nfw.reshape(1, -1), rw_t=rw.T, rb=rb.reshape(-1, 1),
        wgu=wgu.astype(BF16), bgu=bgu.reshape(N_EXPERTS, 1, -1), wd=wd.astype(BF16),
        bd=bd.reshape(N_EXPERTS, 1, -1))


def _layer(x, p, tables):
    b, s, _ = x.shape
    x2d = x.reshape(b * s, D_MODEL)
    proj, gates, qkv1, qkv2 = _inproj(x2d, p["nmw"], p["w_main"], p["w_gate"], tables[0], tables[1], s)
    proj3 = proj.reshape(b, s, N_MAIN)

    merged = _attn_plain(proj3)
    for g, qkv in ((1, qkv1), (2, qkv2)):
        merged = _attn_dilated(qkv, g, ATTN_PATTERNS[g][1], merged, b, s)
    attn_o = merged[0]

    ml_q, ml_kt = _conv_qk(proj3, p["conv_w"], p["conv_b"])
    gates_t = gates[:, :ML_GATE_COLS].reshape(b, s, ML_GATE_COLS).transpose(0, 2, 1)
    g_rows = _gate_prep(gates_t, p["gate_b"])
    g_cols = g_rows.transpose(0, 2, 1)
    h_f, h_b = _mlstm(proj3, ml_q, ml_kt, g_rows, g_cols)

    return _merge(x2d, attn_o, h_f.reshape(b * s, ML_DIM), h_b.reshape(b * s, ML_DIM), proj,
                  p["wpa"], p["wpm"], p["wo"], p["nfw"], p["rw_t"], p["rb"])


def kernel(x_prompt, x_sample, norm_mix_w, w_in, mlstm_conv_w, mlstm_conv_b, mlstm_gate_b, w_proj_attn,
           w_proj_mlstm, w_out, norm_ffn_w, router_w, router_b, expert_w_gu, expert_b_gu, expert_w_down,
           expert_b_down, norm_final_w):
    depth = w_in.shape[0]
    assert depth == 1, "the final RMSNorm is fused into the last layer's MoE kernel"
    p = _pack_layer(w_in[0], mlstm_conv_w[0], mlstm_conv_b[0], mlstm_gate_b[0], w_proj_attn[0],
                    w_proj_mlstm[0], w_out[0], norm_mix_w[0], norm_ffn_w[0], router_w[0], router_b[0],
                    expert_w_gu[0], expert_b_gu[0], expert_w_down[0], expert_b_down[0])
    final_w = norm_final_w.reshape(1, -1)
    xs = (x_prompt, x_sample)
    parts = [_layer(x, p, _rope_tables(x.shape[1])) for x in xs]
    outs = _moe(parts, p, final_w)
    return tuple(o.reshape(x.shape) for o, x in zip(outs, xs))
```

```python
import functools

import jax
import jax.numpy as jnp
from jax import lax
from jax.experimental import pallas as pl
from jax.experimental.pallas import tpu as pltpu

F32 = jnp.float32
BF16 = jnp.bfloat16

D_MODEL = 1024
ATTN_PATTERNS = ((128, 1), (512, 4), (2048, 16))
N_GROUPS = 3
ATTN_HEADS = 8
ATTN_HEAD_DIM = 64
ATTN_DIM = ATTN_HEADS * ATTN_HEAD_DIM
ATTN_BLOCK = 64
ROPE_THETA = 10000.0
ML_DIM = D_MODEL
ML_HEADS = 4
ML_HEAD_DIM = ML_DIM // ML_HEADS
ML_CHUNK = 128
CONV_W = 5
ATTN_QKV_COLS = N_GROUPS * 3 * ATTN_DIM
ML_QKV_COLS = 3 * ML_DIM
ML_O_COLS = ML_DIM
ML_GATE_COLS = 4 * ML_HEADS
MERGE_COLS = 2 * D_MODEL
N_EXPERTS = 32
TOP_K = 4
D_FF = D_MODEL
SWIGLU_LIMIT = 7.0
SWIGLU_ALPHA = 1.702
RMS_EPS = 1e-5
NEG_INF = -1e30

LANES = 128
COL_TILE = 512
N_PACKED = ML_QKV_COLS + ML_O_COLS + MERGE_COLS + ATTN_QKV_COLS
N_COL_TILES = N_PACKED // COL_TILE
ATTN_TILE0 = (ML_QKV_COLS + ML_O_COLS + MERGE_COLS) // COL_TILE
GROUP_TILES = 3 * ATTN_DIM // COL_TILE
N_MAIN_TILES = ATTN_TILE0 + GROUP_TILES
N_MAIN = N_MAIN_TILES * COL_TILE
ML_EXT = ML_HEAD_DIM + LANES


def _params(sem, vmem_mb):
    return pltpu.CompilerParams(dimension_semantics=sem, vmem_limit_bytes=vmem_mb * 1024 * 1024)


def _sigmoid(x):
    return 1.0 / (1.0 + jnp.exp(-x))


INPROJ_CHUNK = 512
WIDE_TILE = 3 * ATTN_DIM
N_WIDE_TILES = N_PACKED // WIDE_TILE
N_PLAIN_WIDE = ATTN_TILE0 * COL_TILE // WIDE_TILE


def _inproj_kernel(x_ref, nw_ref, w_ref, wg_ref, cos_ref, sin_ref, out_ref, gates_ref, g1_ref, g2_ref,
                   h_scr, slab, slab2):
    j = pl.program_id(1)
    tm = x_ref.shape[0]
    n_slabs = WIDE_TILE // LANES

    @pl.when(j == 0)
    def _():
        x = x_ref[...]
        ms = jnp.mean(x * x, axis=-1, keepdims=True)
        h = (x * lax.rsqrt(ms + RMS_EPS) * nw_ref[...]).astype(BF16)
        h_scr[...] = h
        gates_ref[...] = jnp.dot(h, wg_ref[...], preferred_element_type=F32)

    chunk = min(INPROJ_CHUNK, tm)

    def chunks():
        for c in range(tm // chunk):
            rows = slice(c * chunk, (c + 1) * chunk)
            yield c, rows, jnp.dot(h_scr[rows, :], w_ref[...], preferred_element_type=F32)

    def rope(a, rows):
        reps = ATTN_DIM // LANES
        c = jnp.concatenate([cos_ref[rows, :]] * reps, axis=1)
        s = jnp.concatenate([sin_ref[rows, :]] * reps, axis=1)
        lane = lax.broadcasted_iota(jnp.int32, a.shape, 1)
        half = ATTN_HEAD_DIM // 2
        first = (lane % ATTN_HEAD_DIM) < half
        sw = jnp.where(first, pltpu.roll(a, ATTN_DIM - half, 1), pltpu.roll(a, half, 1))
        return a * c + sw * s

    def rotated(acc, rows):
        return jnp.concatenate([rope(acc[:, :ATTN_DIM], rows), rope(acc[:, ATTN_DIM:2 * ATTN_DIM], rows),
                                acc[:, 2 * ATTN_DIM:]], axis=1)

    def deinterleave(val, c, dst_ref, dilation):
        n = chunk // dilation
        for s in range(n_slabs):
            slab[s] = val[:, s * LANES:(s + 1) * LANES]
        if dilation == 16:
            quarter = chunk // 4
            for s in range(n_slabs):
                for r4 in range(4):
                    slab2[s, r4 * quarter:(r4 + 1) * quarter, :] = slab[s, pl.ds(r4, quarter, stride=4), :]
            for r in range(dilation):
                start = (r % 4) * quarter + r // 4
                piece = jnp.concatenate([slab2[s, pl.ds(start, n, stride=4), :] for s in range(n_slabs)], axis=1)
                dst_ref[0, r, c * n:(c + 1) * n, :] = piece.astype(BF16)
            return
        for r in range(dilation):
            piece = jnp.concatenate([slab[s, pl.ds(r, n, stride=dilation), :] for s in range(n_slabs)], axis=1)
            dst_ref[0, r, c * n:(c + 1) * n, :] = piece.astype(BF16)

    @pl.when(j < N_PLAIN_WIDE)
    def _():
        for _, rows, acc in chunks():
            out_ref[rows, :] = acc.astype(BF16)

    dst = (out_ref, g1_ref, g2_ref)
    for g, (_, dilation) in enumerate(ATTN_PATTERNS):
        @pl.when(j == N_PLAIN_WIDE + g)
        def _(g=g, dilation=dilation):
            for c, rows, acc in chunks():
                val = rotated(acc, rows)
                if dilation == 1:
                    out_ref[rows, :] = val.astype(BF16)
                else:
                    deinterleave(val, c, dst[g], dilation)


def _inproj(x2d, nw, w_main, w_gate, cos_t, sin_t, seq):
    t = x2d.shape[0]
    tm = min(1024, seq)
    tiles_per_seq = seq // tm
    chunk = min(INPROJ_CHUNK, tm)

    def group_out(dilation):
        return (pl.BlockSpec((1, dilation, tm // dilation, WIDE_TILE), lambda i, j: (i, 0, 0, 0)),
                jax.ShapeDtypeStruct((t // tm, dilation, tm // dilation, WIDE_TILE), BF16))

    (g1_spec, g1_shape), (g2_spec, g2_shape) = [group_out(ATTN_PATTERNS[g][1]) for g in (1, 2)]
    return pl.pallas_call(
        _inproj_kernel,
        grid=(t // tm, N_WIDE_TILES),
        in_specs=[
            pl.BlockSpec((tm, D_MODEL), lambda i, j: (i, 0)),
            pl.BlockSpec((1, D_MODEL), lambda i, j: (0, 0)),
            pl.BlockSpec((D_MODEL, WIDE_TILE), lambda i, j: (0, j)),
            pl.BlockSpec((D_MODEL, LANES), lambda i, j: (0, 0)),
            pl.BlockSpec((tm, LANES), lambda i, j: (i % tiles_per_seq, 0)),
            pl.BlockSpec((tm, LANES), lambda i, j: (i % tiles_per_seq, 0)),
        ],
        out_specs=[
            pl.BlockSpec((tm, WIDE_TILE), lambda i, j: (i, jnp.minimum(j, N_PLAIN_WIDE))),
            pl.BlockSpec((tm, LANES), lambda i, j: (i, 0)),
            g1_spec, g2_spec,
        ],
        out_shape=[
            jax.ShapeDtypeStruct((t, N_MAIN), BF16),
            jax.ShapeDtypeStruct((t, LANES), F32),
            g1_shape, g2_shape,
        ],
        scratch_shapes=[pltpu.VMEM((tm, D_MODEL), BF16), pltpu.VMEM((WIDE_TILE // LANES, chunk, LANES), F32),
                        pltpu.VMEM((WIDE_TILE // LANES, chunk, LANES), F32)],
        compiler_params=_params(("arbitrary", "arbitrary"), 56),
        name="inproj",
    )(x2d, nw, w_main, w_gate, cos_t, sin_t)


ATTN_QUERY_TILE = 128


def _window_mask(n, qt, sub_len):
    kt = qt + 2 * ATTN_BLOCK
    qpos = n * qt + lax.broadcasted_iota(jnp.int32, (qt, kt), 0)
    kpos = n * qt - ATTN_BLOCK + lax.broadcasted_iota(jnp.int32, (qt, kt), 1)
    return (jnp.abs(kpos - qpos) <= ATTN_BLOCK) & (kpos >= 0) & (kpos < sub_len)


def _attend(q, k, v, valid):
    qt = q.shape[0]
    pairs = range(ATTN_HEADS // 2)
    lane = lax.broadcasted_iota(jnp.int32, (qt, LANES), 1)
    lo = lane < ATTN_HEAD_DIM
    valid2 = jnp.concatenate([valid, valid], axis=0)
    scores = []
    for p in pairs:
        qp = q[:, p * LANES:(p + 1) * LANES]
        zero = jnp.zeros_like(qp)
        stacked = jnp.concatenate([jnp.where(lo, qp, zero), jnp.where(lo, zero, qp)], axis=0)
        scores.append(lax.dot_general(stacked, k[:, p * LANES:(p + 1) * LANES], (((1,), (1,)), ((), ())),
                                      preferred_element_type=F32))
    probs, inv_den, lses = [], [], []
    for p in pairs:
        s = jnp.where(valid2, scores[p], NEG_INF)
        mx = jnp.max(s, axis=-1, keepdims=True)
        e = jnp.exp(s - mx)
        den = jnp.sum(e, axis=-1, keepdims=True)
        probs.append(e.astype(BF16))
        inv_den.append(1.0 / den)
        lses.append(mx + jnp.log(den))
    pvs = [jnp.dot(probs[p], v[:, p * LANES:(p + 1) * LANES], preferred_element_type=F32) for p in pairs]
    lse_out = jnp.zeros((qt, LANES), F32)
    outs = []
    for p in pairs:
        o = pvs[p] * inv_den[p]
        outs.append(jnp.where(lo, o[:qt], o[qt:]))
        lse_out = jnp.where(lane == 2 * p, lses[p][:qt], jnp.where(lane == 2 * p + 1, lses[p][qt:], lse_out))
    return outs, lse_out


def _head_expander():
    row = lax.broadcasted_iota(jnp.int32, (LANES, ATTN_DIM), 0)
    col = lax.broadcasted_iota(jnp.int32, (LANES, ATTN_DIM), 1)
    return jnp.where(col // ATTN_HEAD_DIM == row, 1.0, 0.0).astype(BF16)


def _merge_groups(o_cur, lse_cur, o_prev, lse_prev, expander):
    m = jnp.maximum(lse_prev, lse_cur)
    a = jnp.exp(lse_prev - m)
    tot = a + jnp.exp(lse_cur - m)
    w_prev = a / tot
    hi = w_prev.astype(BF16)
    lo = (w_prev - hi.astype(F32)).astype(BF16)
    spread = (jnp.dot(hi, expander, preferred_element_type=F32) + jnp.dot(lo, expander, preferred_element_type=F32))
    outs = [o_cur[p] + spread[:, p * LANES:(p + 1) * LANES] * (o_prev[p] - o_cur[p])
            for p in range(ATTN_HEADS // 2)]
    lane = lax.broadcasted_iota(jnp.int32, m.shape, 1)
    return outs, jnp.where(lane < ATTN_HEADS, m + jnp.log(tot), 0.0)


def _attn_kernel(q_ref, kp_ref, kc_ref, kn_ref, vp_ref, vc_ref, vn_ref, o_ref, l_ref, *, sub_len, qt):
    valid = _window_mask(pl.program_id(1), qt, sub_len)
    k = jnp.concatenate([kp_ref[0], kc_ref[0], kn_ref[0]], axis=0)
    v = jnp.concatenate([vp_ref[0], vc_ref[0], vn_ref[0]], axis=0)
    outs, lse = _attend(q_ref[0], k, v, valid)
    for p, o in enumerate(outs):
        o_ref[0, :, p * LANES:(p + 1) * LANES] = o.astype(BF16)
    l_ref[0] = lse


MERGE_ROWS = 256


def _attn_dilated_kernel(q_ref, kp_ref, kc_ref, kn_ref, vp_ref, vc_ref, vn_ref, op_ref, lp_ref, o_ref, l_ref,
                         o_slab, l_slab, *, sub_len, qt, dilation):
    valid = _window_mask(pl.program_id(1), qt, sub_len)

    def residue(r, c):
        k = jnp.concatenate([kp_ref[0, r], kc_ref[0, r], kn_ref[0, r]], axis=0)
        v = jnp.concatenate([vp_ref[0, r], vc_ref[0, r], vn_ref[0, r]], axis=0)
        outs, lse = _attend(q_ref[0, r], k, v, valid)
        for p, o in enumerate(outs):
            o_slab[p, pl.ds(r, qt, stride=dilation), :] = o
        l_slab[pl.ds(r, qt, stride=dilation), :] = lse
        return c

    lax.fori_loop(0, dilation, residue, 0)
    expander = _head_expander()

    def merge(i, c):
        rows = pl.ds(pl.multiple_of(i * MERGE_ROWS, MERGE_ROWS), MERGE_ROWS)
        prev = op_ref[rows, :].astype(F32)
        o_cur = [o_slab[p, rows, :] for p in range(ATTN_HEADS // 2)]
        o_prev = [prev[:, p * LANES:(p + 1) * LANES] for p in range(ATTN_HEADS // 2)]
        outs, lse = _merge_groups(o_cur, l_slab[rows, :], o_prev, lp_ref[rows, :], expander)
        for p, o in enumerate(outs):
            o_ref[rows, p * LANES:(p + 1) * LANES] = o.astype(BF16)
        l_ref[rows, :] = lse
        return c

    lax.fori_loop(0, qt * dilation // MERGE_ROWS, merge, 0)


def _attn_plain(proj3):
    b, s, _ = proj3.shape
    qt = min(ATTN_QUERY_TILE, s)
    nblk = s // ATTN_BLOCK
    qb = qt // ATTN_BLOCK

    def cur(which):
        return pl.BlockSpec((1, qt, COL_TILE), lambda bi, n: (bi, n, ATTN_TILE0 + which))

    def before(which):
        return pl.BlockSpec((1, ATTN_BLOCK, COL_TILE),
                            lambda bi, n: (bi, jnp.maximum(n * qb - 1, 0), ATTN_TILE0 + which))

    def after(which):
        return pl.BlockSpec((1, ATTN_BLOCK, COL_TILE),
                            lambda bi, n: (bi, jnp.minimum((n + 1) * qb, nblk - 1), ATTN_TILE0 + which))

    o, l = pl.pallas_call(
        functools.partial(_attn_kernel, sub_len=s, qt=qt),
        grid=(b, s // qt),
        in_specs=[cur(0), before(1), cur(1), after(1), before(2), cur(2), after(2)],
        out_specs=[pl.BlockSpec((1, qt, ATTN_DIM), lambda bi, n: (bi, n, 0)),
                   pl.BlockSpec((1, qt, LANES), lambda bi, n: (bi, n, 0))],
        out_shape=[jax.ShapeDtypeStruct((b, s, ATTN_DIM), BF16), jax.ShapeDtypeStruct((b, s, LANES), F32)],
        compiler_params=_params(("arbitrary", "arbitrary"), 48),
        name="attn_g0",
    )(*([proj3] * 7))
    return o.reshape(b * s, ATTN_DIM), l.reshape(b * s, LANES)


def _attn_dilated(qkv, g, dilation, prev, b, s):
    n_tiles, _, rows, _ = qkv.shape
    tiles_per_seq = n_tiles // b
    sub_len = s // dilation
    qt = min(ATTN_QUERY_TILE, rows)
    span = qt * dilation
    q_per_tile = rows // qt
    h_per_tile = rows // ATTN_BLOCK
    qb = qt // ATTN_BLOCK
    nblk = sub_len // ATTN_BLOCK

    def cur(which):
        return pl.BlockSpec((1, dilation, qt, COL_TILE),
                            lambda bi, n: (bi * tiles_per_seq + n // q_per_tile, 0, n % q_per_tile, which))

    def halo(which, blk_of):
        def index(bi, n):
            blk = blk_of(n)
            return (bi * tiles_per_seq + blk // h_per_tile, 0, blk % h_per_tile, which)
        return pl.BlockSpec((1, dilation, ATTN_BLOCK, COL_TILE), index)

    before = lambda which: halo(which, lambda n: jnp.maximum(n * qb - 1, 0))
    after = lambda which: halo(which, lambda n: jnp.minimum((n + 1) * qb, nblk - 1))
    steps = sub_len // qt
    o_spec = pl.BlockSpec((span, ATTN_DIM), lambda bi, n: (bi * steps + n, 0))
    l_spec = pl.BlockSpec((span, LANES), lambda bi, n: (bi * steps + n, 0))
    return pl.pallas_call(
        functools.partial(_attn_dilated_kernel, sub_len=sub_len, qt=qt, dilation=dilation),
        grid=(b, steps),
        in_specs=[cur(0), before(1), cur(1), after(1), before(2), cur(2), after(2), o_spec, l_spec],
        out_specs=[o_spec, l_spec],
        out_shape=[jax.ShapeDtypeStruct((b * s, ATTN_DIM), BF16), jax.ShapeDtypeStruct((b * s, LANES), F32)],
        scratch_shapes=[pltpu.VMEM((ATTN_HEADS // 2, span, LANES), F32), pltpu.VMEM((span, LANES), F32)],
        compiler_params=_params(("arbitrary", "arbitrary"), 56),
        name=f"attn_g{g}",
    )(*([qkv] * 7), prev[0], prev[1])


CONV_HALO = 16


Q_TILES = ML_DIM // COL_TILE


def _conv_kernel(xp_ref, xc_ref, xn_ref, w_ref, b_ref, q_ref, kt_ref, buf, *, tm):
    i = pl.program_id(1)
    c = pl.program_id(2)
    last = pl.num_programs(1) - 1
    buf[0:CONV_HALO, :] = jnp.where(i > 0, xp_ref[0].astype(F32), 0.0)
    buf[CONV_HALO:CONV_HALO + tm, :] = xc_ref[0].astype(F32)
    buf[CONV_HALO + tm:, :] = jnp.where(i < last, xn_ref[0].astype(F32), 0.0)
    w = w_ref[...]
    y = jnp.broadcast_to(b_ref[...], (tm, COL_TILE))
    for tap in range(CONV_W):
        y = y + w[tap:tap + 1, :] * buf[pl.ds(CONV_HALO - CONV_W // 2 + tap, tm), :]
    y = y * _sigmoid(y)

    @pl.when(c < Q_TILES)
    def _():
        q_ref[0] = y.astype(BF16)

    @pl.when(c >= Q_TILES)
    def _():
        kt_ref[0] = (y * ML_HEAD_DIM ** -0.5).T.astype(BF16)


def _conv_qk(proj3, conv_w, conv_b):
    b, s, _ = proj3.shape
    tm = min(1024, s)
    hb = tm // CONV_HALO
    nh = s // CONV_HALO
    return pl.pallas_call(
        functools.partial(_conv_kernel, tm=tm),
        grid=(b, s // tm, 2 * Q_TILES),
        in_specs=[
            pl.BlockSpec((1, CONV_HALO, COL_TILE), lambda bi, i, c: (bi, jnp.maximum(i * hb - 1, 0), c)),
            pl.BlockSpec((1, tm, COL_TILE), lambda bi, i, c: (bi, i, c)),
            pl.BlockSpec((1, CONV_HALO, COL_TILE), lambda bi, i, c: (bi, jnp.minimum((i + 1) * hb, nh - 1), c)),
            pl.BlockSpec((CONV_W, COL_TILE), lambda bi, i, c: (0, c)),
            pl.BlockSpec((1, COL_TILE), lambda bi, i, c: (0, c)),
        ],
        out_specs=[
            pl.BlockSpec((1, tm, COL_TILE), lambda bi, i, c: (bi, i, jnp.minimum(c, Q_TILES - 1))),
            pl.BlockSpec((1, COL_TILE, tm), lambda bi, i, c: (bi, jnp.maximum(c - Q_TILES, 0), i)),
        ],
        out_shape=[jax.ShapeDtypeStruct((b, s, ML_DIM), BF16), jax.ShapeDtypeStruct((b, ML_DIM, s), BF16)],
        scratch_shapes=[pltpu.VMEM((tm + 2 * CONV_HALO, COL_TILE), F32)],
        compiler_params=_params(("arbitrary", "arbitrary", "arbitrary"), 32),
        name="conv_qk",
    )(proj3, proj3, proj3, conv_w, conv_b)


def _gate_kernel(g_ref, b_ref, o_ref):
    g = g_ref[0] + b_ref[...]
    width = g.shape[1]
    logsig = jnp.minimum(g, 0.0) - jnp.log(1.0 + jnp.exp(-jnp.abs(g)))
    lane = lax.broadcasted_iota(jnp.int32, g.shape, 1) % ML_CHUNK
    pre = logsig
    suf = logsig
    step = 1
    while step < ML_CHUNK:
        pre = pre + jnp.where(lane >= step, pltpu.roll(pre, step, 1), 0.0)
        suf = suf + jnp.where(lane < ML_CHUNK - step, pltpu.roll(suf, width - step, 1), 0.0)
        step *= 2
    row = lax.broadcasted_iota(jnp.int32, g.shape, 0)
    is_f_fwd = (row >= ML_HEADS) & (row < 2 * ML_HEADS)
    is_f_bwd = row >= 3 * ML_HEADS
    o_ref[0] = jnp.where(is_f_fwd, pre, jnp.where(is_f_bwd, suf, g))


def _gate_prep(gates_t, gate_b):
    b, rows, s = gates_t.shape
    sb = min(2048, s)
    return pl.pallas_call(
        _gate_kernel,
        grid=(b, s // sb),
        in_specs=[
            pl.BlockSpec((1, rows, sb), lambda bi, i: (bi, 0, i)),
            pl.BlockSpec((rows, 1), lambda bi, i: (0, 0)),
        ],
        out_specs=pl.BlockSpec((1, rows, sb), lambda bi, i: (bi, 0, i)),
        out_shape=jax.ShapeDtypeStruct((b, rows, s), F32),
        compiler_params=_params(("arbitrary", "arbitrary"), 32),
        name="gate_prep",
    )(gates_t, gate_b)


ML_CHAIN_GROUP = 8


def _mlstm_kernel(qf_ref, kf_ref, vf_ref, qb_ref, kb_ref, vb_ref, grf_ref, grb_ref, gcf_ref, gcb_ref,
                  hf_ref, hb_ref, s_scr, m_scr):
    c = pl.program_id(1)

    @pl.when(c == 0)
    def _():
        def clear(i, carry):
            s_scr[i] = jnp.zeros(s_scr.shape[1:], F32)
            return carry

        lax.fori_loop(0, s_scr.shape[0], clear, 0)
        m_scr[...] = jnp.zeros_like(m_scr)

    t_i = lax.broadcasted_iota(jnp.int32, (ML_CHUNK, ML_CHUNK), 0)
    s_i = lax.broadcasted_iota(jnp.int32, (ML_CHUNK, ML_CHUNK), 1)
    ones_col = jnp.where(lax.broadcasted_iota(jnp.int32, (ML_CHUNK, LANES), 1) == 0, 1.0, 0.0).astype(BF16)
    dirs = ((qf_ref, kf_ref, vf_ref, grf_ref, gcf_ref, hf_ref), (qb_ref, kb_ref, vb_ref, grb_ref, gcb_ref, hb_ref))
    all_chains = [(dirn, head) + refs for dirn, refs in enumerate(dirs) for head in range(ML_HEADS)]

    for first in range(0, len(all_chains), ML_CHAIN_GROUP):
        chains = all_chains[first:first + ML_CHAIN_GROUP]
        matmuls = []
        for dirn, head, q_ref, k_ref, v_ref, _, _, _ in chains:
            hs = slice(head * ML_HEAD_DIM, (head + 1) * ML_HEAD_DIM)
            q = q_ref[0, :, hs]
            qk_raw = jnp.dot(q, k_ref[0, hs, :], preferred_element_type=F32)
            q_state = jnp.dot(q, s_scr[dirn * ML_HEADS + head].astype(BF16), preferred_element_type=F32)
            matmuls.append((qk_raw, q_state))

        weights = []
        for dirn, head, _, _, _, gr_ref, gc_ref, _ in chains:
            ii = dirn * 2 * ML_HEADS + head
            bi = ii + ML_HEADS
            mask = (s_i <= t_i) if dirn == 0 else (s_i >= t_i)
            i_row, b_row = gr_ref[0, ii:ii + 1, :], gr_ref[0, bi:bi + 1, :]
            b_col = gc_ref[0, :, bi:bi + 1]
            m = m_scr[dirn * ML_HEADS + head][0:1, 0:1]
            dmat = jnp.where(mask, b_col - b_row + i_row, NEG_INF)
            inter = b_col + m
            m_t = jnp.maximum(inter, jnp.max(dmat, axis=-1, keepdims=True))
            weights.append((jnp.exp(dmat - m_t), jnp.exp(inter - m_t), jnp.exp(-m_t)))

        for (dirn, head, _, _, v_ref, _, _, h_ref), (qk_raw, q_state), (w_intra, w_inter, floor) in zip(
                chains, matmuls, weights):
            hs = slice(head * ML_HEAD_DIM, (head + 1) * ML_HEAD_DIM)
            v_ext = jnp.concatenate([v_ref[0, :, hs], ones_col], axis=1)
            num = w_inter * q_state + jnp.dot((qk_raw * w_intra).astype(BF16), v_ext, preferred_element_type=F32)
            den = jnp.maximum(jnp.abs(num[:, ML_HEAD_DIM:ML_HEAD_DIM + 1]), floor)
            h_ref[0, :, hs] = (num[:, :ML_HEAD_DIM] / den).astype(BF16)

        for dirn, head, _, k_ref, v_ref, gr_ref, _, _ in chains:
            idx = dirn * ML_HEADS + head
            ii = dirn * 2 * ML_HEADS + head
            bi = ii + ML_HEADS
            hs = slice(head * ML_HEAD_DIM, (head + 1) * ML_HEAD_DIM)
            i_row, b_row = gr_ref[0, ii:ii + 1, :], gr_ref[0, bi:bi + 1, :]
            b_last = b_row[:, ML_CHUNK - 1:ML_CHUNK] if dirn == 0 else b_row[:, 0:1]
            m = m_scr[idx][0:1, 0:1]
            log_w = b_last - b_row + i_row
            m_new = jnp.maximum(b_last + m, jnp.max(log_w, axis=-1, keepdims=True))
            decay = jnp.exp(b_last + m - m_new)
            v_ext = jnp.concatenate([v_ref[0, :, hs], ones_col], axis=1)
            wk_t = (k_ref[0, hs, :].astype(F32) * jnp.exp(log_w - m_new)).astype(BF16)
            s_scr[idx] = decay * s_scr[idx] + jnp.dot(wk_t, v_ext, preferred_element_type=F32)
            m_scr[idx] = jnp.broadcast_to(m_new, m_scr.shape[1:])


def _mlstm(proj3, q, k_t, g_rows, g_cols):
    b, s, _ = proj3.shape
    nc = s // ML_CHUNK
    v_tile = 2 * ML_DIM // ML_DIM
    n_rows = g_rows.shape[1]

    def rows(col, reverse):
        return pl.BlockSpec((1, ML_CHUNK, ML_DIM), lambda bi, c: (bi, nc - 1 - c if reverse else c, col))

    def cols(height, reverse):
        return pl.BlockSpec((1, height, ML_CHUNK), lambda bi, c: (bi, 0, nc - 1 - c if reverse else c))

    return pl.pallas_call(
        _mlstm_kernel,
        grid=(b, nc),
        in_specs=[
            rows(0, False), cols(ML_DIM, False), rows(v_tile, False),
            rows(0, True), cols(ML_DIM, True), rows(v_tile, True),
            cols(n_rows, False), cols(n_rows, True),
            pl.BlockSpec((1, ML_CHUNK, n_rows), lambda bi, c: (bi, c, 0)),
            pl.BlockSpec((1, ML_CHUNK, n_rows), lambda bi, c: (bi, nc - 1 - c, 0)),
        ],
        out_specs=[rows(0, False), rows(0, True)],
        out_shape=[jax.ShapeDtypeStruct((b, s, ML_DIM), BF16)] * 2,
        scratch_shapes=[
            pltpu.VMEM((2 * ML_HEADS, ML_HEAD_DIM, ML_EXT), F32),
            pltpu.VMEM((2 * ML_HEADS, 8, LANES), F32),
        ],
        compiler_params=_params(("arbitrary", "arbitrary"), 32),
        name="mlstm",
    )(q, k_t, proj3, q, k_t, proj3, g_rows, g_rows, g_cols, g_cols)


def _pack_bf16_pair(lo, hi):
    lo_bits = lax.bitcast_convert_type(lo.astype(BF16).astype(F32), jnp.uint32)
    hi_bits = lax.bitcast_convert_type(hi.astype(BF16).astype(F32), jnp.uint32)
    return (hi_bits & jnp.uint32(0xFFFF0000)) | (lo_bits >> 16)


def _unpack_bf16_pair(packed):
    lo = lax.bitcast_convert_type(packed << 16, F32)
    hi = lax.bitcast_convert_type(packed & jnp.uint32(0xFFFF0000), F32)
    return lo, hi


MERGE_CHUNK = 512


def _merge_kernel(x_ref, ao_ref, hf_ref, hb_ref, mo_ref, mg_ref, wpa_ref, wpm_ref, wo_ref, nfw_ref,
                  rw_ref, rb_ref, x1_ref, h2_ref, idx_ref, wt_ref):
    tm = x_ref.shape[0]
    chunk = min(MERGE_CHUNK, tm)
    for c in range(tm // chunk):
        rows = slice(c * chunk, (c + 1) * chunk)
        y_attn = jnp.dot(ao_ref[rows, :], wpa_ref[...], preferred_element_type=F32)
        hsum = hf_ref[rows, :].astype(F32) + hb_ref[rows, :].astype(F32)
        ml = (_sigmoid(mo_ref[rows, :].astype(F32)) * hsum).astype(BF16)
        y_ml = jnp.dot(ml, wpm_ref[...], preferred_element_type=F32)
        gates = _sigmoid(mg_ref[rows, :].astype(F32))
        mixed = (gates[:, :D_MODEL] * y_attn + gates[:, D_MODEL:] * y_ml).astype(BF16)
        x1 = x_ref[rows, :] + jnp.dot(mixed, wo_ref[...], preferred_element_type=F32)
        x1_ref[rows, :] = x1
        ms = jnp.mean(x1 * x1, axis=-1, keepdims=True)
        h2 = x1 * lax.rsqrt(ms + RMS_EPS) * nfw_ref[...]
        h2_ref[rows, :] = _pack_bf16_pair(h2[:, :D_MODEL // 2], h2[:, D_MODEL // 2:])

        logits = lax.dot_general(rw_ref[...], h2, (((1,), (1,)), ((), ())), preferred_element_type=F32,
                                 precision=lax.Precision.HIGHEST) + rb_ref[...]
        row = lax.broadcasted_iota(jnp.int32, logits.shape, 0)
        rest = logits
        vals = []
        for k in range(TOP_K):
            mx = jnp.max(rest, axis=0, keepdims=True)
            first = jnp.min(jnp.where(rest == mx, row, N_EXPERTS), axis=0, keepdims=True)
            vals.append(mx)
            idx_ref[k:k + 1, rows] = first
            rest = jnp.where(row == first, -jnp.inf, rest)
        exps = [jnp.exp(v - vals[0]) for v in vals]
        tot = exps[0] + exps[1] + exps[2] + exps[3]
        for k in range(TOP_K):
            wt_ref[k:k + 1, rows] = exps[k] / tot


def _merge(x2d, attn_o, h_f, h_b, proj, wpa, wpm, wo, nfw, rw_t, rb):
    t = x2d.shape[0]
    tm = min(512, t)
    ml_o_tile = ML_QKV_COLS // ML_O_COLS
    merge_tile = (ML_QKV_COLS + ML_O_COLS) // MERGE_COLS
    row = lambda width: pl.BlockSpec((tm, width), lambda i: (i, 0))
    full = lambda a: pl.BlockSpec(a.shape, lambda i: (0, 0))
    return pl.pallas_call(
        _merge_kernel,
        grid=(t // tm,),
        in_specs=[
            row(D_MODEL), row(ATTN_DIM), row(ML_DIM), row(ML_DIM),
            pl.BlockSpec((tm, ML_O_COLS), lambda i: (i, ml_o_tile)),
            pl.BlockSpec((tm, MERGE_COLS), lambda i: (i, merge_tile)),
            full(wpa), full(wpm), full(wo), full(nfw), full(rw_t), full(rb),
        ],
        out_specs=[row(D_MODEL), row(D_MODEL // 2), pl.BlockSpec((TOP_K, tm), lambda i: (0, i)),
                   pl.BlockSpec((TOP_K, tm), lambda i: (0, i))],
        out_shape=[
            jax.ShapeDtypeStruct((t, D_MODEL), F32),
            jax.ShapeDtypeStruct((t, D_MODEL // 2), jnp.uint32),
            jax.ShapeDtypeStruct((TOP_K, t), jnp.int32),
            jax.ShapeDtypeStruct((TOP_K, t), F32),
        ],
        compiler_params=_params(("arbitrary",), 48),
        name="merge",
    )(x2d, attn_o, h_f, h_b, proj, proj, wpa, wpm, wo, nfw, rw_t, rb)


ROUTE_TILE = 512


def _route_kernel(idx_ref, pos_ref, cnt_ref, base):
    @pl.when(pl.program_id(0) == 0)
    def _():
        base[...] = jnp.zeros_like(base)

    idx = idx_ref[...]
    row = lax.broadcasted_iota(jnp.int32, (N_EXPERTS, ROUTE_TILE), 0)
    onehot = jnp.zeros((N_EXPERTS, ROUTE_TILE), F32)
    for k in range(TOP_K):
        onehot = onehot + jnp.where(row == idx[k:k + 1, :], 1.0, 0.0)
    s_i = lax.broadcasted_iota(jnp.int32, (ROUTE_TILE, ROUTE_TILE), 0)
    t_i = lax.broadcasted_iota(jnp.int32, (ROUTE_TILE, ROUTE_TILE), 1)
    upper = jnp.where(s_i <= t_i, 1.0, 0.0).astype(BF16)
    incl = jnp.dot(onehot.astype(BF16), upper, preferred_element_type=F32)
    before = base[:, 0:1]
    count = incl + before
    for k in range(TOP_K):
        mine = jnp.sum(jnp.where(row == idx[k:k + 1, :], count, 0.0), axis=0, keepdims=True)
        pos_ref[k:k + 1, :] = (mine - 1.0).astype(jnp.int32)
    total = before + incl[:, ROUTE_TILE - 1:ROUTE_TILE]
    base[...] = jnp.broadcast_to(total, base.shape)
    cnt_ref[...] = jnp.broadcast_to(total, cnt_ref.shape)


def _route(idx):
    t = idx.shape[1]
    return pl.pallas_call(
        _route_kernel,
        grid=(t // ROUTE_TILE,),
        in_specs=[pl.BlockSpec((TOP_K, ROUTE_TILE), lambda i: (0, i))],
        out_specs=[pl.BlockSpec((TOP_K, ROUTE_TILE), lambda i: (0, i)),
                   pl.BlockSpec((N_EXPERTS, LANES), lambda i: (0, 0))],
        out_shape=[jax.ShapeDtypeStruct((TOP_K, t), jnp.int32),
                   jax.ShapeDtypeStruct((N_EXPERTS, LANES), F32)],
        scratch_shapes=[pltpu.VMEM((N_EXPERTS, LANES), F32)],
        compiler_params=_params(("arbitrary",), 32),
        name="route",
    )(idx)


GROUP_TILE = 512
DISPATCH_TILE = 1024
ROW_WORDS = D_MODEL // 2


def _dispatch_kernel(slot_hbm, ztile_ref, nu_ref, *refs, steps):
    srcs = refs[:len(steps)]
    dst_hbm, slot_smem, zbuf, sem = refs[len(steps):]
    i = pl.program_id(0)
    n_slots = DISPATCH_TILE * TOP_K
    n_tiles = dst_hbm.shape[0] // GROUP_TILE
    load = pltpu.make_async_copy(slot_hbm.at[pl.ds(i * n_slots, n_slots)], slot_smem, sem.at[0])
    load.start()

    @pl.when(i == 0)
    def _():
        zbuf[...] = jnp.zeros_like(zbuf)

        def fill_tile(tile):
            pltpu.make_async_copy(zbuf, dst_hbm.at[pl.ds(tile * GROUP_TILE, GROUP_TILE), :], sem.at[1]).start()

        def fill_last(e, n):
            tile = ztile_ref[e]

            @pl.when(tile >= 0)
            def _():
                fill_tile(tile)

            return n + jnp.where(tile >= 0, 1, 0)

        def fill_unused(tile, c):
            fill_tile(tile)
            return c

        n_fill = lax.fori_loop(0, N_EXPERTS, fill_last, 0)
        lax.fori_loop(nu_ref[0], n_tiles, fill_unused, 0)

        def drain(_, c):
            pltpu.make_async_copy(zbuf, dst_hbm.at[pl.ds(0, GROUP_TILE), :], sem.at[1]).wait()
            return c

        lax.fori_loop(0, n_fill + n_tiles - nu_ref[0], drain, 0)

    load.wait()

    first = 0
    for src_ref, n_steps in zip(srcs, steps):
        @pl.when((i >= first) & (i < first + n_steps))
        def _(src_ref=src_ref):
            def issue(t, c):
                src = src_ref.at[pl.ds(t, 1), :]
                for k in range(TOP_K):
                    slot = slot_smem[t * TOP_K + k]
                    pltpu.make_async_copy(src, dst_hbm.at[pl.ds(slot, 1), :], sem.at[1]).start(priority=k % 2)
                return c

            lax.fori_loop(0, DISPATCH_TILE, issue, 0)

        first += n_steps
    pltpu.make_async_copy(dst_hbm.at[pl.ds(0, n_slots), :], dst_hbm.at[pl.ds(0, n_slots), :], sem.at[1]).wait()


def _dispatch(slots, ztile, n_used, sources, n_rows):
    steps = tuple(h.shape[0] // DISPATCH_TILE for h in sources)
    any_spec = pl.BlockSpec(memory_space=pl.ANY)
    smem_spec = pl.BlockSpec(memory_space=pltpu.SMEM)
    src_specs = []
    first = 0
    for n_steps in steps:
        src_specs.append(pl.BlockSpec((DISPATCH_TILE, ROW_WORDS),
                                      lambda i, first=first, n_steps=n_steps: (jnp.clip(i - first, 0, n_steps - 1), 0)))
        first += n_steps
    return pl.pallas_call(
        functools.partial(_dispatch_kernel, steps=steps),
        grid=(sum(steps),),
        in_specs=[any_spec, smem_spec, smem_spec] + src_specs,
        out_specs=any_spec,
        out_shape=jax.ShapeDtypeStruct((n_rows, ROW_WORDS), jnp.uint32),
        scratch_shapes=[
            pltpu.SMEM((DISPATCH_TILE * TOP_K,), jnp.int32),
            pltpu.VMEM((GROUP_TILE, ROW_WORDS), jnp.uint32),
            pltpu.SemaphoreType.DMA((2,)),
        ],
        compiler_params=_params(("arbitrary",), 32),
        name="dispatch",
    )(slots, ztile, n_used, *sources)


EXPERT_CHUNK = 256


def _expert_kernel(te_ref, ts_ref, nu_ref, x_ref, wgu_ref, bgu_ref, wd_ref, bd_ref, y_ref):
    j = pl.program_id(0)

    @pl.when(j < nu_ref[0])
    def _():
        half = D_MODEL // 2
        for c in range(GROUP_TILE // EXPERT_CHUNK):
            rows = slice(c * EXPERT_CHUNK, (c + 1) * EXPERT_CHUNK)
            lo, hi = _unpack_bf16_pair(x_ref[rows, :])
            gu = (jnp.dot(lo.astype(BF16), wgu_ref[0, :half, :], preferred_element_type=F32)
                  + jnp.dot(hi.astype(BF16), wgu_ref[0, half:, :], preferred_element_type=F32) + bgu_ref[0])
            gate = jnp.minimum(gu[:, :D_FF], SWIGLU_LIMIT)
            up = jnp.clip(gu[:, D_FF:], -SWIGLU_LIMIT, SWIGLU_LIMIT)
            hid = (up + 1.0) * gate * _sigmoid(SWIGLU_ALPHA * gate)
            y = jnp.dot(hid.astype(BF16), wd_ref[0], preferred_element_type=F32) + bd_ref[0]
            y_ref[rows, :] = _pack_bf16_pair(y[:, :half], y[:, half:])

    @pl.when(j >= nu_ref[0])
    def _():
        y_ref[...] = jnp.zeros_like(y_ref)


def _experts(tile_expert, tile_src, n_used, xs, wgu, bgu, wd, bd):
    n_rows = xs.shape[0]
    row_spec = pl.BlockSpec((GROUP_TILE, ROW_WORDS), lambda j, te, ts, nu: (ts[j], 0))
    out_spec = pl.BlockSpec((GROUP_TILE, ROW_WORDS), lambda j, te, ts, nu: (j, 0))
    return pl.pallas_call(
        _expert_kernel,
        grid_spec=pltpu.PrefetchScalarGridSpec(
            num_scalar_prefetch=3,
            grid=(n_rows // GROUP_TILE,),
            in_specs=[
                row_spec,
                pl.BlockSpec((1, D_MODEL, 2 * D_FF), lambda j, te, ts, nu: (te[j], 0, 0)),
                pl.BlockSpec((1, 1, 2 * D_FF), lambda j, te, ts, nu: (te[j], 0, 0)),
                pl.BlockSpec((1, D_FF, D_MODEL), lambda j, te, ts, nu: (te[j], 0, 0)),
                pl.BlockSpec((1, 1, D_MODEL), lambda j, te, ts, nu: (te[j], 0, 0)),
            ],
            out_specs=out_spec,
        ),
        out_shape=jax.ShapeDtypeStruct((n_rows, ROW_WORDS), jnp.uint32),
        compiler_params=_params(("arbitrary",), 56),
        name="experts",
    )(tile_expert, tile_src, n_used, xs, wgu, bgu, wd, bd)


COMBINE_TILE = 256


def _combine_kernel(slot_hbm, ys_hbm, x1_ref, wt_ref, nw_ref, o_ref, slot_a, slot_b, buf_a, buf_b, sem):
    i = pl.program_id(0)
    n = pl.num_programs(0)
    n_slots = COMBINE_TILE * TOP_K
    slot_bufs = (slot_a, slot_b)
    row_bufs = (buf_a, buf_b)

    def slot_load(tile, par):
        return pltpu.make_async_copy(slot_hbm.at[pl.ds(tile * n_slots, n_slots)], slot_bufs[par], sem.at[par])

    def issue_rows(par):
        def issue(t, c):
            for k in range(TOP_K):
                slot = slot_bufs[par][t * TOP_K + k]
                pltpu.make_async_copy(ys_hbm.at[pl.ds(slot, 1), :], row_bufs[par].at[k, pl.ds(t, 1), :],
                                      sem.at[2 + par]).start(priority=k % 2)
            return c

        lax.fori_loop(0, COMBINE_TILE, issue, 0)

    def reduce_rows(par):
        pltpu.make_async_copy(ys_hbm.at[pl.ds(0, n_slots), :], ys_hbm.at[pl.ds(0, n_slots), :],
                              sem.at[2 + par]).wait()
        half = D_MODEL // 2
        wt = wt_ref[...]
        acc_lo = x1_ref[:, :half]
        acc_hi = x1_ref[:, half:]
        for k in range(TOP_K):
            lo, hi = _unpack_bf16_pair(row_bufs[par][k])
            acc_lo = acc_lo + wt[:, k:k + 1] * lo
            acc_hi = acc_hi + wt[:, k:k + 1] * hi
        ms = (jnp.sum(acc_lo * acc_lo, axis=-1, keepdims=True)
              + jnp.sum(acc_hi * acc_hi, axis=-1, keepdims=True)) * (1.0 / D_MODEL)
        inv = lax.rsqrt(ms + RMS_EPS)
        o_ref[:, :half] = acc_lo * inv * nw_ref[:, :half]
        o_ref[:, half:] = acc_hi * inv * nw_ref[:, half:]

    @pl.when(i == 0)
    def _():
        first = slot_load(0, 0)
        first.start()
        first.wait()
        issue_rows(0)

        @pl.when(n > 1)
        def _():
            slot_load(1, 1).start()

    for par in range(2):
        @pl.when(i % 2 == par)
        def _(par=par):
            @pl.when(i + 1 < n)
            def _():
                slot_load(i + 1, 1 - par).wait()
                issue_rows(1 - par)

            @pl.when(i + 2 < n)
            def _():
                slot_load(i + 2, par).start()

            reduce_rows(par)


def _combine(slots, ys, x1, wt, nw):
    t = x1.shape[0]
    tm = COMBINE_TILE
    return pl.pallas_call(
        _combine_kernel,
        grid=(t // tm,),
        in_specs=[
            pl.BlockSpec(memory_space=pl.ANY),
            pl.BlockSpec(memory_space=pl.ANY),
            pl.BlockSpec((tm, D_MODEL), lambda i: (i, 0)),
            pl.BlockSpec((tm, TOP_K), lambda i: (i, 0)),
            pl.BlockSpec((1, D_MODEL), lambda i: (0, 0)),
        ],
        out_specs=pl.BlockSpec((tm, D_MODEL), lambda i: (i, 0)),
        out_shape=jax.ShapeDtypeStruct((t, D_MODEL), F32),
        scratch_shapes=[
            pltpu.SMEM((tm * TOP_K,), jnp.int32),
            pltpu.SMEM((tm * TOP_K,), jnp.int32),
            pltpu.VMEM((TOP_K, tm, ROW_WORDS), jnp.uint32),
            pltpu.VMEM((TOP_K, tm, ROW_WORDS), jnp.uint32),
            pltpu.SemaphoreType.DMA((4,)),
        ],
        compiler_params=_params(("arbitrary",), 32),
        name="combine",
    )(slots, ys, x1, wt, nw)


def _moe(parts, p, final_w):
    sizes = [x1.shape[0] for x1, _, _, _ in parts]
    idx = jnp.concatenate([q[2] for q in parts], axis=1)
    t_all = idx.shape[1]
    pos, cnt = _route(idx)
    counts = cnt[:, 0].astype(jnp.int32)
    tiles_e = (counts + GROUP_TILE - 1) // GROUP_TILE
    tile_end = jnp.cumsum(tiles_e)
    tile_start = tile_end - tiles_e
    n_used = tile_end[-1]
    n_tiles = (t_all * TOP_K) // GROUP_TILE + N_EXPERTS
    n_rows = n_tiles * GROUP_TILE
    first_row = tile_start * GROUP_TILE
    experts = jnp.arange(N_EXPERTS, dtype=jnp.int32)[:, None, None]
    base = jnp.sum(jnp.where(idx[None] == experts, first_row[:, None, None], 0), axis=0)
    slots = (base + pos).T.reshape(-1)
    tile_id = jnp.arange(n_tiles, dtype=jnp.int32)
    tile_src = jnp.minimum(tile_id, n_used - 1).astype(jnp.int32)
    tile_expert = jnp.sum(tile_src[:, None] >= tile_end[None, :], axis=1).astype(jnp.int32)
    tile_expert = jnp.minimum(tile_expert, N_EXPERTS - 1)
    ztile = jnp.where(counts > 0, tile_end - 1, -1).astype(jnp.int32)

    n_used = n_used.reshape(1).astype(jnp.int32)
    xs = _dispatch(slots, ztile, n_used, [q[1] for q in parts], n_rows)
    ys = _experts(tile_expert, tile_src, n_used, xs, p["wgu"], p["bgu"], p["wd"], p["bd"])
    outs = []
    off = 0
    for (x1, _, _, wt), t in zip(parts, sizes):
        outs.append(_combine(lax.slice(slots, (off * TOP_K,), ((off + t) * TOP_K,)), ys, x1, wt.T, final_w))
        off += t
    return outs


def _rope_tables(seq):
    half = ATTN_HEAD_DIM // 2
    inv_freq = ROPE_THETA ** (-jnp.arange(half, dtype=F32) / half)
    ang = jnp.arange(seq, dtype=F32)[:, None] * inv_freq[None, :]
    cos, sin = jnp.cos(ang), jnp.sin(ang)
    reps = LANES // ATTN_HEAD_DIM
    cos_t = jnp.tile(jnp.concatenate([cos, cos], axis=1), (1, reps))
    sin_t = jnp.tile(jnp.concatenate([-sin, sin], axis=1), (1, reps))
    return cos_t, sin_t


def _pack_layer(w_in, conv_w, conv_b, gate_b, wpa, wpm, wo, nmw, nfw, rw, rb, wgu, bgu, wd, bd):
    c0 = ATTN_QKV_COLS
    c1 = c0 + ML_QKV_COLS
    c2 = c1 + ML_O_COLS
    c3 = c2 + ML_GATE_COLS
    w_attn = w_in[:, :c0].reshape(D_MODEL, N_GROUPS, 3, ATTN_DIM)
    w_attn = w_attn * jnp.array([ATTN_HEAD_DIM ** -0.5, 1.0, 1.0], F32)[None, None, :, None]
    w_main = jnp.concatenate([w_in[:, c0:c2], w_in[:, c3:], w_attn.reshape(D_MODEL, c0)], axis=1).astype(BF16)
    w_gate = jnp.pad(w_in[:, c2:c3], ((0, 0), (0, LANES - ML_GATE_COLS))).astype(BF16)
    return dict(
        w_main=w_main, w_gate=w_gate, conv_w=conv_w, conv_b=conv_b.reshape(1, -1),
        gate_b=gate_b.reshape(-1, 1), wpa=wpa.astype(BF16), wpm=wpm.astype(BF16), wo=wo.astype(BF16),
        nmw=nmw.reshape(1, -1), nfw=nfw.reshape(1, -1), rw_t=rw.T, rb=rb.reshape(-1, 1),
        wgu=wgu.astype(BF16), bgu=bgu.reshape(N_EXPERTS, 1, -1), wd=wd.astype(BF16),
        bd=bd.reshape(N_EXPERTS, 1, -1))


def _layer(x, p, tables):
    b, s, _ = x.shape
    x2d = x.reshape(b * s, D_MODEL)
    proj, gates, qkv1, qkv2 = _inproj(x2d, p["nmw"], p["w_main"], p["w_gate"], tables[0], tables[1], s)
    proj3 = proj.reshape(b, s, N_MAIN)

    merged = _attn_plain(proj3)
    for g, qkv in ((1, qkv1), (2, qkv2)):
        merged = _attn_dilated(qkv, g, ATTN_PATTERNS[g][1], merged, b, s)
    attn_o = merged[0]

    ml_q, ml_kt = _conv_qk(proj3, p["conv_w"], p["conv_b"])
    gates_t = gates[:, :ML_GATE_COLS].reshape(b, s, ML_GATE_COLS).transpose(0, 2, 1)
    g_rows = _gate_prep(gates_t, p["gate_b"])
    g_cols = g_rows.transpose(0, 2, 1)
    h_f, h_b = _mlstm(proj3, ml_q, ml_kt, g_rows, g_cols)

    return _merge(x2d, attn_o, h_f.reshape(b * s, ML_DIM), h_b.reshape(b * s, ML_DIM), proj,
                  p["wpa"], p["wpm"], p["wo"], p["nfw"], p["rw_t"], p["rb"])


def kernel(x_prompt, x_sample, norm_mix_w, w_in, mlstm_conv_w, mlstm_conv_b, mlstm_gate_b, w_proj_attn,
           w_proj_mlstm, w_out, norm_ffn_w, router_w, router_b, expert_w_gu, expert_b_gu, expert_w_down,
           expert_b_down, norm_final_w):
    depth = w_in.shape[0]
    assert depth == 1, "the final RMSNorm is fused into the last layer's MoE kernel"
    p = _pack_layer(w_in[0], mlstm_conv_w[0], mlstm_conv_b[0], mlstm_gate_b[0], w_proj_attn[0],
                    w_proj_mlstm[0], w_out[0], norm_mix_w[0], norm_ffn_w[0], router_w[0], router_b[0],
                    expert_w_gu[0], expert_b_gu[0], expert_w_down[0], expert_b_down[0])
    final_w = norm_final_w.reshape(1, -1)
    xs = (x_prompt, x_sample)
    parts = [_layer(x, p, _rope_tables(x.shape[1])) for x in xs]
    outs = _moe(parts, p, final_w)
    return tuple(o.reshape(x.shape) for o, x in zip(outs, xs))
```

```python
import functools

import jax
import jax.numpy as jnp
from jax import lax
from jax.experimental import pallas as pl
from jax.experimental.pallas import tpu as pltpu

F32 = jnp.float32
BF16 = jnp.bfloat16

D_MODEL = 1024
ATTN_PATTERNS = ((128, 1), (512, 4), (2048, 16))
N_GROUPS = 3
ATTN_HEADS = 8
ATTN_HEAD_DIM = 64
ATTN_DIM = ATTN_HEADS * ATTN_HEAD_DIM
ATTN_BLOCK = 64
ROPE_THETA = 10000.0
ML_DIM = D_MODEL
ML_HEADS = 4
ML_HEAD_DIM = ML_DIM // ML_HEADS
ML_CHUNK = 256
CONV_W = 5
ATTN_QKV_COLS = N_GROUPS * 3 * ATTN_DIM
ML_QKV_COLS = 3 * ML_DIM
ML_O_COLS = ML_DIM
ML_GATE_COLS = 4 * ML_HEADS
MERGE_COLS = 2 * D_MODEL
N_EXPERTS = 32
TOP_K = 4
D_FF = D_MODEL
SWIGLU_LIMIT = 7.0
SWIGLU_ALPHA = 1.702
RMS_EPS = 1e-5
NEG_INF = -1e30

LANES = 128
COL_TILE = 512
N_PACKED = ML_QKV_COLS + ML_O_COLS + MERGE_COLS + ATTN_QKV_COLS
N_COL_TILES = N_PACKED // COL_TILE
ATTN_TILE0 = (ML_QKV_COLS + ML_O_COLS + MERGE_COLS) // COL_TILE
GROUP_TILES = 3 * ATTN_DIM // COL_TILE
N_MAIN_TILES = ATTN_TILE0 + GROUP_TILES
N_MAIN = N_MAIN_TILES * COL_TILE
ML_EXT = ML_HEAD_DIM + LANES


def _params(sem, vmem_mb):
    return pltpu.CompilerParams(dimension_semantics=sem, vmem_limit_bytes=vmem_mb * 1024 * 1024)


def _sigmoid(x):
    return 1.0 / (1.0 + jnp.exp(-x))


INPROJ_CHUNK = 512
WIDE_TILE = 3 * ATTN_DIM
N_WIDE_TILES = N_PACKED // WIDE_TILE
N_PLAIN_WIDE = ATTN_TILE0 * COL_TILE // WIDE_TILE


def _inproj_kernel(x_ref, nw_ref, w_ref, wg_ref, cos_ref, sin_ref, out_ref, gates_ref, g1_ref, g2_ref,
                   h_scr, slab, slab2):
    j = pl.program_id(1)
    tm = x_ref.shape[0]
    n_slabs = WIDE_TILE // LANES

    @pl.when(j == 0)
    def _():
        x = x_ref[...]
        ms = jnp.mean(x * x, axis=-1, keepdims=True)
        h = (x * lax.rsqrt(ms + RMS_EPS) * nw_ref[...]).astype(BF16)
        h_scr[...] = h
        gates_ref[...] = jnp.dot(h, wg_ref[...], preferred_element_type=F32)

    chunk = min(INPROJ_CHUNK, tm)

    def chunks():
        for c in range(tm // chunk):
            rows = slice(c * chunk, (c + 1) * chunk)
            yield c, rows, jnp.dot(h_scr[rows, :], w_ref[...], preferred_element_type=F32)

    def rope(a, rows):
        reps = ATTN_DIM // LANES
        c = jnp.concatenate([cos_ref[rows, :]] * reps, axis=1)
        s = jnp.concatenate([sin_ref[rows, :]] * reps, axis=1)
        lane = lax.broadcasted_iota(jnp.int32, a.shape, 1)
        half = ATTN_HEAD_DIM // 2
        first = (lane % ATTN_HEAD_DIM) < half
        sw = jnp.where(first, pltpu.roll(a, ATTN_DIM - half, 1), pltpu.roll(a, half, 1))
        return a * c + sw * s

    def rotated(acc, rows):
        return jnp.concatenate([rope(acc[:, :ATTN_DIM], rows), rope(acc[:, ATTN_DIM:2 * ATTN_DIM], rows),
                                acc[:, 2 * ATTN_DIM:]], axis=1)

    def deinterleave(val, c, dst_ref, dilation):
        n = chunk // dilation
        for s in range(n_slabs):
            slab[s] = val[:, s * LANES:(s + 1) * LANES]
        if dilation == 16:
            quarter = chunk // 4
            for s in range(n_slabs):
                for r4 in range(4):
                    slab2[s, r4 * quarter:(r4 + 1) * quarter, :] = slab[s, pl.ds(r4, quarter, stride=4), :]
            for r in range(dilation):
                start = (r % 4) * quarter + r // 4
                piece = jnp.concatenate([slab2[s, pl.ds(start, n, stride=4), :] for s in range(n_slabs)], axis=1)
                dst_ref[0, r, c * n:(c + 1) * n, :] = piece.astype(BF16)
            return
        for r in range(dilation):
            piece = jnp.concatenate([slab[s, pl.ds(r, n, stride=dilation), :] for s in range(n_slabs)], axis=1)
            dst_ref[0, r, c * n:(c + 1) * n, :] = piece.astype(BF16)

    @pl.when(j < N_PLAIN_WIDE)
    def _():
        for _, rows, acc in chunks():
            out_ref[rows, :] = acc.astype(BF16)

    dst = (out_ref, g1_ref, g2_ref)
    for g, (_, dilation) in enumerate(ATTN_PATTERNS):
        @pl.when(j == N_PLAIN_WIDE + g)
        def _(g=g, dilation=dilation):
            for c, rows, acc in chunks():
                val = rotated(acc, rows)
                if dilation == 1:
                    out_ref[rows, :] = val.astype(BF16)
                else:
                    deinterleave(val, c, dst[g], dilation)


def _inproj(x2d, nw, w_main, w_gate, cos_t, sin_t, seq):
    t = x2d.shape[0]
    tm = min(1024, seq)
    tiles_per_seq = seq // tm
    chunk = min(INPROJ_CHUNK, tm)

    def group_out(dilation):
        return (pl.BlockSpec((1, dilation, tm // dilation, WIDE_TILE), lambda i, j: (i, 0, 0, 0)),
                jax.ShapeDtypeStruct((t // tm, dilation, tm // dilation, WIDE_TILE), BF16))

    (g1_spec, g1_shape), (g2_spec, g2_shape) = [group_out(ATTN_PATTERNS[g][1]) for g in (1, 2)]
    return pl.pallas_call(
        _inproj_kernel,
        grid=(t // tm, N_WIDE_TILES),
        in_specs=[
            pl.BlockSpec((tm, D_MODEL), lambda i, j: (i, 0)),
            pl.BlockSpec((1, D_MODEL), lambda i, j: (0, 0)),
            pl.BlockSpec((D_MODEL, WIDE_TILE), lambda i, j: (0, j)),
            pl.BlockSpec((D_MODEL, LANES), lambda i, j: (0, 0)),
            pl.BlockSpec((tm, LANES), lambda i, j: (i % tiles_per_seq, 0)),
            pl.BlockSpec((tm, LANES), lambda i, j: (i % tiles_per_seq, 0)),
        ],
        out_specs=[
            pl.BlockSpec((tm, WIDE_TILE), lambda i, j: (i, jnp.minimum(j, N_PLAIN_WIDE))),
            pl.BlockSpec((tm, LANES), lambda i, j: (i, 0)),
            g1_spec, g2_spec,
        ],
        out_shape=[
            jax.ShapeDtypeStruct((t, N_MAIN), BF16),
            jax.ShapeDtypeStruct((t, LANES), F32),
            g1_shape, g2_shape,
        ],
        scratch_shapes=[pltpu.VMEM((tm, D_MODEL), BF16), pltpu.VMEM((WIDE_TILE // LANES, chunk, LANES), F32),
                        pltpu.VMEM((WIDE_TILE // LANES, chunk, LANES), F32)],
        compiler_params=_params(("arbitrary", "arbitrary"), 56),
        name="inproj",
    )(x2d, nw, w_main, w_gate, cos_t, sin_t)


ATTN_QUERY_TILE = 128


def _window_mask(n, qt, sub_len):
    kt = qt + 2 * ATTN_BLOCK
    qpos = n * qt + lax.broadcasted_iota(jnp.int32, (qt, kt), 0)
    kpos = n * qt - ATTN_BLOCK + lax.broadcasted_iota(jnp.int32, (qt, kt), 1)
    return (jnp.abs(kpos - qpos) <= ATTN_BLOCK) & (kpos >= 0) & (kpos < sub_len)


def _attend(q, k, v, valid):
    qt = q.shape[0]
    pairs = range(ATTN_HEADS // 2)
    lane = lax.broadcasted_iota(jnp.int32, (qt, LANES), 1)
    lo = lane < ATTN_HEAD_DIM
    valid2 = jnp.concatenate([valid, valid], axis=0)
    scores = []
    for p in pairs:
        qp = q[:, p * LANES:(p + 1) * LANES]
        zero = jnp.zeros_like(qp)
        stacked = jnp.concatenate([jnp.where(lo, qp, zero), jnp.where(lo, zero, qp)], axis=0)
        scores.append(lax.dot_general(stacked, k[:, p * LANES:(p + 1) * LANES], (((1,), (1,)), ((), ())),
                                      preferred_element_type=F32))
    probs, inv_den, lses = [], [], []
    for p in pairs:
        s = jnp.where(valid2, scores[p], NEG_INF)
        mx = jnp.max(s, axis=-1, keepdims=True)
        e = jnp.exp(s - mx)
        den = jnp.sum(e, axis=-1, keepdims=True)
        probs.append(e.astype(BF16))
        inv_den.append(1.0 / den)
        lses.append(mx + jnp.log(den))
    pvs = [jnp.dot(probs[p], v[:, p * LANES:(p + 1) * LANES], preferred_element_type=F32) for p in pairs]
    lse_out = jnp.zeros((qt, LANES), F32)
    outs = []
    for p in pairs:
        o = pvs[p] * inv_den[p]
        outs.append(jnp.where(lo, o[:qt], o[qt:]))
        lse_out = jnp.where(lane == 2 * p, lses[p][:qt], jnp.where(lane == 2 * p + 1, lses[p][qt:], lse_out))
    return outs, lse_out


def _head_expander():
    row = lax.broadcasted_iota(jnp.int32, (LANES, ATTN_DIM), 0)
    col = lax.broadcasted_iota(jnp.int32, (LANES, ATTN_DIM), 1)
    return jnp.where(col // ATTN_HEAD_DIM == row, 1.0, 0.0).astype(BF16)


def _merge_groups(o_cur, lse_cur, o_prev, lse_prev, expander):
    m = jnp.maximum(lse_prev, lse_cur)
    a = jnp.exp(lse_prev - m)
    tot = a + jnp.exp(lse_cur - m)
    w_prev = a / tot
    hi = w_prev.astype(BF16)
    lo = (w_prev - hi.astype(F32)).astype(BF16)
    spread = (jnp.dot(hi, expander, preferred_element_type=F32) + jnp.dot(lo, expander, preferred_element_type=F32))
    outs = [o_cur[p] + spread[:, p * LANES:(p + 1) * LANES] * (o_prev[p] - o_cur[p])
            for p in range(ATTN_HEADS // 2)]
    lane = lax.broadcasted_iota(jnp.int32, m.shape, 1)
    return outs, jnp.where(lane < ATTN_HEADS, m + jnp.log(tot), 0.0)


def _attn_kernel(q_ref, kp_ref, kc_ref, kn_ref, vp_ref, vc_ref, vn_ref, o_ref, l_ref, *, sub_len, qt):
    n_sub = q_ref.shape[1] // qt
    k = jnp.concatenate([kp_ref[0], kc_ref[0], kn_ref[0]], axis=0)
    v = jnp.concatenate([vp_ref[0], vc_ref[0], vn_ref[0]], axis=0)
    for sub in range(n_sub):
        valid = _window_mask(pl.program_id(1) * n_sub + sub, qt, sub_len)
        rows = slice(sub * qt, (sub + 1) * qt)
        keys = slice(sub * qt, (sub + 1) * qt + 2 * ATTN_BLOCK)
        outs, lse = _attend(q_ref[0, rows, :], k[keys], v[keys], valid)
        for p, o in enumerate(outs):
            o_ref[0, rows, p * LANES:(p + 1) * LANES] = o.astype(BF16)
        l_ref[0, rows, :] = lse


MERGE_ROWS = 256


def _attn_dilated_kernel(q_ref, kp_ref, kc_ref, kn_ref, vp_ref, vc_ref, vn_ref, op_ref, lp_ref, o_ref, l_ref,
                         o_slab, l_slab, *, sub_len, qt, dilation):
    valid = _window_mask(pl.program_id(1), qt, sub_len)

    per_iter = 4 if qt <= ATTN_BLOCK else 2

    def residues(i, c):
        for r in [per_iter * i + u for u in range(per_iter)]:
            k = jnp.concatenate([kp_ref[0, r], kc_ref[0, r], kn_ref[0, r]], axis=0)
            v = jnp.concatenate([vp_ref[0, r], vc_ref[0, r], vn_ref[0, r]], axis=0)
            outs, lse = _attend(q_ref[0, r], k, v, valid)
            for p, o in enumerate(outs):
                o_slab[p, pl.ds(r, qt, stride=dilation), :] = o
            l_slab[pl.ds(r, qt, stride=dilation), :] = lse
        return c

    lax.fori_loop(0, dilation // per_iter, residues, 0)
    expander = _head_expander()

    def merge(i, c):
        rows = pl.ds(pl.multiple_of(i * MERGE_ROWS, MERGE_ROWS), MERGE_ROWS)
        prev = op_ref[rows, :].astype(F32)
        o_cur = [o_slab[p, rows, :] for p in range(ATTN_HEADS // 2)]
        o_prev = [prev[:, p * LANES:(p + 1) * LANES] for p in range(ATTN_HEADS // 2)]
        outs, lse = _merge_groups(o_cur, l_slab[rows, :], o_prev, lp_ref[rows, :], expander)
        for p, o in enumerate(outs):
            o_ref[rows, p * LANES:(p + 1) * LANES] = o.astype(BF16)
        l_ref[rows, :] = lse
        return c

    lax.fori_loop(0, qt * dilation // MERGE_ROWS, merge, 0)


PLAIN_STEP_TILES = 2


def _attn_plain(proj3):
    b, s, _ = proj3.shape
    qt = min(ATTN_QUERY_TILE, s)
    step = min(PLAIN_STEP_TILES * qt, s)
    nblk = s // ATTN_BLOCK
    qb = step // ATTN_BLOCK

    def cur(which):
        return pl.BlockSpec((1, step, COL_TILE), lambda bi, n: (bi, n, ATTN_TILE0 + which))

    def before(which):
        return pl.BlockSpec((1, ATTN_BLOCK, COL_TILE),
                            lambda bi, n: (bi, jnp.maximum(n * qb - 1, 0), ATTN_TILE0 + which))

    def after(which):
        return pl.BlockSpec((1, ATTN_BLOCK, COL_TILE),
                            lambda bi, n: (bi, jnp.minimum((n + 1) * qb, nblk - 1), ATTN_TILE0 + which))

    o, l = pl.pallas_call(
        functools.partial(_attn_kernel, sub_len=s, qt=qt),
        grid=(b, s // step),
        in_specs=[cur(0), before(1), cur(1), after(1), before(2), cur(2), after(2)],
        out_specs=[pl.BlockSpec((1, step, ATTN_DIM), lambda bi, n: (bi, n, 0)),
                   pl.BlockSpec((1, step, LANES), lambda bi, n: (bi, n, 0))],
        out_shape=[jax.ShapeDtypeStruct((b, s, ATTN_DIM), BF16), jax.ShapeDtypeStruct((b, s, LANES), F32)],
        compiler_params=_params(("arbitrary", "arbitrary"), 48),
        name="attn_g0",
    )(*([proj3] * 7))
    return o.reshape(b * s, ATTN_DIM), l.reshape(b * s, LANES)


def _attn_dilated(qkv, g, dilation, prev, b, s):
    n_tiles, _, rows, _ = qkv.shape
    tiles_per_seq = n_tiles // b
    sub_len = s // dilation
    qt = min(ATTN_QUERY_TILE, rows)
    span = qt * dilation
    q_per_tile = rows // qt
    h_per_tile = rows // ATTN_BLOCK
    qb = qt // ATTN_BLOCK
    nblk = sub_len // ATTN_BLOCK

    def cur(which):
        return pl.BlockSpec((1, dilation, qt, COL_TILE),
                            lambda bi, n: (bi * tiles_per_seq + n // q_per_tile, 0, n % q_per_tile, which))

    def halo(which, blk_of):
        def index(bi, n):
            blk = blk_of(n)
            return (bi * tiles_per_seq + blk // h_per_tile, 0, blk % h_per_tile, which)
        return pl.BlockSpec((1, dilation, ATTN_BLOCK, COL_TILE), index)

    before = lambda which: halo(which, lambda n: jnp.maximum(n * qb - 1, 0))
    after = lambda which: halo(which, lambda n: jnp.minimum((n + 1) * qb, nblk - 1))
    steps = sub_len // qt
    o_spec = pl.BlockSpec((span, ATTN_DIM), lambda bi, n: (bi * steps + n, 0))
    l_spec = pl.BlockSpec((span, LANES), lambda bi, n: (bi * steps + n, 0))
    return pl.pallas_call(
        functools.partial(_attn_dilated_kernel, sub_len=sub_len, qt=qt, dilation=dilation),
        grid=(b, steps),
        in_specs=[cur(0), before(1), cur(1), after(1), before(2), cur(2), after(2), o_spec, l_spec],
        out_specs=[o_spec, l_spec],
        out_shape=[jax.ShapeDtypeStruct((b * s, ATTN_DIM), BF16), jax.ShapeDtypeStruct((b * s, LANES), F32)],
        scratch_shapes=[pltpu.VMEM((ATTN_HEADS // 2, span, LANES), F32), pltpu.VMEM((span, LANES), F32)],
        compiler_params=_params(("arbitrary", "arbitrary"), 56),
        name=f"attn_g{g}",
    )(*([qkv] * 7), prev[0], prev[1])


CONV_HALO = 16


Q_TILES = ML_DIM // COL_TILE


def _conv_kernel(xp_ref, xc_ref, xn_ref, w_ref, b_ref, q_ref, kt_ref, buf, *, tm):
    i = pl.program_id(1)
    c = pl.program_id(2)
    last = pl.num_programs(1) - 1
    before = jnp.where(i > 0, xp_ref[0].astype(F32), 0.0)
    centre = xc_ref[0].astype(F32)
    after = jnp.where(i < last, xn_ref[0].astype(F32), 0.0)
    w = w_ref[...]
    bias = b_ref[...]
    pieces = []
    for s in range(COL_TILE // LANES):
        lanes = slice(s * LANES, (s + 1) * LANES)
        buf[s, 0:CONV_HALO, :] = before[:, lanes]
        buf[s, CONV_HALO:CONV_HALO + tm, :] = centre[:, lanes]
        buf[s, CONV_HALO + tm:, :] = after[:, lanes]
        acc = jnp.broadcast_to(bias[:, lanes], (tm, LANES))
        for tap in range(CONV_W):
            acc = acc + w[tap:tap + 1, lanes] * buf[s, pl.ds(CONV_HALO - CONV_W // 2 + tap, tm, stride=1), :]
        pieces.append(acc)
    y = jnp.concatenate(pieces, axis=1)
    y = y * _sigmoid(y)

    @pl.when(c < Q_TILES)
    def _():
        q_ref[0] = y.astype(BF16)

    @pl.when(c >= Q_TILES)
    def _():
        kt_ref[0] = (y * ML_HEAD_DIM ** -0.5).T.astype(BF16)


def _conv_qk(proj3, conv_w, conv_b):
    b, s, _ = proj3.shape
    tm = min(1024, s)
    hb = tm // CONV_HALO
    nh = s // CONV_HALO
    return pl.pallas_call(
        functools.partial(_conv_kernel, tm=tm),
        grid=(b, s // tm, 2 * Q_TILES),
        in_specs=[
            pl.BlockSpec((1, CONV_HALO, COL_TILE), lambda bi, i, c: (bi, jnp.maximum(i * hb - 1, 0), c)),
            pl.BlockSpec((1, tm, COL_TILE), lambda bi, i, c: (bi, i, c)),
            pl.BlockSpec((1, CONV_HALO, COL_TILE), lambda bi, i, c: (bi, jnp.minimum((i + 1) * hb, nh - 1), c)),
            pl.BlockSpec((CONV_W, COL_TILE), lambda bi, i, c: (0, c)),
            pl.BlockSpec((1, COL_TILE), lambda bi, i, c: (0, c)),
        ],
        out_specs=[
            pl.BlockSpec((1, tm, COL_TILE), lambda bi, i, c: (bi, i, jnp.minimum(c, Q_TILES - 1))),
            pl.BlockSpec((1, COL_TILE, tm), lambda bi, i, c: (bi, jnp.maximum(c - Q_TILES, 0), i)),
        ],
        out_shape=[jax.ShapeDtypeStruct((b, s, ML_DIM), BF16), jax.ShapeDtypeStruct((b, ML_DIM, s), BF16)],
        scratch_shapes=[pltpu.VMEM((COL_TILE // LANES, tm + 2 * CONV_HALO, LANES), F32)],
        compiler_params=_params(("arbitrary", "arbitrary", "arbitrary"), 32),
        name="conv_qk",
    )(proj3, proj3, proj3, conv_w, conv_b)


def _gate_kernel(g_ref, b_ref, o_ref):
    g = g_ref[0] + b_ref[...]
    width = g.shape[1]
    logsig = jnp.minimum(g, 0.0) - jnp.log(1.0 + jnp.exp(-jnp.abs(g)))
    lane = lax.broadcasted_iota(jnp.int32, g.shape, 1) % ML_CHUNK
    pre = logsig
    suf = logsig
    step = 1
    while step < ML_CHUNK:
        pre = pre + jnp.where(lane >= step, pltpu.roll(pre, step, 1), 0.0)
        suf = suf + jnp.where(lane < ML_CHUNK - step, pltpu.roll(suf, width - step, 1), 0.0)
        step *= 2
    row = lax.broadcasted_iota(jnp.int32, g.shape, 0)
    is_f_fwd = (row >= ML_HEADS) & (row < 2 * ML_HEADS)
    is_f_bwd = row >= 3 * ML_HEADS
    o_ref[0] = jnp.where(is_f_fwd, pre, jnp.where(is_f_bwd, suf, g))


def _gate_prep(gates_t, gate_b):
    b, rows, s = gates_t.shape
    sb = min(2048, s)
    return pl.pallas_call(
        _gate_kernel,
        grid=(b, s // sb),
        in_specs=[
            pl.BlockSpec((1, rows, sb), lambda bi, i: (bi, 0, i)),
            pl.BlockSpec((rows, 1), lambda bi, i: (0, 0)),
        ],
        out_specs=pl.BlockSpec((1, rows, sb), lambda bi, i: (bi, 0, i)),
        out_shape=jax.ShapeDtypeStruct((b, rows, s), F32),
        compiler_params=_params(("arbitrary", "arbitrary"), 32),
        name="gate_prep",
    )(gates_t, gate_b)


ML_CHAIN_GROUP = 8


def _mlstm_kernel(qf_ref, kf_ref, vf_ref, qb_ref, kb_ref, vb_ref, grf_ref, grb_ref, gcf_ref, gcb_ref,
                  hf_ref, hb_ref, s_scr, m_scr):
    c = pl.program_id(1)

    @pl.when(c == 0)
    def _():
        def clear(i, carry):
            s_scr[i] = jnp.zeros(s_scr.shape[1:], F32)
            return carry

        lax.fori_loop(0, s_scr.shape[0], clear, 0)
        m_scr[...] = jnp.zeros_like(m_scr)

    t_i = lax.broadcasted_iota(jnp.int32, (ML_CHUNK, ML_CHUNK), 0)
    s_i = lax.broadcasted_iota(jnp.int32, (ML_CHUNK, ML_CHUNK), 1)
    ones_col = jnp.where(lax.broadcasted_iota(jnp.int32, (ML_CHUNK, LANES), 1) == 0, 1.0, 0.0).astype(BF16)
    dirs = ((qf_ref, kf_ref, vf_ref, grf_ref, gcf_ref, hf_ref), (qb_ref, kb_ref, vb_ref, grb_ref, gcb_ref, hb_ref))
    all_chains = [(dirn, head) + refs for dirn, refs in enumerate(dirs) for head in range(ML_HEADS)]

    for first in range(0, len(all_chains), ML_CHAIN_GROUP):
        chains = all_chains[first:first + ML_CHAIN_GROUP]
        matmuls = []
        for dirn, head, q_ref, k_ref, v_ref, _, _, _ in chains:
            hs = slice(head * ML_HEAD_DIM, (head + 1) * ML_HEAD_DIM)
            q = q_ref[0, :, hs]
            qk_raw = jnp.dot(q, k_ref[0, hs, :], preferred_element_type=F32)
            q_state = jnp.dot(q, s_scr[dirn * ML_HEADS + head].astype(BF16), preferred_element_type=F32)
            matmuls.append((qk_raw, q_state))

        weights = []
        for dirn, head, _, _, _, gr_ref, gc_ref, _ in chains:
            ii = dirn * 2 * ML_HEADS + head
            bi = ii + ML_HEADS
            mask = (s_i <= t_i) if dirn == 0 else (s_i >= t_i)
            i_row, b_row = gr_ref[0, ii:ii + 1, :], gr_ref[0, bi:bi + 1, :]
            b_col = gc_ref[0, :, bi:bi + 1]
            m = m_scr[dirn * ML_HEADS + head][0:1, 0:1]
            dmat = jnp.where(mask, b_col - b_row + i_row, NEG_INF)
            inter = b_col + m
            m_t = jnp.maximum(inter, jnp.max(dmat, axis=-1, keepdims=True))
            weights.append((jnp.exp(dmat - m_t), jnp.exp(inter - m_t), jnp.exp(-m_t)))

        for (dirn, head, _, _, v_ref, _, _, h_ref), (qk_raw, q_state), (w_intra, w_inter, floor) in zip(
                chains, matmuls, weights):
            hs = slice(head * ML_HEAD_DIM, (head + 1) * ML_HEAD_DIM)
            v_ext = jnp.concatenate([v_ref[0, :, hs], ones_col], axis=1)
            num = w_inter * q_state + jnp.dot((qk_raw * w_intra).astype(BF16), v_ext, preferred_element_type=F32)
            den = jnp.maximum(jnp.abs(num[:, ML_HEAD_DIM:ML_HEAD_DIM + 1]), floor)
            h_ref[0, :, hs] = (num[:, :ML_HEAD_DIM] / den).astype(BF16)

        for dirn, head, _, k_ref, v_ref, gr_ref, _, _ in chains:
            idx = dirn * ML_HEADS + head
            ii = dirn * 2 * ML_HEADS + head
            bi = ii + ML_HEADS
            hs = slice(head * ML_HEAD_DIM, (head + 1) * ML_HEAD_DIM)
            i_row, b_row = gr_ref[0, ii:ii + 1, :], gr_ref[0, bi:bi + 1, :]
            b_last = b_row[:, ML_CHUNK - 1:ML_CHUNK] if dirn == 0 else b_row[:, 0:1]
            m = m_scr[idx][0:1, 0:1]
            log_w = b_last - b_row + i_row
            m_new = jnp.maximum(b_last + m, jnp.max(log_w, axis=-1, keepdims=True))
            decay = jnp.exp(b_last + m - m_new)
            v_ext = jnp.concatenate([v_ref[0, :, hs], ones_col], axis=1)
            wk_t = (k_ref[0, hs, :].astype(F32) * jnp.exp(log_w - m_new)).astype(BF16)
            s_scr[idx] = decay * s_scr[idx] + jnp.dot(wk_t, v_ext, preferred_element_type=F32)
            m_scr[idx] = jnp.broadcast_to(m_new, m_scr.shape[1:])


def _mlstm(proj3, q, k_t, g_rows, g_cols):
    b, s, _ = proj3.shape
    nc = s // ML_CHUNK
    v_tile = 2 * ML_DIM // ML_DIM
    n_rows = g_rows.shape[1]

    def rows(col, reverse):
        return pl.BlockSpec((1, ML_CHUNK, ML_DIM), lambda bi, c: (bi, nc - 1 - c if reverse else c, col))

    def cols(height, reverse):
        return pl.BlockSpec((1, height, ML_CHUNK), lambda bi, c: (bi, 0, nc - 1 - c if reverse else c))

    return pl.pallas_call(
        _mlstm_kernel,
        grid=(b, nc),
        in_specs=[
            rows(0, False), cols(ML_DIM, False), rows(v_tile, False),
            rows(0, True), cols(ML_DIM, True), rows(v_tile, True),
            cols(n_rows, False), cols(n_rows, True),
            pl.BlockSpec((1, ML_CHUNK, n_rows), lambda bi, c: (bi, c, 0)),
            pl.BlockSpec((1, ML_CHUNK, n_rows), lambda bi, c: (bi, nc - 1 - c, 0)),
        ],
        out_specs=[rows(0, False), rows(0, True)],
        out_shape=[jax.ShapeDtypeStruct((b, s, ML_DIM), BF16)] * 2,
        scratch_shapes=[
            pltpu.VMEM((2 * ML_HEADS, ML_HEAD_DIM, ML_EXT), F32),
            pltpu.VMEM((2 * ML_HEADS, 8, LANES), F32),
        ],
        compiler_params=_params(("arbitrary", "arbitrary"), 32),
        name="mlstm",
    )(q, k_t, proj3, q, k_t, proj3, g_rows, g_rows, g_cols, g_cols)


def _pack_bf16_pair(lo, hi):
    lo_bits = lax.bitcast_convert_type(lo.astype(BF16).astype(F32), jnp.uint32)
    hi_bits = lax.bitcast_convert_type(hi.astype(BF16).astype(F32), jnp.uint32)
    return (hi_bits & jnp.uint32(0xFFFF0000)) | (lo_bits >> 16)


def _unpack_bf16_pair(packed):
    lo = lax.bitcast_convert_type(packed << 16, F32)
    hi = lax.bitcast_convert_type(packed & jnp.uint32(0xFFFF0000), F32)
    return lo, hi


MERGE_CHUNK = 512


def _merge_kernel(x_ref, ao_ref, hf_ref, hb_ref, mo_ref, mg_ref, wpa_ref, wpm_ref, wo_ref, nfw_ref,
                  rw_ref, rb_ref, x1_ref, h2_ref, idx_ref, wt_ref):
    tm = x_ref.shape[0]
    chunk = min(MERGE_CHUNK, tm)
    for c in range(tm // chunk):
        rows = slice(c * chunk, (c + 1) * chunk)
        y_attn = jnp.dot(ao_ref[rows, :], wpa_ref[...], preferred_element_type=F32)
        hsum = hf_ref[rows, :].astype(F32) + hb_ref[rows, :].astype(F32)
        ml = (_sigmoid(mo_ref[rows, :].astype(F32)) * hsum).astype(BF16)
        y_ml = jnp.dot(ml, wpm_ref[...], preferred_element_type=F32)
        gates = _sigmoid(mg_ref[rows, :].astype(F32))
        mixed = (gates[:, :D_MODEL] * y_attn + gates[:, D_MODEL:] * y_ml).astype(BF16)
        x1 = x_ref[rows, :] + jnp.dot(mixed, wo_ref[...], preferred_element_type=F32)
        x1_ref[rows, :] = x1
        ms = jnp.mean(x1 * x1, axis=-1, keepdims=True)
        h2 = x1 * lax.rsqrt(ms + RMS_EPS) * nfw_ref[...]
        h2_ref[rows, :] = _pack_bf16_pair(h2[:, :D_MODEL // 2], h2[:, D_MODEL // 2:])

        logits = lax.dot_general(rw_ref[...], h2, (((1,), (1,)), ((), ())), preferred_element_type=F32,
                                 precision=lax.Precision.HIGHEST) + rb_ref[...]
        row = lax.broadcasted_iota(jnp.int32, logits.shape, 0)
        rest = logits
        vals = []
        for k in range(TOP_K):
            mx = jnp.max(rest, axis=0, keepdims=True)
            first = jnp.min(jnp.where(rest == mx, row, N_EXPERTS), axis=0, keepdims=True)
            vals.append(mx)
            idx_ref[k:k + 1, rows] = first
            rest = jnp.where(row == first, -jnp.inf, rest)
        exps = [jnp.exp(v - vals[0]) for v in vals]
        tot = exps[0] + exps[1] + exps[2] + exps[3]
        for k in range(TOP_K):
            wt_ref[k:k + 1, rows] = exps[k] / tot


def _merge(x2d, attn_o, h_f, h_b, proj, wpa, wpm, wo, nfw, rw_t, rb):
    t = x2d.shape[0]
    tm = min(512, t)
    ml_o_tile = ML_QKV_COLS // ML_O_COLS
    merge_tile = (ML_QKV_COLS + ML_O_COLS) // MERGE_COLS
    row = lambda width: pl.BlockSpec((tm, width), lambda i: (i, 0))
    full = lambda a: pl.BlockSpec(a.shape, lambda i: (0, 0))
    return pl.pallas_call(
        _merge_kernel,
        grid=(t // tm,),
        in_specs=[
            row(D_MODEL), row(ATTN_DIM), row(ML_DIM), row(ML_DIM),
            pl.BlockSpec((tm, ML_O_COLS), lambda i: (i, ml_o_tile)),
            pl.BlockSpec((tm, MERGE_COLS), lambda i: (i, merge_tile)),
            full(wpa), full(wpm), full(wo), full(nfw), full(rw_t), full(rb),
        ],
        out_specs=[row(D_MODEL), row(D_MODEL // 2), pl.BlockSpec((TOP_K, tm), lambda i: (0, i)),
                   pl.BlockSpec((TOP_K, tm), lambda i: (0, i))],
        out_shape=[
            jax.ShapeDtypeStruct((t, D_MODEL), F32),
            jax.ShapeDtypeStruct((t, D_MODEL // 2), jnp.uint32),
            jax.ShapeDtypeStruct((TOP_K, t), jnp.int32),
            jax.ShapeDtypeStruct((TOP_K, t), F32),
        ],
        compiler_params=_params(("arbitrary",), 48),
        name="merge",
    )(x2d, attn_o, h_f, h_b, proj, proj, wpa, wpm, wo, nfw, rw_t, rb)


ROUTE_TILE = 512


def _route_kernel(idx_ref, pos_ref, cnt_ref, base):
    @pl.when(pl.program_id(0) == 0)
    def _():
        base[...] = jnp.zeros_like(base)

    idx = idx_ref[...]
    row = lax.broadcasted_iota(jnp.int32, (N_EXPERTS, ROUTE_TILE), 0)
    onehot = jnp.zeros((N_EXPERTS, ROUTE_TILE), F32)
    for k in range(TOP_K):
        onehot = onehot + jnp.where(row == idx[k:k + 1, :], 1.0, 0.0)
    s_i = lax.broadcasted_iota(jnp.int32, (ROUTE_TILE, ROUTE_TILE), 0)
    t_i = lax.broadcasted_iota(jnp.int32, (ROUTE_TILE, ROUTE_TILE), 1)
    upper = jnp.where(s_i <= t_i, 1.0, 0.0).astype(BF16)
    incl = jnp.dot(onehot.astype(BF16), upper, preferred_element_type=F32)
    before = base[:, 0:1]
    count = incl + before
    for k in range(TOP_K):
        mine = jnp.sum(jnp.where(row == idx[k:k + 1, :], count, 0.0), axis=0, keepdims=True)
        pos_ref[k:k + 1, :] = (mine - 1.0).astype(jnp.int32)
    total = before + incl[:, ROUTE_TILE - 1:ROUTE_TILE]
    base[...] = jnp.broadcast_to(total, base.shape)
    cnt_ref[...] = jnp.broadcast_to(total, cnt_ref.shape)


def _route(idx):
    t = idx.shape[1]
    return pl.pallas_call(
        _route_kernel,
        grid=(t // ROUTE_TILE,),
        in_specs=[pl.BlockSpec((TOP_K, ROUTE_TILE), lambda i: (0, i))],
        out_specs=[pl.BlockSpec((TOP_K, ROUTE_TILE), lambda i: (0, i)),
                   pl.BlockSpec((N_EXPERTS, LANES), lambda i: (0, 0))],
        out_shape=[jax.ShapeDtypeStruct((TOP_K, t), jnp.int32),
                   jax.ShapeDtypeStruct((N_EXPERTS, LANES), F32)],
        scratch_shapes=[pltpu.VMEM((N_EXPERTS, LANES), F32)],
        compiler_params=_params(("arbitrary",), 32),
        name="route",
    )(idx)


GROUP_TILE = 512
DISPATCH_TILE = 1024
ROW_WORDS = D_MODEL // 2


def _dispatch_kernel(slot_hbm, ztile_ref, nu_ref, *refs, steps):
    srcs = refs[:len(steps)]
    dst_hbm, slot_smem, zbuf, sem = refs[len(steps):]
    i = pl.program_id(0)
    n_slots = DISPATCH_TILE * TOP_K
    n_tiles = dst_hbm.shape[0] // GROUP_TILE
    load = pltpu.make_async_copy(slot_hbm.at[pl.ds(i * n_slots, n_slots)], slot_smem, sem.at[0])
    load.start()

    @pl.when(i == 0)
    def _():
        zbuf[...] = jnp.zeros_like(zbuf)

        def fill_tile(tile):
            pltpu.make_async_copy(zbuf, dst_hbm.at[pl.ds(tile * GROUP_TILE, GROUP_TILE), :], sem.at[1]).start()

        def fill_last(e, n):
            tile = ztile_ref[e]

            @pl.when(tile >= 0)
            def _():
                fill_tile(tile)

            return n + jnp.where(tile >= 0, 1, 0)

        def fill_unused(tile, c):
            fill_tile(tile)
            return c

        n_fill = lax.fori_loop(0, N_EXPERTS, fill_last, 0)
        lax.fori_loop(nu_ref[0], n_tiles, fill_unused, 0)

        def drain(_, c):
            pltpu.make_async_copy(zbuf, dst_hbm.at[pl.ds(0, GROUP_TILE), :], sem.at[1]).wait()
            return c

        lax.fori_loop(0, n_fill + n_tiles - nu_ref[0], drain, 0)

    load.wait()

    first = 0
    for src_ref, n_steps in zip(srcs, steps):
        @pl.when((i >= first) & (i < first + n_steps))
        def _(src_ref=src_ref):
            def issue(t, c):
                src = src_ref.at[pl.ds(t, 1), :]
                for k in range(TOP_K):
                    slot = slot_smem[t * TOP_K + k]
                    pltpu.make_async_copy(src, dst_hbm.at[pl.ds(slot, 1), :], sem.at[1]).start(priority=k % 2)
                return c

            lax.fori_loop(0, DISPATCH_TILE, issue, 0)

        first += n_steps
    pltpu.make_async_copy(dst_hbm.at[pl.ds(0, n_slots), :], dst_hbm.at[pl.ds(0, n_slots), :], sem.at[1]).wait()


def _dispatch(slots, ztile, n_used, sources, n_rows):
    steps = tuple(h.shape[0] // DISPATCH_TILE for h in sources)
    any_spec = pl.BlockSpec(memory_space=pl.ANY)
    smem_spec = pl.BlockSpec(memory_space=pltpu.SMEM)
    src_specs = []
    first = 0
    for n_steps in steps:
        src_specs.append(pl.BlockSpec((DISPATCH_TILE, ROW_WORDS),
                                      lambda i, first=first, n_steps=n_steps: (jnp.clip(i - first, 0, n_steps - 1), 0)))
        first += n_steps
    return pl.pallas_call(
        functools.partial(_dispatch_kernel, steps=steps),
        grid=(sum(steps),),
        in_specs=[any_spec, smem_spec, smem_spec] + src_specs,
        out_specs=any_spec,
        out_shape=jax.ShapeDtypeStruct((n_rows, ROW_WORDS), jnp.uint32),
        scratch_shapes=[
            pltpu.SMEM((DISPATCH_TILE * TOP_K,), jnp.int32),
            pltpu.VMEM((GROUP_TILE, ROW_WORDS), jnp.uint32),
            pltpu.SemaphoreType.DMA((2,)),
        ],
        compiler_params=_params(("arbitrary",), 32),
        name="dispatch",
    )(slots, ztile, n_used, *sources)


EXPERT_CHUNK = 256


def _expert_kernel(te_ref, ts_ref, nu_ref, x_ref, wgu_ref, bgu_ref, wd_ref, bd_ref, y_ref):
    j = pl.program_id(0)

    @pl.when(j < nu_ref[0])
    def _():
        half = D_MODEL // 2
        for c in range(GROUP_TILE // EXPERT_CHUNK):
            rows = slice(c * EXPERT_CHUNK, (c + 1) * EXPERT_CHUNK)
            lo, hi = _unpack_bf16_pair(x_ref[rows, :])
            gu = (jnp.dot(lo.astype(BF16), wgu_ref[0, :half, :], preferred_element_type=F32)
                  + jnp.dot(hi.astype(BF16), wgu_ref[0, half:, :], preferred_element_type=F32) + bgu_ref[0])
            gate = jnp.minimum(gu[:, :D_FF], SWIGLU_LIMIT)
            up = jnp.clip(gu[:, D_FF:], -SWIGLU_LIMIT, SWIGLU_LIMIT)
            hid = (up + 1.0) * gate * _sigmoid(SWIGLU_ALPHA * gate)
            y = jnp.dot(hid.astype(BF16), wd_ref[0], preferred_element_type=F32) + bd_ref[0]
            y_ref[rows, :] = _pack_bf16_pair(y[:, :half], y[:, half:])

    @pl.when(j >= nu_ref[0])
    def _():
        y_ref[...] = jnp.zeros_like(y_ref)


def _experts(tile_expert, tile_src, n_used, xs, wgu, bgu, wd, bd):
    n_rows = xs.shape[0]
    row_spec = pl.BlockSpec((GROUP_TILE, ROW_WORDS), lambda j, te, ts, nu: (ts[j], 0))
    out_spec = pl.BlockSpec((GROUP_TILE, ROW_WORDS), lambda j, te, ts, nu: (j, 0))
    return pl.pallas_call(
        _expert_kernel,
        grid_spec=pltpu.PrefetchScalarGridSpec(
            num_scalar_prefetch=3,
            grid=(n_rows // GROUP_TILE,),
            in_specs=[
                row_spec,
                pl.BlockSpec((1, D_MODEL, 2 * D_FF), lambda j, te, ts, nu: (te[j], 0, 0)),
                pl.BlockSpec((1, 1, 2 * D_FF), lambda j, te, ts, nu: (te[j], 0, 0)),
                pl.BlockSpec((1, D_FF, D_MODEL), lambda j, te, ts, nu: (te[j], 0, 0)),
                pl.BlockSpec((1, 1, D_MODEL), lambda j, te, ts, nu: (te[j], 0, 0)),
            ],
            out_specs=out_spec,
        ),
        out_shape=jax.ShapeDtypeStruct((n_rows, ROW_WORDS), jnp.uint32),
        compiler_params=_params(("arbitrary",), 56),
        name="experts",
    )(tile_expert, tile_src, n_used, xs, wgu, bgu, wd, bd)


COMBINE_TILE = 256


def _combine_kernel(slot_hbm, ys_hbm, x1_ref, wt_ref, nw_ref, o_ref, slot_a, slot_b, buf_a, buf_b, sem):
    i = pl.program_id(0)
    n = pl.num_programs(0)
    n_slots = COMBINE_TILE * TOP_K
    slot_bufs = (slot_a, slot_b)
    row_bufs = (buf_a, buf_b)

    def slot_load(tile, par):
        return pltpu.make_async_copy(slot_hbm.at[pl.ds(tile * n_slots, n_slots)], slot_bufs[par], sem.at[par])

    def issue_rows(par):
        def issue(t, c):
            for k in range(TOP_K):
                slot = slot_bufs[par][t * TOP_K + k]
                pltpu.make_async_copy(ys_hbm.at[pl.ds(slot, 1), :], row_bufs[par].at[k, pl.ds(t, 1), :],
                                      sem.at[2 + par]).start(priority=k % 2)
            return c

        lax.fori_loop(0, COMBINE_TILE, issue, 0)

    def reduce_rows(par):
        pltpu.make_async_copy(ys_hbm.at[pl.ds(0, n_slots), :], ys_hbm.at[pl.ds(0, n_slots), :],
                              sem.at[2 + par]).wait()
        half = D_MODEL // 2
        wt = wt_ref[...]
        acc_lo = x1_ref[:, :half]
        acc_hi = x1_ref[:, half:]
        for k in range(TOP_K):
            lo, hi = _unpack_bf16_pair(row_bufs[par][k])
            acc_lo = acc_lo + wt[:, k:k + 1] * lo
            acc_hi = acc_hi + wt[:, k:k + 1] * hi
        ms = (jnp.sum(acc_lo * acc_lo, axis=-1, keepdims=True)
              + jnp.sum(acc_hi * acc_hi, axis=-1, keepdims=True)) * (1.0 / D_MODEL)
        inv = lax.rsqrt(ms + RMS_EPS)
        o_ref[:, :half] = acc_lo * inv * nw_ref[:, :half]
        o_ref[:, half:] = acc_hi * inv * nw_ref[:, half:]

    @pl.when(i == 0)
    def _():
        first = slot_load(0, 0)
        first.start()
        first.wait()
        issue_rows(0)

        @pl.when(n > 1)
        def _():
            slot_load(1, 1).start()

    for par in range(2):
        @pl.when(i % 2 == par)
        def _(par=par):
            @pl.when(i + 1 < n)
            def _():
                slot_load(i + 1, 1 - par).wait()
                issue_rows(1 - par)

            @pl.when(i + 2 < n)
            def _():
                slot_load(i + 2, par).start()

            reduce_rows(par)


def _combine(slots, ys, x1, wt, nw):
    t = x1.shape[0]
    tm = COMBINE_TILE
    return pl.pallas_call(
        _combine_kernel,
        grid=(t // tm,),
        in_specs=[
            pl.BlockSpec(memory_space=pl.ANY),
            pl.BlockSpec(memory_space=pl.ANY),
            pl.BlockSpec((tm, D_MODEL), lambda i: (i, 0)),
            pl.BlockSpec((tm, TOP_K), lambda i: (i, 0)),
            pl.BlockSpec((1, D_MODEL), lambda i: (0, 0)),
        ],
        out_specs=pl.BlockSpec((tm, D_MODEL), lambda i: (i, 0)),
        out_shape=jax.ShapeDtypeStruct((t, D_MODEL), F32),
        scratch_shapes=[
            pltpu.SMEM((tm * TOP_K,), jnp.int32),
            pltpu.SMEM((tm * TOP_K,), jnp.int32),
            pltpu.VMEM((TOP_K, tm, ROW_WORDS), jnp.uint32),
            pltpu.VMEM((TOP_K, tm, ROW_WORDS), jnp.uint32),
            pltpu.SemaphoreType.DMA((4,)),
        ],
        compiler_params=_params(("arbitrary",), 32),
        name="combine",
    )(slots, ys, x1, wt, nw)


def _moe(parts, p, final_w):
    sizes = [x1.shape[0] for x1, _, _, _ in parts]
    idx = jnp.concatenate([q[2] for q in parts], axis=1)
    t_all = idx.shape[1]
    pos, cnt = _route(idx)
    counts = cnt[:, 0].astype(jnp.int32)
    tiles_e = (counts + GROUP_TILE - 1) // GROUP_TILE
    tile_end = jnp.cumsum(tiles_e)
    tile_start = tile_end - tiles_e
    n_used = tile_end[-1]
    n_tiles = (t_all * TOP_K) // GROUP_TILE + N_EXPERTS
    n_rows = n_tiles * GROUP_TILE
    first_row = tile_start * GROUP_TILE
    experts = jnp.arange(N_EXPERTS, dtype=jnp.int32)[:, None, None]
    base = jnp.sum(jnp.where(idx[None] == experts, first_row[:, None, None], 0), axis=0)
    slots = (base + pos).T.reshape(-1)
    tile_id = jnp.arange(n_tiles, dtype=jnp.int32)
    tile_src = jnp.minimum(tile_id, n_used - 1).astype(jnp.int32)
    tile_expert = jnp.sum(tile_src[:, None] >= tile_end[None, :], axis=1).astype(jnp.int32)
    tile_expert = jnp.minimum(tile_expert, N_EXPERTS - 1)
    ztile = jnp.where(counts > 0, tile_end - 1, -1).astype(jnp.int32)

    n_used = n_used.reshape(1).astype(jnp.int32)
    xs = _dispatch(slots, ztile, n_used, [q[1] for q in parts], n_rows)
    ys = _experts(tile_expert, tile_src, n_used, xs, p["wgu"], p["bgu"], p["wd"], p["bd"])
    outs = []
    off = 0
    for (x1, _, _, wt), t in zip(parts, sizes):
        outs.append(_combine(lax.slice(slots, (off * TOP_K,), ((off + t) * TOP_K,)), ys, x1, wt.T, final_w))
        off += t
    return outs


def _rope_tables(seq):
    half = ATTN_HEAD_DIM // 2
    inv_freq = ROPE_THETA ** (-jnp.arange(half, dtype=F32) / half)
    ang = jnp.arange(seq, dtype=F32)[:, None] * inv_freq[None, :]
    cos, sin = jnp.cos(ang), jnp.sin(ang)
    reps = LANES // ATTN_HEAD_DIM
    cos_t = jnp.tile(jnp.concatenate([cos, cos], axis=1), (1, reps))
    sin_t = jnp.tile(jnp.concatenate([-sin, sin], axis=1), (1, reps))
    return cos_t, sin_t


def _pack_layer(w_in, conv_w, conv_b, gate_b, wpa, wpm, wo, nmw, nfw, rw, rb, wgu, bgu, wd, bd):
    c0 = ATTN_QKV_COLS
    c1 = c0 + ML_QKV_COLS
    c2 = c1 + ML_O_COLS
    c3 = c2 + ML_GATE_COLS
    w_attn = w_in[:, :c0].reshape(D_MODEL, N_GROUPS, 3, ATTN_DIM)
    w_attn = w_attn * jnp.array([ATTN_HEAD_DIM ** -0.5, 1.0, 1.0], F32)[None, None, :, None]
    w_main = jnp.concatenate([w_in[:, c0:c2], w_in[:, c3:], w_attn.reshape(D_MODEL, c0)], axis=1).astype(BF16)
    w_gate = jnp.pad(w_in[:, c2:c3], ((0, 0), (0, LANES - ML_GATE_COLS))).astype(BF16)
    return dict(
        w_main=w_main, w_gate=w_gate, conv_w=conv_w, conv_b=conv_b.reshape(1, -1),
        gate_b=gate_b.reshape(-1, 1), wpa=wpa.astype(BF16), wpm=wpm.astype(BF16), wo=wo.astype(BF16),
        nmw=nmw.reshape(1, -1), nfw=nfw.reshape(1, -1), rw_t=rw.T, rb=rb.reshape(-1, 1),
        wgu=wgu.astype(BF16), bgu=bgu.reshape(N_EXPERTS, 1, -1), wd=wd.astype(BF16),
        bd=bd.reshape(N_EXPERTS, 1, -1))


def _layer(x, p, tables):
    b, s, _ = x.shape
    x2d = x.reshape(b * s, D_MODEL)
    proj, gates, qkv1, qkv2 = _inproj(x2d, p["nmw"], p["w_main"], p["w_gate"], tables[0], tables[1], s)
    proj3 = proj.reshape(b, s, N_MAIN)

    merged = _attn_plain(proj3)
    for g, qkv in ((1, qkv1), (2, qkv2)):
        merged = _attn_dilated(qkv, g, ATTN_PATTERNS[g][1], merged, b, s)
    attn_o = merged[0]

    ml_q, ml_kt = _conv_qk(proj3, p["conv_w"], p["conv_b"])
    gates_t = gates[:, :ML_GATE_COLS].reshape(b, s, ML_GATE_COLS).transpose(0, 2, 1)
    g_rows = _gate_prep(gates_t, p["gate_b"])
    g_cols = g_rows.transpose(0, 2, 1)
    h_f, h_b = _mlstm(proj3, ml_q, ml_kt, g_rows, g_cols)

    return _merge(x2d, attn_o, h_f.reshape(b * s, ML_DIM), h_b.reshape(b * s, ML_DIM), proj,
                  p["wpa"], p["wpm"], p["wo"], p["nfw"], p["rw_t"], p["rb"])


def kernel(x_prompt, x_sample, norm_mix_w, w_in, mlstm_conv_w, mlstm_conv_b, mlstm_gate_b, w_proj_attn,
           w_proj_mlstm, w_out, norm_ffn_w, router_w, router_b, expert_w_gu, expert_b_gu, expert_w_down,
           expert_b_down, norm_final_w):
    depth = w_in.shape[0]
    assert depth == 1, "the final RMSNorm is fused into the last layer's MoE kernel"
    p = _pack_layer(w_in[0], mlstm_conv_w[0], mlstm_conv_b[0], mlstm_gate_b[0], w_proj_attn[0],
                    w_proj_mlstm[0], w_out[0], norm_mix_w[0], norm_ffn_w[0], router_w[0], router_b[0],
                    expert_w_gu[0], expert_b_gu[0], expert_w_down[0], expert_b_down[0])
    final_w = norm_final_w.reshape(1, -1)
    xs = (x_prompt, x_sample)
    parts = [_layer(x, p, _rope_tables(x.shape[1])) for x in xs]
    outs = _moe(parts, p, final_w)
    return tuple(o.reshape(x.shape) for o, x in zip(outs, xs))
```

```python
import functools

import jax
import jax.numpy as jnp
from jax import lax
from jax.experimental import pallas as pl
from jax.experimental.pallas import tpu as pltpu

F32 = jnp.float32
BF16 = jnp.bfloat16

D_MODEL = 1024
ATTN_PATTERNS = ((128, 1), (512, 4), (2048, 16))
N_GROUPS = 3
ATTN_HEADS = 8
ATTN_HEAD_DIM = 64
ATTN_DIM = ATTN_HEADS * ATTN_HEAD_DIM
ATTN_BLOCK = 64
ROPE_THETA = 10000.0
ML_DIM = D_MODEL
ML_HEADS = 4
ML_HEAD_DIM = ML_DIM // ML_HEADS
ML_CHUNK = 256
CONV_W = 5
ATTN_QKV_COLS = N_GROUPS * 3 * ATTN_DIM
ML_QKV_COLS = 3 * ML_DIM
ML_O_COLS = ML_DIM
ML_GATE_COLS = 4 * ML_HEADS
MERGE_COLS = 2 * D_MODEL
N_EXPERTS = 32
TOP_K = 4
D_FF = D_MODEL
SWIGLU_LIMIT = 7.0
SWIGLU_ALPHA = 1.702
RMS_EPS = 1e-5
NEG_INF = -1e30

LANES = 128
SUBLANES = 8
COL_TILE = 512
N_PACKED = ML_QKV_COLS + ML_O_COLS + MERGE_COLS + ATTN_QKV_COLS
N_COL_TILES = N_PACKED // COL_TILE
ATTN_TILE0 = (ML_QKV_COLS + ML_O_COLS + MERGE_COLS) // COL_TILE
GROUP_TILES = 3 * ATTN_DIM // COL_TILE
N_MAIN_TILES = ATTN_TILE0 + GROUP_TILES
N_MAIN = N_MAIN_TILES * COL_TILE
ML_EXT = ML_HEAD_DIM + LANES


def _params(sem, vmem_mb):
    return pltpu.CompilerParams(dimension_semantics=sem, vmem_limit_bytes=vmem_mb * 1024 * 1024)


def _sigmoid(x):
    return 1.0 / (1.0 + jnp.exp(-x))


INPROJ_CHUNK = 512
WIDE_TILE = 3 * ATTN_DIM
N_WIDE_TILES = N_PACKED // WIDE_TILE
N_PLAIN_WIDE = ATTN_TILE0 * COL_TILE // WIDE_TILE


def _inproj_kernel(x_ref, nw_ref, w_ref, wg_ref, cos_ref, sin_ref, out_ref, gates_ref, g1_ref, g2_ref,
                   h_scr, slab, slab2):
    j = pl.program_id(1)
    tm = x_ref.shape[0]
    n_slabs = WIDE_TILE // LANES

    @pl.when(j == 0)
    def _():
        x = x_ref[...]
        ms = jnp.mean(x * x, axis=-1, keepdims=True)
        h = (x * lax.rsqrt(ms + RMS_EPS) * nw_ref[...]).astype(BF16)
        h_scr[...] = h
        gates_ref[...] = jnp.dot(h, wg_ref[...], preferred_element_type=F32)

    chunk = min(INPROJ_CHUNK, tm)

    def chunks():
        for c in range(tm // chunk):
            rows = slice(c * chunk, (c + 1) * chunk)
            yield c, rows, jnp.dot(h_scr[rows, :], w_ref[...], preferred_element_type=F32)

    def rope(a, rows):
        reps = ATTN_DIM // LANES
        c = jnp.concatenate([cos_ref[rows, :]] * reps, axis=1)
        s = jnp.concatenate([sin_ref[rows, :]] * reps, axis=1)
        lane = lax.broadcasted_iota(jnp.int32, a.shape, 1)
        half = ATTN_HEAD_DIM // 2
        first = (lane % ATTN_HEAD_DIM) < half
        sw = jnp.where(first, pltpu.roll(a, ATTN_DIM - half, 1), pltpu.roll(a, half, 1))
        return a * c + sw * s

    def rotated(acc, rows):
        return jnp.concatenate([rope(acc[:, :ATTN_DIM], rows), rope(acc[:, ATTN_DIM:2 * ATTN_DIM], rows),
                                acc[:, 2 * ATTN_DIM:]], axis=1)

    def deinterleave(val, c, dst_ref, dilation):
        n = chunk // dilation
        for s in range(n_slabs):
            slab[s] = val[:, s * LANES:(s + 1) * LANES]
        if dilation == 16:
            quarter = chunk // 4
            for s in range(n_slabs):
                for r4 in range(4):
                    slab2[s, r4 * quarter:(r4 + 1) * quarter, :] = slab[s, pl.ds(r4, quarter, stride=4), :]
            for r in range(dilation):
                start = (r % 4) * quarter + r // 4
                piece = jnp.concatenate([slab2[s, pl.ds(start, n, stride=4), :] for s in range(n_slabs)], axis=1)
                dst_ref[0, r, c * n:(c + 1) * n, :] = piece.astype(BF16)
            return
        for r in range(dilation):
            piece = jnp.concatenate([slab[s, pl.ds(r, n, stride=dilation), :] for s in range(n_slabs)], axis=1)
            dst_ref[0, r, c * n:(c + 1) * n, :] = piece.astype(BF16)

    @pl.when(j < N_PLAIN_WIDE)
    def _():
        for _, rows, acc in chunks():
            out_ref[rows, :] = acc.astype(BF16)

    dst = (out_ref, g1_ref, g2_ref)
    for g, (_, dilation) in enumerate(ATTN_PATTERNS):
        @pl.when(j == N_PLAIN_WIDE + g)
        def _(g=g, dilation=dilation):
            for c, rows, acc in chunks():
                val = rotated(acc, rows)
                if dilation == 1:
                    out_ref[rows, :] = val.astype(BF16)
                else:
                    deinterleave(val, c, dst[g], dilation)


def _inproj(x2d, nw, w_main, w_gate, cos_t, sin_t, seq):
    t = x2d.shape[0]
    tm = min(1024, seq)
    tiles_per_seq = seq // tm
    chunk = min(INPROJ_CHUNK, tm)

    def group_out(dilation):
        return (pl.BlockSpec((1, dilation, tm // dilation, WIDE_TILE), lambda i, j: (i, 0, 0, 0)),
                jax.ShapeDtypeStruct((t // tm, dilation, tm // dilation, WIDE_TILE), BF16))

    (g1_spec, g1_shape), (g2_spec, g2_shape) = [group_out(ATTN_PATTERNS[g][1]) for g in (1, 2)]
    return pl.pallas_call(
        _inproj_kernel,
        grid=(t // tm, N_WIDE_TILES),
        in_specs=[
            pl.BlockSpec((tm, D_MODEL), lambda i, j: (i, 0)),
            pl.BlockSpec((1, D_MODEL), lambda i, j: (0, 0)),
            pl.BlockSpec((D_MODEL, WIDE_TILE), lambda i, j: (0, j)),
            pl.BlockSpec((D_MODEL, LANES), lambda i, j: (0, 0)),
            pl.BlockSpec((tm, LANES), lambda i, j: (i % tiles_per_seq, 0)),
            pl.BlockSpec((tm, LANES), lambda i, j: (i % tiles_per_seq, 0)),
        ],
        out_specs=[
            pl.BlockSpec((tm, WIDE_TILE), lambda i, j: (i, jnp.minimum(j, N_PLAIN_WIDE))),
            pl.BlockSpec((tm, LANES), lambda i, j: (i, 0)),
            g1_spec, g2_spec,
        ],
        out_shape=[
            jax.ShapeDtypeStruct((t, N_MAIN), BF16),
            jax.ShapeDtypeStruct((t, LANES), F32),
            g1_shape, g2_shape,
        ],
        scratch_shapes=[pltpu.VMEM((tm, D_MODEL), BF16), pltpu.VMEM((WIDE_TILE // LANES, chunk, LANES), F32),
                        pltpu.VMEM((WIDE_TILE // LANES, chunk, LANES), F32)],
        compiler_params=_params(("arbitrary", "arbitrary"), 56),
        name="inproj",
    )(x2d, nw, w_main, w_gate, cos_t, sin_t)


ATTN_QUERY_TILE = 128


def _window_mask(n, qt, sub_len):
    kt = qt + 2 * ATTN_BLOCK
    qpos = n * qt + lax.broadcasted_iota(jnp.int32, (qt, kt), 0)
    kpos = n * qt - ATTN_BLOCK + lax.broadcasted_iota(jnp.int32, (qt, kt), 1)
    return (jnp.abs(kpos - qpos) <= ATTN_BLOCK) & (kpos >= 0) & (kpos < sub_len)


def _attend(q, k, v, valid):
    qt = q.shape[0]
    pairs = range(ATTN_HEADS // 2)
    lane = lax.broadcasted_iota(jnp.int32, (qt, LANES), 1)
    lo = lane < ATTN_HEAD_DIM
    valid2 = jnp.concatenate([valid, valid], axis=0)
    scores = []
    for p in pairs:
        qp = q[:, p * LANES:(p + 1) * LANES]
        zero = jnp.zeros_like(qp)
        stacked = jnp.concatenate([jnp.where(lo, qp, zero), jnp.where(lo, zero, qp)], axis=0)
        scores.append(lax.dot_general(stacked, k[:, p * LANES:(p + 1) * LANES], (((1,), (1,)), ((), ())),
                                      preferred_element_type=F32))
    probs, inv_den, lses = [], [], []
    for p in pairs:
        s = jnp.where(valid2, scores[p], NEG_INF)
        mx = jnp.max(s, axis=-1, keepdims=True)
        e = jnp.exp(s - mx)
        den = jnp.sum(e, axis=-1, keepdims=True)
        probs.append(e.astype(BF16))
        inv_den.append(1.0 / den)
        lses.append(mx + jnp.log(den))
    pvs = [jnp.dot(probs[p], v[:, p * LANES:(p + 1) * LANES], preferred_element_type=F32) for p in pairs]
    lse_out = jnp.zeros((qt, LANES), F32)
    outs = []
    for p in pairs:
        o = pvs[p] * inv_den[p]
        outs.append(jnp.where(lo, o[:qt], o[qt:]))
        lse_out = jnp.where(lane == 2 * p, lses[p][:qt], jnp.where(lane == 2 * p + 1, lses[p][qt:], lse_out))
    return outs, lse_out


def _head_expander():
    row = lax.broadcasted_iota(jnp.int32, (LANES, ATTN_DIM), 0)
    col = lax.broadcasted_iota(jnp.int32, (LANES, ATTN_DIM), 1)
    return jnp.where(col // ATTN_HEAD_DIM == row, 1.0, 0.0).astype(BF16)


def _merge_groups(o_cur, lse_cur, o_prev, lse_prev, expander):
    m = jnp.maximum(lse_prev, lse_cur)
    a = jnp.exp(lse_prev - m)
    tot = a + jnp.exp(lse_cur - m)
    w_prev = a / tot
    hi = w_prev.astype(BF16)
    lo = (w_prev - hi.astype(F32)).astype(BF16)
    spread = (jnp.dot(hi, expander, preferred_element_type=F32) + jnp.dot(lo, expander, preferred_element_type=F32))
    outs = [o_cur[p] + spread[:, p * LANES:(p + 1) * LANES] * (o_prev[p] - o_cur[p])
            for p in range(ATTN_HEADS // 2)]
    lane = lax.broadcasted_iota(jnp.int32, m.shape, 1)
    return outs, jnp.where(lane < ATTN_HEADS, m + jnp.log(tot), 0.0)


def _attn_kernel(q_ref, kp_ref, kc_ref, kn_ref, vp_ref, vc_ref, vn_ref, o_ref, l_ref, *, sub_len, qt):
    n_sub = q_ref.shape[1] // qt
    k = jnp.concatenate([kp_ref[0], kc_ref[0], kn_ref[0]], axis=0)
    v = jnp.concatenate([vp_ref[0], vc_ref[0], vn_ref[0]], axis=0)
    for sub in range(n_sub):
        valid = _window_mask(pl.program_id(1) * n_sub + sub, qt, sub_len)
        rows = slice(sub * qt, (sub + 1) * qt)
        keys = slice(sub * qt, (sub + 1) * qt + 2 * ATTN_BLOCK)
        outs, lse = _attend(q_ref[0, rows, :], k[keys], v[keys], valid)
        for p, o in enumerate(outs):
            o_ref[0, rows, p * LANES:(p + 1) * LANES] = o.astype(BF16)
        l_ref[0, rows, :] = lse


MERGE_ROWS = 256


def _attn_dilated_kernel(q_ref, kp_ref, kc_ref, kn_ref, vp_ref, vc_ref, vn_ref, op_ref, lp_ref, o_ref, l_ref,
                         o_slab, l_slab, *, sub_len, qt, dilation):
    valid = _window_mask(pl.program_id(1), qt, sub_len)

    per_iter = 4 if qt <= ATTN_BLOCK else 2

    def residues(i, c):
        for r in [per_iter * i + u for u in range(per_iter)]:
            k = jnp.concatenate([kp_ref[0, r], kc_ref[0, r], kn_ref[0, r]], axis=0)
            v = jnp.concatenate([vp_ref[0, r], vc_ref[0, r], vn_ref[0, r]], axis=0)
            outs, lse = _attend(q_ref[0, r], k, v, valid)
            for p, o in enumerate(outs):
                o_slab[p, pl.ds(r, qt, stride=dilation), :] = o
            l_slab[pl.ds(r, qt, stride=dilation), :] = lse
        return c

    lax.fori_loop(0, dilation // per_iter, residues, 0)
    expander = _head_expander()

    def merge(i, c):
        rows = pl.ds(pl.multiple_of(i * MERGE_ROWS, MERGE_ROWS), MERGE_ROWS)
        prev = op_ref[rows, :].astype(F32)
        o_cur = [o_slab[p, rows, :] for p in range(ATTN_HEADS // 2)]
        o_prev = [prev[:, p * LANES:(p + 1) * LANES] for p in range(ATTN_HEADS // 2)]
        outs, lse = _merge_groups(o_cur, l_slab[rows, :], o_prev, lp_ref[rows, :], expander)
        for p, o in enumerate(outs):
            o_ref[rows, p * LANES:(p + 1) * LANES] = o.astype(BF16)
        l_ref[rows, :] = lse
        return c

    lax.fori_loop(0, qt * dilation // MERGE_ROWS, merge, 0)


PLAIN_STEP_TILES = 2


def _attn_plain(proj3):
    b, s, _ = proj3.shape
    qt = min(ATTN_QUERY_TILE, s)
    step = min(PLAIN_STEP_TILES * qt, s)
    nblk = s // ATTN_BLOCK
    qb = step // ATTN_BLOCK

    def cur(which):
        return pl.BlockSpec((1, step, COL_TILE), lambda bi, n: (bi, n, ATTN_TILE0 + which))

    def before(which):
        return pl.BlockSpec((1, ATTN_BLOCK, COL_TILE),
                            lambda bi, n: (bi, jnp.maximum(n * qb - 1, 0), ATTN_TILE0 + which))

    def after(which):
        return pl.BlockSpec((1, ATTN_BLOCK, COL_TILE),
                            lambda bi, n: (bi, jnp.minimum((n + 1) * qb, nblk - 1), ATTN_TILE0 + which))

    o, l = pl.pallas_call(
        functools.partial(_attn_kernel, sub_len=s, qt=qt),
        grid=(b, s // step),
        in_specs=[cur(0), before(1), cur(1), after(1), before(2), cur(2), after(2)],
        out_specs=[pl.BlockSpec((1, step, ATTN_DIM), lambda bi, n: (bi, n, 0)),
                   pl.BlockSpec((1, step, LANES), lambda bi, n: (bi, n, 0))],
        out_shape=[jax.ShapeDtypeStruct((b, s, ATTN_DIM), BF16), jax.ShapeDtypeStruct((b, s, LANES), F32)],
        compiler_params=_params(("arbitrary", "arbitrary"), 48),
        name="attn_g0",
    )(*([proj3] * 7))
    return o.reshape(b * s, ATTN_DIM), l.reshape(b * s, LANES)


def _attn_dilated(qkv, g, dilation, prev, b, s):
    n_tiles, _, rows, _ = qkv.shape
    tiles_per_seq = n_tiles // b
    sub_len = s // dilation
    qt = min(ATTN_QUERY_TILE, rows)
    span = qt * dilation
    q_per_tile = rows // qt
    h_per_tile = rows // ATTN_BLOCK
    qb = qt // ATTN_BLOCK
    nblk = sub_len // ATTN_BLOCK

    def cur(which):
        return pl.BlockSpec((1, dilation, qt, COL_TILE),
                            lambda bi, n: (bi * tiles_per_seq + n // q_per_tile, 0, n % q_per_tile, which))

    def halo(which, blk_of):
        def index(bi, n):
            blk = blk_of(n)
            return (bi * tiles_per_seq + blk // h_per_tile, 0, blk % h_per_tile, which)
        return pl.BlockSpec((1, dilation, ATTN_BLOCK, COL_TILE), index)

    before = lambda which: halo(which, lambda n: jnp.maximum(n * qb - 1, 0))
    after = lambda which: halo(which, lambda n: jnp.minimum((n + 1) * qb, nblk - 1))
    steps = sub_len // qt
    o_spec = pl.BlockSpec((span, ATTN_DIM), lambda bi, n: (bi * steps + n, 0))
    l_spec = pl.BlockSpec((span, LANES), lambda bi, n: (bi * steps + n, 0))
    return pl.pallas_call(
        functools.partial(_attn_dilated_kernel, sub_len=sub_len, qt=qt, dilation=dilation),
        grid=(b, steps),
        in_specs=[cur(0), before(1), cur(1), after(1), before(2), cur(2), after(2), o_spec, l_spec],
        out_specs=[o_spec, l_spec],
        out_shape=[jax.ShapeDtypeStruct((b * s, ATTN_DIM), BF16), jax.ShapeDtypeStruct((b * s, LANES), F32)],
        scratch_shapes=[pltpu.VMEM((ATTN_HEADS // 2, span, LANES), F32), pltpu.VMEM((span, LANES), F32)],
        compiler_params=_params(("arbitrary", "arbitrary"), 56),
        name=f"attn_g{g}",
    )(*([qkv] * 7), prev[0], prev[1])


CONV_HALO = 16


Q_TILES = ML_DIM // COL_TILE


def _conv_kernel(xp_ref, xc_ref, xn_ref, w_ref, b_ref, q_ref, kt_ref, buf, *, tm):
    i = pl.program_id(1)
    c = pl.program_id(2)
    last = pl.num_programs(1) - 1
    before = jnp.where(i > 0, xp_ref[0].astype(F32), 0.0)
    centre = xc_ref[0].astype(F32)
    after = jnp.where(i < last, xn_ref[0].astype(F32), 0.0)
    w = w_ref[...]
    bias = b_ref[...]
    pieces = []
    for s in range(COL_TILE // LANES):
        lanes = slice(s * LANES, (s + 1) * LANES)
        buf[s, 0:CONV_HALO, :] = before[:, lanes]
        buf[s, CONV_HALO:CONV_HALO + tm, :] = centre[:, lanes]
        buf[s, CONV_HALO + tm:, :] = after[:, lanes]
        acc = jnp.broadcast_to(bias[:, lanes], (tm, LANES))
        for tap in range(CONV_W):
            acc = acc + w[tap:tap + 1, lanes] * buf[s, pl.ds(CONV_HALO - CONV_W // 2 + tap, tm, stride=1), :]
        pieces.append(acc)
    y = jnp.concatenate(pieces, axis=1)
    y = y * _sigmoid(y)

    @pl.when(c < Q_TILES)
    def _():
        q_ref[0] = y.astype(BF16)

    @pl.when(c >= Q_TILES)
    def _():
        kt_ref[0] = (y * ML_HEAD_DIM ** -0.5).T.astype(BF16)


def _conv_qk(proj3, conv_w, conv_b):
    b, s, _ = proj3.shape
    tm = min(1024, s)
    hb = tm // CONV_HALO
    nh = s // CONV_HALO
    return pl.pallas_call(
        functools.partial(_conv_kernel, tm=tm),
        grid=(b, s // tm, 2 * Q_TILES),
        in_specs=[
            pl.BlockSpec((1, CONV_HALO, COL_TILE), lambda bi, i, c: (bi, jnp.maximum(i * hb - 1, 0), c)),
            pl.BlockSpec((1, tm, COL_TILE), lambda bi, i, c: (bi, i, c)),
            pl.BlockSpec((1, CONV_HALO, COL_TILE), lambda bi, i, c: (bi, jnp.minimum((i + 1) * hb, nh - 1), c)),
            pl.BlockSpec((CONV_W, COL_TILE), lambda bi, i, c: (0, c)),
            pl.BlockSpec((1, COL_TILE), lambda bi, i, c: (0, c)),
        ],
        out_specs=[
            pl.BlockSpec((1, tm, COL_TILE), lambda bi, i, c: (bi, i, jnp.minimum(c, Q_TILES - 1))),
            pl.BlockSpec((1, COL_TILE, tm), lambda bi, i, c: (bi, jnp.maximum(c - Q_TILES, 0), i)),
        ],
        out_shape=[jax.ShapeDtypeStruct((b, s, ML_DIM), BF16), jax.ShapeDtypeStruct((b, ML_DIM, s), BF16)],
        scratch_shapes=[pltpu.VMEM((COL_TILE // LANES, tm + 2 * CONV_HALO, LANES), F32)],
        compiler_params=_params(("arbitrary", "arbitrary", "arbitrary"), 32),
        name="conv_qk",
    )(proj3, proj3, proj3, conv_w, conv_b)


def _gate_kernel(g_ref, b_ref, o_ref):
    g = g_ref[0] + b_ref[...]
    width = g.shape[1]
    logsig = jnp.minimum(g, 0.0) - jnp.log(1.0 + jnp.exp(-jnp.abs(g)))
    lane = lax.broadcasted_iota(jnp.int32, g.shape, 1) % ML_CHUNK
    pre = logsig
    suf = logsig
    step = 1
    while step < ML_CHUNK:
        pre = pre + jnp.where(lane >= step, pltpu.roll(pre, step, 1), 0.0)
        suf = suf + jnp.where(lane < ML_CHUNK - step, pltpu.roll(suf, width - step, 1), 0.0)
        step *= 2
    row = lax.broadcasted_iota(jnp.int32, g.shape, 0)
    is_f_fwd = (row >= ML_HEADS) & (row < 2 * ML_HEADS)
    is_f_bwd = row >= 3 * ML_HEADS
    o_ref[0] = jnp.where(is_f_fwd, pre, jnp.where(is_f_bwd, suf, g))


def _gate_prep(gates_t, gate_b):
    b, rows, s = gates_t.shape
    sb = min(2048, s)
    return pl.pallas_call(
        _gate_kernel,
        grid=(b, s // sb),
        in_specs=[
            pl.BlockSpec((1, rows, sb), lambda bi, i: (bi, 0, i)),
            pl.BlockSpec((rows, 1), lambda bi, i: (0, 0)),
        ],
        out_specs=pl.BlockSpec((1, rows, sb), lambda bi, i: (bi, 0, i)),
        out_shape=jax.ShapeDtypeStruct((b, rows, s), F32),
        compiler_params=_params(("arbitrary", "arbitrary"), 32),
        name="gate_prep",
    )(gates_t, gate_b)


ML_CHAIN_GROUP = 8


def _mlstm_kernel(qf_ref, kf_ref, vf_ref, qb_ref, kb_ref, vb_ref, grf_ref, grb_ref, gcf_ref, gcb_ref,
                  hf_ref, hb_ref, s_scr, m_scr):
    c = pl.program_id(1)

    @pl.when(c == 0)
    def _():
        def clear(i, carry):
            s_scr[i] = jnp.zeros(s_scr.shape[1:], F32)
            return carry

        lax.fori_loop(0, s_scr.shape[0], clear, 0)
        m_scr[...] = jnp.zeros_like(m_scr)

    t_i = lax.broadcasted_iota(jnp.int32, (ML_CHUNK, ML_CHUNK), 0)
    s_i = lax.broadcasted_iota(jnp.int32, (ML_CHUNK, ML_CHUNK), 1)
    ones_col = jnp.where(lax.broadcasted_iota(jnp.int32, (ML_CHUNK, LANES), 1) == 0, 1.0, 0.0).astype(BF16)
    dirs = ((qf_ref, kf_ref, vf_ref, grf_ref, gcf_ref, hf_ref), (qb_ref, kb_ref, vb_ref, grb_ref, gcb_ref, hb_ref))
    all_chains = [(dirn, head) + refs for dirn, refs in enumerate(dirs) for head in range(ML_HEADS)]

    for first in range(0, len(all_chains), ML_CHAIN_GROUP):
        chains = all_chains[first:first + ML_CHAIN_GROUP]
        matmuls = []
        for dirn, head, q_ref, k_ref, v_ref, _, _, _ in chains:
            hs = slice(head * ML_HEAD_DIM, (head + 1) * ML_HEAD_DIM)
            q = q_ref[0, :, hs]
            qk_raw = jnp.dot(q, k_ref[0, hs, :], preferred_element_type=F32)
            q_state = jnp.dot(q, s_scr[dirn * ML_HEADS + head].astype(BF16), preferred_element_type=F32)
            matmuls.append((qk_raw, q_state))

        weights = []
        for dirn, head, _, _, _, gr_ref, gc_ref, _ in chains:
            ii = dirn * 2 * ML_HEADS + head
            bi = ii + ML_HEADS
            mask = (s_i <= t_i) if dirn == 0 else (s_i >= t_i)
            i_row, b_row = gr_ref[0, ii:ii + 1, :], gr_ref[0, bi:bi + 1, :]
            b_col = gc_ref[0, :, bi:bi + 1]
            m = m_scr[dirn * ML_HEADS + head][0:1, 0:1]
            dmat = jnp.where(mask, b_col - b_row + i_row, NEG_INF)
            inter = b_col + m
            m_t = jnp.maximum(inter, jnp.max(dmat, axis=-1, keepdims=True))
            weights.append((jnp.exp(dmat - m_t), jnp.exp(inter - m_t), jnp.exp(-m_t)))

        for (dirn, head, _, _, v_ref, _, _, h_ref), (qk_raw, q_state), (w_intra, w_inter, floor) in zip(
                chains, matmuls, weights):
            hs = slice(head * ML_HEAD_DIM, (head + 1) * ML_HEAD_DIM)
            v_ext = jnp.concatenate([v_ref[0, :, hs], ones_col], axis=1)
            num = w_inter * q_state + jnp.dot((qk_raw * w_intra).astype(BF16), v_ext, preferred_element_type=F32)
            den = jnp.maximum(jnp.abs(num[:, ML_HEAD_DIM:ML_HEAD_DIM + 1]), floor)
            h_ref[0, :, hs] = (num[:, :ML_HEAD_DIM] / den).astype(BF16)

        for dirn, head, _, k_ref, v_ref, gr_ref, _, _ in chains:
            idx = dirn * ML_HEADS + head
            ii = dirn * 2 * ML_HEADS + head
            bi = ii + ML_HEADS
            hs = slice(head * ML_HEAD_DIM, (head + 1) * ML_HEAD_DIM)
            i_row, b_row = gr_ref[0, ii:ii + 1, :], gr_ref[0, bi:bi + 1, :]
            b_last = b_row[:, ML_CHUNK - 1:ML_CHUNK] if dirn == 0 else b_row[:, 0:1]
            m = m_scr[idx][0:1, 0:1]
            log_w = b_last - b_row + i_row
            m_new = jnp.maximum(b_last + m, jnp.max(log_w, axis=-1, keepdims=True))
            decay = jnp.exp(b_last + m - m_new)
            v_ext = jnp.concatenate([v_ref[0, :, hs], ones_col], axis=1)
            wk_t = (k_ref[0, hs, :].astype(F32) * jnp.exp(log_w - m_new)).astype(BF16)
            s_scr[idx] = decay * s_scr[idx] + jnp.dot(wk_t, v_ext, preferred_element_type=F32)
            m_scr[idx] = jnp.broadcast_to(m_new, m_scr.shape[1:])


def _mlstm(proj3, q, k_t, g_rows, g_cols):
    b, s, _ = proj3.shape
    nc = s // ML_CHUNK
    v_tile = 2 * ML_DIM // ML_DIM
    n_rows = g_rows.shape[1]

    def rows(col, reverse):
        return pl.BlockSpec((1, ML_CHUNK, ML_DIM), lambda bi, c: (bi, nc - 1 - c if reverse else c, col))

    def cols(height, reverse):
        return pl.BlockSpec((1, height, ML_CHUNK), lambda bi, c: (bi, 0, nc - 1 - c if reverse else c))

    return pl.pallas_call(
        _mlstm_kernel,
        grid=(b, nc),
        in_specs=[
            rows(0, False), cols(ML_DIM, False), rows(v_tile, False),
            rows(0, True), cols(ML_DIM, True), rows(v_tile, True),
            cols(n_rows, False), cols(n_rows, True),
            pl.BlockSpec((1, ML_CHUNK, n_rows), lambda bi, c: (bi, c, 0)),
            pl.BlockSpec((1, ML_CHUNK, n_rows), lambda bi, c: (bi, nc - 1 - c, 0)),
        ],
        out_specs=[rows(0, False), rows(0, True)],
        out_shape=[jax.ShapeDtypeStruct((b, s, ML_DIM), BF16)] * 2,
        scratch_shapes=[
            pltpu.VMEM((2 * ML_HEADS, ML_HEAD_DIM, ML_EXT), F32),
            pltpu.VMEM((2 * ML_HEADS, 8, LANES), F32),
        ],
        compiler_params=_params(("arbitrary", "arbitrary"), 32),
        name="mlstm",
    )(q, k_t, proj3, q, k_t, proj3, g_rows, g_rows, g_cols, g_cols)


def _pack_bf16_pair(lo, hi):
    lo_bits = lax.bitcast_convert_type(lo.astype(BF16).astype(F32), jnp.uint32)
    hi_bits = lax.bitcast_convert_type(hi.astype(BF16).astype(F32), jnp.uint32)
    return (hi_bits & jnp.uint32(0xFFFF0000)) | (lo_bits >> 16)


def _unpack_bf16_pair(packed):
    lo = lax.bitcast_convert_type(packed << 16, F32)
    hi = lax.bitcast_convert_type(packed & jnp.uint32(0xFFFF0000), F32)
    return lo, hi


MERGE_CHUNK = 512


def _merge_kernel(x_ref, ao_ref, hf_ref, hb_ref, mo_ref, mg_ref, wpa_ref, wpm_ref, wo_ref, nfw_ref,
                  rw_ref, rb_ref, x1_ref, h2_ref, idx_ref, wt_ref):
    tm = x_ref.shape[0]
    chunk = min(MERGE_CHUNK, tm)
    for c in range(tm // chunk):
        rows = slice(c * chunk, (c + 1) * chunk)
        y_attn = jnp.dot(ao_ref[rows, :], wpa_ref[...], preferred_element_type=F32)
        hsum = hf_ref[rows, :].astype(F32) + hb_ref[rows, :].astype(F32)
        ml = (_sigmoid(mo_ref[rows, :].astype(F32)) * hsum).astype(BF16)
        y_ml = jnp.dot(ml, wpm_ref[...], preferred_element_type=F32)
        gates = _sigmoid(mg_ref[rows, :].astype(F32))
        mixed = (gates[:, :D_MODEL] * y_attn + gates[:, D_MODEL:] * y_ml).astype(BF16)
        x1 = x_ref[rows, :] + jnp.dot(mixed, wo_ref[...], preferred_element_type=F32)
        x1_ref[rows, :] = x1
        ms = jnp.mean(x1 * x1, axis=-1, keepdims=True)
        h2 = x1 * lax.rsqrt(ms + RMS_EPS) * nfw_ref[...]
        h2_ref[rows, :] = _pack_bf16_pair(h2[:, :D_MODEL // 2], h2[:, D_MODEL // 2:])

        logits = lax.dot_general(rw_ref[...], h2, (((1,), (1,)), ((), ())), preferred_element_type=F32,
                                 precision=lax.Precision.HIGHEST) + rb_ref[...]
        row = lax.broadcasted_iota(jnp.int32, logits.shape, 0)
        rest = logits
        vals = []
        for k in range(TOP_K):
            mx = jnp.max(rest, axis=0, keepdims=True)
            first = jnp.min(jnp.where(rest == mx, row, N_EXPERTS), axis=0, keepdims=True)
            vals.append(mx)
            idx_ref[k:k + 1, rows] = first
            rest = jnp.where(row == first, -jnp.inf, rest)
        exps = [jnp.exp(v - vals[0]) for v in vals]
        tot = exps[0] + exps[1] + exps[2] + exps[3]
        for k in range(TOP_K):
            wt_ref[k:k + 1, rows] = exps[k] / tot


def _merge(x2d, attn_o, h_f, h_b, proj, wpa, wpm, wo, nfw, rw_t, rb):
    t = x2d.shape[0]
    tm = min(512, t)
    ml_o_tile = ML_QKV_COLS // ML_O_COLS
    merge_tile = (ML_QKV_COLS + ML_O_COLS) // MERGE_COLS
    row = lambda width: pl.BlockSpec((tm, width), lambda i: (i, 0))
    full = lambda a: pl.BlockSpec(a.shape, lambda i: (0, 0))
    return pl.pallas_call(
        _merge_kernel,
        grid=(t // tm,),
        in_specs=[
            row(D_MODEL), row(ATTN_DIM), row(ML_DIM), row(ML_DIM),
            pl.BlockSpec((tm, ML_O_COLS), lambda i: (i, ml_o_tile)),
            pl.BlockSpec((tm, MERGE_COLS), lambda i: (i, merge_tile)),
            full(wpa), full(wpm), full(wo), full(nfw), full(rw_t), full(rb),
        ],
        out_specs=[row(D_MODEL), row(D_MODEL // 2), pl.BlockSpec((TOP_K, tm), lambda i: (0, i)),
                   pl.BlockSpec((TOP_K, tm), lambda i: (0, i))],
        out_shape=[
            jax.ShapeDtypeStruct((t, D_MODEL), F32),
            jax.ShapeDtypeStruct((t, D_MODEL // 2), jnp.uint32),
            jax.ShapeDtypeStruct((TOP_K, t), jnp.int32),
            jax.ShapeDtypeStruct((TOP_K, t), F32),
        ],
        compiler_params=_params(("arbitrary",), 48),
        name="merge",
    )(x2d, attn_o, h_f, h_b, proj, proj, wpa, wpm, wo, nfw, rw_t, rb)


ROUTE_TILE = 512


def _route_kernel(idx_ref, pos_ref, cnt_ref, base):
    @pl.when(pl.program_id(0) == 0)
    def _():
        base[...] = jnp.zeros_like(base)

    idx = idx_ref[...]
    row = lax.broadcasted_iota(jnp.int32, (N_EXPERTS, ROUTE_TILE), 0)
    onehot = jnp.zeros((N_EXPERTS, ROUTE_TILE), F32)
    for k in range(TOP_K):
        onehot = onehot + jnp.where(row == idx[k:k + 1, :], 1.0, 0.0)
    s_i = lax.broadcasted_iota(jnp.int32, (ROUTE_TILE, ROUTE_TILE), 0)
    t_i = lax.broadcasted_iota(jnp.int32, (ROUTE_TILE, ROUTE_TILE), 1)
    upper = jnp.where(s_i <= t_i, 1.0, 0.0).astype(BF16)
    incl = jnp.dot(onehot.astype(BF16), upper, preferred_element_type=F32)
    before = base[:, 0:1]
    count = incl + before
    for k in range(TOP_K):
        mine = jnp.sum(jnp.where(row == idx[k:k + 1, :], count, 0.0), axis=0, keepdims=True)
        pos_ref[k:k + 1, :] = (mine - 1.0).astype(jnp.int32)
    total = before + incl[:, ROUTE_TILE - 1:ROUTE_TILE]
    base[...] = jnp.broadcast_to(total, base.shape)
    cnt_ref[...] = jnp.broadcast_to(total, cnt_ref.shape)


def _route(idx):
    t = idx.shape[1]
    return pl.pallas_call(
        _route_kernel,
        grid=(t // ROUTE_TILE,),
        in_specs=[pl.BlockSpec((TOP_K, ROUTE_TILE), lambda i: (0, i))],
        out_specs=[pl.BlockSpec((TOP_K, ROUTE_TILE), lambda i: (0, i)),
                   pl.BlockSpec((N_EXPERTS, LANES), lambda i: (0, 0))],
        out_shape=[jax.ShapeDtypeStruct((TOP_K, t), jnp.int32),
                   jax.ShapeDtypeStruct((N_EXPERTS, LANES), F32)],
        scratch_shapes=[pltpu.VMEM((N_EXPERTS, LANES), F32)],
        compiler_params=_params(("arbitrary",), 32),
        name="route",
    )(idx)


GROUP_TILE = 512
DISPATCH_TILE = 2048
ROW_WORDS = D_MODEL // 2


def _dispatch_kernel(slot_hbm, ztile_ref, nu_ref, *refs, steps):
    srcs = refs[:len(steps)]
    dst_hbm, slot_smem, zbuf, sem = refs[len(steps):]
    i = pl.program_id(0)
    n_slots = DISPATCH_TILE * TOP_K
    n_tiles = dst_hbm.shape[0] // GROUP_TILE
    load = pltpu.make_async_copy(slot_hbm.at[pl.ds(i * n_slots, n_slots)], slot_smem, sem.at[0])
    load.start()

    @pl.when(i == 0)
    def _():
        zbuf[...] = jnp.zeros_like(zbuf)

        def fill_tile(tile):
            pltpu.make_async_copy(zbuf, dst_hbm.at[pl.ds(tile * GROUP_TILE, GROUP_TILE), :], sem.at[1]).start()

        def fill_last(e, n):
            tile = ztile_ref[e]

            @pl.when(tile >= 0)
            def _():
                fill_tile(tile)

            return n + jnp.where(tile >= 0, 1, 0)

        def fill_unused(tile, c):
            fill_tile(tile)
            return c

        n_fill = lax.fori_loop(0, N_EXPERTS, fill_last, 0)
        lax.fori_loop(nu_ref[0], n_tiles, fill_unused, 0)

        def drain(_, c):
            pltpu.make_async_copy(zbuf, dst_hbm.at[pl.ds(0, GROUP_TILE), :], sem.at[1]).wait()
            return c

        lax.fori_loop(0, n_fill + n_tiles - nu_ref[0], drain, 0)

    load.wait()

    first = 0
    for src_ref, n_steps in zip(srcs, steps):
        @pl.when((i >= first) & (i < first + n_steps))
        def _(src_ref=src_ref):
            def issue(t8, c):
                for r in range(SUBLANES):
                    src = src_ref.at[t8, pl.ds(r, 1), :]
                    for k in range(TOP_K):
                        slot = slot_smem[(t8 * SUBLANES + r) * TOP_K + k]
                        pltpu.make_async_copy(src, dst_hbm.at[pl.ds(slot, 1), :],
                                              sem.at[1]).start(priority=k % 2)
                return c

            lax.fori_loop(0, DISPATCH_TILE // SUBLANES, issue, 0)

        first += n_steps
    pltpu.make_async_copy(dst_hbm.at[pl.ds(0, n_slots), :], dst_hbm.at[pl.ds(0, n_slots), :], sem.at[1]).wait()


def _dispatch(slots, ztile, n_used, sources, n_rows):
    steps = tuple(h.shape[0] // DISPATCH_TILE for h in sources)
    any_spec = pl.BlockSpec(memory_space=pl.ANY)
    smem_spec = pl.BlockSpec(memory_space=pltpu.SMEM)
    src_specs = []
    first = 0
    for n_steps in steps:
        src_specs.append(pl.BlockSpec(
            (DISPATCH_TILE // SUBLANES, SUBLANES, ROW_WORDS),
            lambda i, first=first, n_steps=n_steps: (jnp.clip(i - first, 0, n_steps - 1), 0, 0)))
        first += n_steps
    sources = [h.reshape(h.shape[0] // SUBLANES, SUBLANES, ROW_WORDS) for h in sources]
    return pl.pallas_call(
        functools.partial(_dispatch_kernel, steps=steps),
        grid=(sum(steps),),
        in_specs=[any_spec, smem_spec, smem_spec] + src_specs,
        out_specs=any_spec,
        out_shape=jax.ShapeDtypeStruct((n_rows, ROW_WORDS), jnp.uint32),
        scratch_shapes=[
            pltpu.SMEM((DISPATCH_TILE * TOP_K,), jnp.int32),
            pltpu.VMEM((GROUP_TILE, ROW_WORDS), jnp.uint32),
            pltpu.SemaphoreType.DMA((2,)),
        ],
        compiler_params=_params(("arbitrary",), 32),
        name="dispatch",
    )(slots, ztile, n_used, *sources)


EXPERT_CHUNK = 256


def _expert_kernel(te_ref, ts_ref, nu_ref, x_ref, wgu_ref, bgu_ref, wd_ref, bd_ref, y_ref):
    j = pl.program_id(0)

    @pl.when(j < nu_ref[0])
    def _():
        half = D_MODEL // 2
        for c in range(GROUP_TILE // EXPERT_CHUNK):
            rows = slice(c * EXPERT_CHUNK, (c + 1) * EXPERT_CHUNK)
            lo, hi = _unpack_bf16_pair(x_ref[rows, :])
            gu = (jnp.dot(lo.astype(BF16), wgu_ref[0, :half, :], preferred_element_type=F32)
                  + jnp.dot(hi.astype(BF16), wgu_ref[0, half:, :], preferred_element_type=F32) + bgu_ref[0])
            gate = jnp.minimum(gu[:, :D_FF], SWIGLU_LIMIT)
            up = jnp.clip(gu[:, D_FF:], -SWIGLU_LIMIT, SWIGLU_LIMIT)
            hid = (up + 1.0) * gate * _sigmoid(SWIGLU_ALPHA * gate)
            y = jnp.dot(hid.astype(BF16), wd_ref[0], preferred_element_type=F32) + bd_ref[0]
            y_ref[rows, :] = _pack_bf16_pair(y[:, :half], y[:, half:])

    @pl.when(j >= nu_ref[0])
    def _():
        y_ref[...] = jnp.zeros_like(y_ref)


def _experts(tile_expert, tile_src, n_used, xs, wgu, bgu, wd, bd):
    n_rows = xs.shape[0]
    row_spec = pl.BlockSpec((GROUP_TILE, ROW_WORDS), lambda j, te, ts, nu: (ts[j], 0))
    out_spec = pl.BlockSpec((GROUP_TILE, ROW_WORDS), lambda j, te, ts, nu: (j, 0))
    return pl.pallas_call(
        _expert_kernel,
        grid_spec=pltpu.PrefetchScalarGridSpec(
            num_scalar_prefetch=3,
            grid=(n_rows // GROUP_TILE,),
            in_specs=[
                row_spec,
                pl.BlockSpec((1, D_MODEL, 2 * D_FF), lambda j, te, ts, nu: (te[j], 0, 0)),
                pl.BlockSpec((1, 1, 2 * D_FF), lambda j, te, ts, nu: (te[j], 0, 0)),
                pl.BlockSpec((1, D_FF, D_MODEL), lambda j, te, ts, nu: (te[j], 0, 0)),
                pl.BlockSpec((1, 1, D_MODEL), lambda j, te, ts, nu: (te[j], 0, 0)),
            ],
            out_specs=out_spec,
        ),
        out_shape=jax.ShapeDtypeStruct((n_rows, ROW_WORDS), jnp.uint32),
        compiler_params=_params(("arbitrary",), 56),
        name="experts",
    )(tile_expert, tile_src, n_used, xs, wgu, bgu, wd, bd)


COMBINE_TILE = 512


def _combine_kernel(slot_hbm, ys_hbm, x1_ref, wt_ref, nw_ref, o_ref, slot_a, slot_b, buf_a, buf_b, sem):
    i = pl.program_id(0)
    n = pl.num_programs(0)
    n_slots = COMBINE_TILE * TOP_K
    slot_bufs = (slot_a, slot_b)
    row_bufs = (buf_a, buf_b)

    def slot_load(tile, par):
        return pltpu.make_async_copy(slot_hbm.at[pl.ds(tile * n_slots, n_slots)], slot_bufs[par], sem.at[par])

    def issue_rows(par):
        def issue(t8, c):
            for r in range(SUBLANES):
                for k in range(TOP_K):
                    slot = slot_bufs[par][(t8 * SUBLANES + r) * TOP_K + k]
                    pltpu.make_async_copy(ys_hbm.at[pl.ds(slot, 1), :], row_bufs[par].at[k, t8, pl.ds(r, 1), :],
                                          sem.at[2 + par]).start(priority=k % 2)
            return c

        lax.fori_loop(0, COMBINE_TILE // SUBLANES, issue, 0)

    def reduce_rows(par):
        pltpu.make_async_copy(ys_hbm.at[pl.ds(0, n_slots), :], ys_hbm.at[pl.ds(0, n_slots), :],
                              sem.at[2 + par]).wait()
        half = D_MODEL // 2
        wt = wt_ref[...]
        acc_lo = x1_ref[:, :half]
        acc_hi = x1_ref[:, half:]
        for k in range(TOP_K):
            lo, hi = _unpack_bf16_pair(row_bufs[par][k].reshape(COMBINE_TILE, ROW_WORDS))
            acc_lo = acc_lo + wt[:, k:k + 1] * lo
            acc_hi = acc_hi + wt[:, k:k + 1] * hi
        ms = (jnp.sum(acc_lo * acc_lo, axis=-1, keepdims=True)
              + jnp.sum(acc_hi * acc_hi, axis=-1, keepdims=True)) * (1.0 / D_MODEL)
        inv = lax.rsqrt(ms + RMS_EPS)
        o_ref[:, :half] = acc_lo * inv * nw_ref[:, :half]
        o_ref[:, half:] = acc_hi * inv * nw_ref[:, half:]

    @pl.when(i == 0)
    def _():
        first = slot_load(0, 0)
        first.start()
        first.wait()
        issue_rows(0)

        @pl.when(n > 1)
        def _():
            slot_load(1, 1).start()

    for par in range(2):
        @pl.when(i % 2 == par)
        def _(par=par):
            @pl.when(i + 1 < n)
            def _():
                slot_load(i + 1, 1 - par).wait()
                issue_rows(1 - par)

            @pl.when(i + 2 < n)
            def _():
                slot_load(i + 2, par).start()

            reduce_rows(par)


def _combine(slots, ys, x1, wt, nw):
    t = x1.shape[0]
    tm = COMBINE_TILE
    return pl.pallas_call(
        _combine_kernel,
        grid=(t // tm,),
        in_specs=[
            pl.BlockSpec(memory_space=pl.ANY),
            pl.BlockSpec(memory_space=pl.ANY),
            pl.BlockSpec((tm, D_MODEL), lambda i: (i, 0)),
            pl.BlockSpec((tm, TOP_K), lambda i: (i, 0)),
            pl.BlockSpec((1, D_MODEL), lambda i: (0, 0)),
        ],
        out_specs=pl.BlockSpec((tm, D_MODEL), lambda i: (i, 0)),
        out_shape=jax.ShapeDtypeStruct((t, D_MODEL), F32),
        scratch_shapes=[
            pltpu.SMEM((tm * TOP_K,), jnp.int32),
            pltpu.SMEM((tm * TOP_K,), jnp.int32),
            pltpu.VMEM((TOP_K, tm // SUBLANES, SUBLANES, ROW_WORDS), jnp.uint32),
            pltpu.VMEM((TOP_K, tm // SUBLANES, SUBLANES, ROW_WORDS), jnp.uint32),
            pltpu.SemaphoreType.DMA((4,)),
        ],
        compiler_params=_params(("arbitrary",), 32),
        name="combine",
    )(slots, ys, x1, wt, nw)


def _moe(parts, p, final_w):
    sizes = [x1.shape[0] for x1, _, _, _ in parts]
    idx = jnp.concatenate([q[2] for q in parts], axis=1)
    t_all = idx.shape[1]
    pos, cnt = _route(idx)
    counts = cnt[:, 0].astype(jnp.int32)
    tiles_e = (counts + GROUP_TILE - 1) // GROUP_TILE
    tile_end = jnp.cumsum(tiles_e)
    tile_start = tile_end - tiles_e
    n_used = tile_end[-1]
    n_tiles = (t_all * TOP_K) // GROUP_TILE + N_EXPERTS
    n_rows = n_tiles * GROUP_TILE
    first_row = tile_start * GROUP_TILE
    experts = jnp.arange(N_EXPERTS, dtype=jnp.int32)[:, None, None]
    base = jnp.sum(jnp.where(idx[None] == experts, first_row[:, None, None], 0), axis=0)
    slots = (base + pos).T.reshape(-1)
    tile_id = jnp.arange(n_tiles, dtype=jnp.int32)
    tile_src = jnp.minimum(tile_id, n_used - 1).astype(jnp.int32)
    tile_expert = jnp.sum(tile_src[:, None] >= tile_end[None, :], axis=1).astype(jnp.int32)
    tile_expert = jnp.minimum(tile_expert, N_EXPERTS - 1)
    ztile = jnp.where(counts > 0, tile_end - 1, -1).astype(jnp.int32)

    n_used = n_used.reshape(1).astype(jnp.int32)
    xs = _dispatch(slots, ztile, n_used, [q[1] for q in parts], n_rows)
    ys = _experts(tile_expert, tile_src, n_used, xs, p["wgu"], p["bgu"], p["wd"], p["bd"])
    outs = []
    off = 0
    for (x1, _, _, wt), t in zip(parts, sizes):
        outs.append(_combine(lax.slice(slots, (off * TOP_K,), ((off + t) * TOP_K,)), ys, x1, wt.T, final_w))
        off += t
    return outs


def _rope_tables(seq):
    half = ATTN_HEAD_DIM // 2
    inv_freq = ROPE_THETA ** (-jnp.arange(half, dtype=F32) / half)
    ang = jnp.arange(seq, dtype=F32)[:, None] * inv_freq[None, :]
    cos, sin = jnp.cos(ang), jnp.sin(ang)
    reps = LANES // ATTN_HEAD_DIM
    cos_t = jnp.tile(jnp.concatenate([cos, cos], axis=1), (1, reps))
    sin_t = jnp.tile(jnp.concatenate([-sin, sin], axis=1), (1, reps))
    return cos_t, sin_t


def _pack_layer(w_in, conv_w, conv_b, gate_b, wpa, wpm, wo, nmw, nfw, rw, rb, wgu, bgu, wd, bd):
    c0 = ATTN_QKV_COLS
    c1 = c0 + ML_QKV_COLS
    c2 = c1 + ML_O_COLS
    c3 = c2 + ML_GATE_COLS
    w_attn = w_in[:, :c0].reshape(D_MODEL, N_GROUPS, 3, ATTN_DIM)
    w_attn = w_attn * jnp.array([ATTN_HEAD_DIM ** -0.5, 1.0, 1.0], F32)[None, None, :, None]
    w_main = jnp.concatenate([w_in[:, c0:c2], w_in[:, c3:], w_attn.reshape(D_MODEL, c0)], axis=1).astype(BF16)
    w_gate = jnp.pad(w_in[:, c2:c3], ((0, 0), (0, LANES - ML_GATE_COLS))).astype(BF16)
    return dict(
        w_main=w_main, w_gate=w_gate, conv_w=conv_w, conv_b=conv_b.reshape(1, -1),
        gate_b=gate_b.reshape(-1, 1), wpa=wpa.astype(BF16), wpm=wpm.astype(BF16), wo=wo.astype(BF16),
        nmw=nmw.reshape(1, -1), nfw=nfw.reshape(1, -1), rw_t=rw.T, rb=rb.reshape(-1, 1),
        wgu=wgu.astype(BF16), bgu=bgu.reshape(N_EXPERTS, 1, -1), wd=wd.astype(BF16),
        bd=bd.reshape(N_EXPERTS, 1, -1))


def _layer(x, p, tables):
    b, s, _ = x.shape
    x2d = x.reshape(b * s, D_MODEL)
    proj, gates, qkv1, qkv2 = _inproj(x2d, p["nmw"], p["w_main"], p["w_gate"], tables[0], tables[1], s)
    proj3 = proj.reshape(b, s, N_MAIN)

    merged = _attn_plain(proj3)
    for g, qkv in ((1, qkv1), (2, qkv2)):
        merged = _attn_dilated(qkv, g, ATTN_PATTERNS[g][1], merged, b, s)
    attn_o = merged[0]

    ml_q, ml_kt = _conv_qk(proj3, p["conv_w"], p["conv_b"])
    gates_t = gates[:, :ML_GATE_COLS].reshape(b, s, ML_GATE_COLS).transpose(0, 2, 1)
    g_rows = _gate_prep(gates_t, p["gate_b"])
    g_cols = g_rows.transpose(0, 2, 1)
    h_f, h_b = _mlstm(proj3, ml_q, ml_kt, g_rows, g_cols)

    return _merge(x2d, attn_o, h_f.reshape(b * s, ML_DIM), h_b.reshape(b * s, ML_DIM), proj,
                  p["wpa"], p["wpm"], p["wo"], p["nfw"], p["rw_t"], p["rb"])


def kernel(x_prompt, x_sample, norm_mix_w, w_in, mlstm_conv_w, mlstm_conv_b, mlstm_gate_b, w_proj_attn,
           w_proj_mlstm, w_out, norm_ffn_w, router_w, router_b, expert_w_gu, expert_b_gu, expert_w_down,
           expert_b_down, norm_final_w):
    depth = w_in.shape[0]
    assert depth == 1, "the final RMSNorm is fused into the last layer's MoE kernel"
    p = _pack_layer(w_in[0], mlstm_conv_w[0], mlstm_conv_b[0], mlstm_gate_b[0], w_proj_attn[0],
                    w_proj_mlstm[0], w_out[0], norm_mix_w[0], norm_ffn_w[0], router_w[0], router_b[0],
                    expert_w_gu[0], expert_b_gu[0], expert_w_down[0], expert_b_down[0])
    final_w = norm_final_w.reshape(1, -1)
    xs = (x_prompt, x_sample)
    parts = [_layer(x, p, _rope_tables(x.shape[1])) for x in xs]
    outs = _moe(parts, p, final_w)
    return tuple(o.reshape(x.shape) for o, x in zip(outs, xs))
```

```python
import functools

import jax
import jax.numpy as jnp
from jax import lax
from jax.experimental import pallas as pl
from jax.experimental.pallas import tpu as pltpu

F32 = jnp.float32
BF16 = jnp.bfloat16

D_MODEL = 1024
ATTN_PATTERNS = ((128, 1), (512, 4), (2048, 16))
N_GROUPS = 3
ATTN_HEADS = 8
ATTN_HEAD_DIM = 64
ATTN_DIM = ATTN_HEADS * ATTN_HEAD_DIM
ATTN_BLOCK = 64
ROPE_THETA = 10000.0
ML_DIM = D_MODEL
ML_HEADS = 4
ML_HEAD_DIM = ML_DIM // ML_HEADS
ML_CHUNK = 256
CONV_W = 5
ATTN_QKV_COLS = N_GROUPS * 3 * ATTN_DIM
ML_QKV_COLS = 3 * ML_DIM
ML_O_COLS = ML_DIM
ML_GATE_COLS = 4 * ML_HEADS
MERGE_COLS = 2 * D_MODEL
N_EXPERTS = 32
TOP_K = 4
D_FF = D_MODEL
SWIGLU_LIMIT = 7.0
SWIGLU_ALPHA = 1.702
RMS_EPS = 1e-5
NEG_INF = -1e30

LANES = 128
SUBLANES = 8
COL_TILE = 512
N_PACKED = ML_QKV_COLS + ML_O_COLS + MERGE_COLS + ATTN_QKV_COLS
N_COL_TILES = N_PACKED // COL_TILE
ATTN_TILE0 = (ML_QKV_COLS + ML_O_COLS + MERGE_COLS) // COL_TILE
GROUP_TILES = 3 * ATTN_DIM // COL_TILE
N_MAIN_TILES = ATTN_TILE0 + GROUP_TILES
N_MAIN = N_MAIN_TILES * COL_TILE
ML_EXT = ML_HEAD_DIM + LANES


def _params(sem, vmem_mb):
    return pltpu.CompilerParams(dimension_semantics=sem, vmem_limit_bytes=vmem_mb * 1024 * 1024)


def _sigmoid(x):
    return 1.0 / (1.0 + jnp.exp(-x))


INPROJ_CHUNK = 512
WIDE_TILE = 3 * ATTN_DIM
N_WIDE_TILES = N_PACKED // WIDE_TILE
N_PLAIN_WIDE = ATTN_TILE0 * COL_TILE // WIDE_TILE


def _inproj_kernel(x_ref, nw_ref, w_ref, wg_ref, cos_ref, sin_ref, out_ref, gates_ref, g1_ref, g2_ref,
                   h_scr, slab, slab2):
    j = pl.program_id(1)
    tm = x_ref.shape[0]
    n_slabs = WIDE_TILE // LANES

    @pl.when(j == 0)
    def _():
        x = x_ref[...]
        ms = jnp.mean(x * x, axis=-1, keepdims=True)
        h = (x * lax.rsqrt(ms + RMS_EPS) * nw_ref[...]).astype(BF16)
        h_scr[...] = h
        gates_ref[...] = jnp.dot(h, wg_ref[...], preferred_element_type=F32)

    chunk = min(INPROJ_CHUNK, tm)

    def chunks():
        for c in range(tm // chunk):
            rows = slice(c * chunk, (c + 1) * chunk)
            yield c, rows, jnp.dot(h_scr[rows, :], w_ref[...], preferred_element_type=F32)

    def rope(a, rows):
        reps = ATTN_DIM // LANES
        c = jnp.concatenate([cos_ref[rows, :]] * reps, axis=1)
        s = jnp.concatenate([sin_ref[rows, :]] * reps, axis=1)
        lane = lax.broadcasted_iota(jnp.int32, a.shape, 1)
        half = ATTN_HEAD_DIM // 2
        first = (lane % ATTN_HEAD_DIM) < half
        sw = jnp.where(first, pltpu.roll(a, ATTN_DIM - half, 1), pltpu.roll(a, half, 1))
        return a * c + sw * s

    def rotated(acc, rows):
        return jnp.concatenate([rope(acc[:, :ATTN_DIM], rows), rope(acc[:, ATTN_DIM:2 * ATTN_DIM], rows),
                                acc[:, 2 * ATTN_DIM:]], axis=1)

    def deinterleave(val, c, dst_ref, dilation):
        n = chunk // dilation
        for s in range(n_slabs):
            slab[s] = val[:, s * LANES:(s + 1) * LANES]
        if dilation == 16:
            quarter = chunk // 4
            for s in range(n_slabs):
                for r4 in range(4):
                    slab2[s, r4 * quarter:(r4 + 1) * quarter, :] = slab[s, pl.ds(r4, quarter, stride=4), :]
            for r in range(dilation):
                start = (r % 4) * quarter + r // 4
                piece = jnp.concatenate([slab2[s, pl.ds(start, n, stride=4), :] for s in range(n_slabs)], axis=1)
                dst_ref[0, r, c * n:(c + 1) * n, :] = piece.astype(BF16)
            return
        for r in range(dilation):
            piece = jnp.concatenate([slab[s, pl.ds(r, n, stride=dilation), :] for s in range(n_slabs)], axis=1)
            dst_ref[0, r, c * n:(c + 1) * n, :] = piece.astype(BF16)

    @pl.when(j < N_PLAIN_WIDE)
    def _():
        for _, rows, acc in chunks():
            out_ref[rows, :] = acc.astype(BF16)

    dst = (out_ref, g1_ref, g2_ref)
    for g, (_, dilation) in enumerate(ATTN_PATTERNS):
        @pl.when(j == N_PLAIN_WIDE + g)
        def _(g=g, dilation=dilation):
            for c, rows, acc in chunks():
                val = rotated(acc, rows)
                if dilation == 1:
                    out_ref[rows, :] = val.astype(BF16)
                else:
                    deinterleave(val, c, dst[g], dilation)


def _inproj(x2d, nw, w_main, w_gate, cos_t, sin_t, seq):
    t = x2d.shape[0]
    tm = min(1024, seq)
    tiles_per_seq = seq // tm
    chunk = min(INPROJ_CHUNK, tm)

    def group_out(dilation):
        return (pl.BlockSpec((1, dilation, tm // dilation, WIDE_TILE), lambda i, j: (i, 0, 0, 0)),
                jax.ShapeDtypeStruct((t // tm, dilation, tm // dilation, WIDE_TILE), BF16))

    (g1_spec, g1_shape), (g2_spec, g2_shape) = [group_out(ATTN_PATTERNS[g][1]) for g in (1, 2)]
    return pl.pallas_call(
        _inproj_kernel,
        grid=(t // tm, N_WIDE_TILES),
        in_specs=[
            pl.BlockSpec((tm, D_MODEL), lambda i, j: (i, 0)),
            pl.BlockSpec((1, D_MODEL), lambda i, j: (0, 0)),
            pl.BlockSpec((D_MODEL, WIDE_TILE), lambda i, j: (0, j)),
            pl.BlockSpec((D_MODEL, LANES), lambda i, j: (0, 0)),
            pl.BlockSpec((tm, LANES), lambda i, j: (i % tiles_per_seq, 0)),
            pl.BlockSpec((tm, LANES), lambda i, j: (i % tiles_per_seq, 0)),
        ],
        out_specs=[
            pl.BlockSpec((tm, WIDE_TILE), lambda i, j: (i, jnp.minimum(j, N_PLAIN_WIDE))),
            pl.BlockSpec((tm, LANES), lambda i, j: (i, 0)),
            g1_spec, g2_spec,
        ],
        out_shape=[
            jax.ShapeDtypeStruct((t, N_MAIN), BF16),
            jax.ShapeDtypeStruct((t, LANES), F32),
            g1_shape, g2_shape,
        ],
        scratch_shapes=[pltpu.VMEM((tm, D_MODEL), BF16), pltpu.VMEM((WIDE_TILE // LANES, chunk, LANES), F32),
                        pltpu.VMEM((WIDE_TILE // LANES, chunk, LANES), F32)],
        compiler_params=_params(("arbitrary", "arbitrary"), 56),
        name="inproj",
    )(x2d, nw, w_main, w_gate, cos_t, sin_t)


ATTN_QUERY_TILE = 128


def _window_mask(n, qt, sub_len):
    kt = qt + 2 * ATTN_BLOCK
    qpos = n * qt + lax.broadcasted_iota(jnp.int32, (qt, kt), 0)
    kpos = n * qt - ATTN_BLOCK + lax.broadcasted_iota(jnp.int32, (qt, kt), 1)
    return (jnp.abs(kpos - qpos) <= ATTN_BLOCK) & (kpos >= 0) & (kpos < sub_len)


def _attend(q, k, v, valid):
    qt = q.shape[0]
    pairs = range(ATTN_HEADS // 2)
    lane = lax.broadcasted_iota(jnp.int32, (qt, LANES), 1)
    lo = lane < ATTN_HEAD_DIM
    valid2 = jnp.concatenate([valid, valid], axis=0)
    scores = []
    for p in pairs:
        qp = q[:, p * LANES:(p + 1) * LANES]
        zero = jnp.zeros_like(qp)
        stacked = jnp.concatenate([jnp.where(lo, qp, zero), jnp.where(lo, zero, qp)], axis=0)
        scores.append(lax.dot_general(stacked, k[:, p * LANES:(p + 1) * LANES], (((1,), (1,)), ((), ())),
                                      preferred_element_type=F32))
    probs, inv_den, lses = [], [], []
    for p in pairs:
        s = jnp.where(valid2, scores[p], NEG_INF)
        mx = jnp.max(s, axis=-1, keepdims=True)
        e = jnp.exp(s - mx)
        den = jnp.sum(e, axis=-1, keepdims=True)
        probs.append(e.astype(BF16))
        inv_den.append(1.0 / den)
        lses.append(mx + jnp.log(den))
    pvs = [jnp.dot(probs[p], v[:, p * LANES:(p + 1) * LANES], preferred_element_type=F32) for p in pairs]
    lse_out = jnp.zeros((qt, LANES), F32)
    outs = []
    for p in pairs:
        o = pvs[p] * inv_den[p]
        outs.append(jnp.where(lo, o[:qt], o[qt:]))
        lse_out = jnp.where(lane == 2 * p, lses[p][:qt], jnp.where(lane == 2 * p + 1, lses[p][qt:], lse_out))
    return outs, lse_out


def _head_expander():
    row = lax.broadcasted_iota(jnp.int32, (LANES, ATTN_DIM), 0)
    col = lax.broadcasted_iota(jnp.int32, (LANES, ATTN_DIM), 1)
    return jnp.where(col // ATTN_HEAD_DIM == row, 1.0, 0.0).astype(BF16)


def _merge_groups(o_cur, lse_cur, o_prev, lse_prev, expander):
    m = jnp.maximum(lse_prev, lse_cur)
    a = jnp.exp(lse_prev - m)
    tot = a + jnp.exp(lse_cur - m)
    w_prev = a / tot
    hi = w_prev.astype(BF16)
    lo = (w_prev - hi.astype(F32)).astype(BF16)
    spread = (jnp.dot(hi, expander, preferred_element_type=F32) + jnp.dot(lo, expander, preferred_element_type=F32))
    outs = [o_cur[p] + spread[:, p * LANES:(p + 1) * LANES] * (o_prev[p] - o_cur[p])
            for p in range(ATTN_HEADS // 2)]
    lane = lax.broadcasted_iota(jnp.int32, m.shape, 1)
    return outs, jnp.where(lane < ATTN_HEADS, m + jnp.log(tot), 0.0)


def _attn_kernel(q_ref, kp_ref, kc_ref, kn_ref, vp_ref, vc_ref, vn_ref, o_ref, l_ref, *, sub_len, qt):
    n_sub = q_ref.shape[1] // qt
    k = jnp.concatenate([kp_ref[0], kc_ref[0], kn_ref[0]], axis=0)
    v = jnp.concatenate([vp_ref[0], vc_ref[0], vn_ref[0]], axis=0)
    for sub in range(n_sub):
        valid = _window_mask(pl.program_id(1) * n_sub + sub, qt, sub_len)
        rows = slice(sub * qt, (sub + 1) * qt)
        keys = slice(sub * qt, (sub + 1) * qt + 2 * ATTN_BLOCK)
        outs, lse = _attend(q_ref[0, rows, :], k[keys], v[keys], valid)
        for p, o in enumerate(outs):
            o_ref[0, rows, p * LANES:(p + 1) * LANES] = o.astype(BF16)
        l_ref[0, rows, :] = lse


MERGE_ROWS = 256
RESIDUE_ROWS_PER_ITER = 512


def _attn_dilated_kernel(q_ref, kp_ref, kc_ref, kn_ref, vp_ref, vc_ref, vn_ref, op_ref, lp_ref, o_ref, l_ref,
                         o_slab, l_slab, *, sub_len, qt, dilation):
    valid = _window_mask(pl.program_id(1), qt, sub_len)

    per_iter = min(dilation, RESIDUE_ROWS_PER_ITER // qt)

    def residues(i, c):
        for r in [per_iter * i + u for u in range(per_iter)]:
            k = jnp.concatenate([kp_ref[0, r], kc_ref[0, r], kn_ref[0, r]], axis=0)
            v = jnp.concatenate([vp_ref[0, r], vc_ref[0, r], vn_ref[0, r]], axis=0)
            outs, lse = _attend(q_ref[0, r], k, v, valid)
            for p, o in enumerate(outs):
                o_slab[p, pl.ds(r, qt, stride=dilation), :] = o
            l_slab[pl.ds(r, qt, stride=dilation), :] = lse
        return c

    lax.fori_loop(0, dilation // per_iter, residues, 0)
    expander = _head_expander()

    def merge(i, c):
        rows = pl.ds(pl.multiple_of(i * MERGE_ROWS, MERGE_ROWS), MERGE_ROWS)
        prev = op_ref[rows, :].astype(F32)
        o_cur = [o_slab[p, rows, :] for p in range(ATTN_HEADS // 2)]
        o_prev = [prev[:, p * LANES:(p + 1) * LANES] for p in range(ATTN_HEADS // 2)]
        outs, lse = _merge_groups(o_cur, l_slab[rows, :], o_prev, lp_ref[rows, :], expander)
        for p, o in enumerate(outs):
            o_ref[rows, p * LANES:(p + 1) * LANES] = o.astype(BF16)
        l_ref[rows, :] = lse
        return c

    lax.fori_loop(0, qt * dilation // MERGE_ROWS, merge, 0)


PLAIN_STEP_TILES = 2


def _attn_plain(proj3):
    b, s, _ = proj3.shape
    qt = min(ATTN_QUERY_TILE, s)
    step = min(PLAIN_STEP_TILES * qt, s)
    nblk = s // ATTN_BLOCK
    qb = step // ATTN_BLOCK

    def cur(which):
        return pl.BlockSpec((1, step, COL_TILE), lambda bi, n: (bi, n, ATTN_TILE0 + which))

    def before(which):
        return pl.BlockSpec((1, ATTN_BLOCK, COL_TILE),
                            lambda bi, n: (bi, jnp.maximum(n * qb - 1, 0), ATTN_TILE0 + which))

    def after(which):
        return pl.BlockSpec((1, ATTN_BLOCK, COL_TILE),
                            lambda bi, n: (bi, jnp.minimum((n + 1) * qb, nblk - 1), ATTN_TILE0 + which))

    o, l = pl.pallas_call(
        functools.partial(_attn_kernel, sub_len=s, qt=qt),
        grid=(b, s // step),
        in_specs=[cur(0), before(1), cur(1), after(1), before(2), cur(2), after(2)],
        out_specs=[pl.BlockSpec((1, step, ATTN_DIM), lambda bi, n: (bi, n, 0)),
                   pl.BlockSpec((1, step, LANES), lambda bi, n: (bi, n, 0))],
        out_shape=[jax.ShapeDtypeStruct((b, s, ATTN_DIM), BF16), jax.ShapeDtypeStruct((b, s, LANES), F32)],
        compiler_params=_params(("arbitrary", "arbitrary"), 48),
        name="attn_g0",
    )(*([proj3] * 7))
    return o.reshape(b * s, ATTN_DIM), l.reshape(b * s, LANES)


def _attn_dilated(qkv, g, dilation, prev, b, s):
    n_tiles, _, rows, _ = qkv.shape
    tiles_per_seq = n_tiles // b
    sub_len = s // dilation
    qt = min(ATTN_QUERY_TILE, rows)
    span = qt * dilation
    q_per_tile = rows // qt
    h_per_tile = rows // ATTN_BLOCK
    qb = qt // ATTN_BLOCK
    nblk = sub_len // ATTN_BLOCK

    def cur(which):
        return pl.BlockSpec((1, dilation, qt, COL_TILE),
                            lambda bi, n: (bi * tiles_per_seq + n // q_per_tile, 0, n % q_per_tile, which))

    def halo(which, blk_of):
        def index(bi, n):
            blk = blk_of(n)
            return (bi * tiles_per_seq + blk // h_per_tile, 0, blk % h_per_tile, which)
        return pl.BlockSpec((1, dilation, ATTN_BLOCK, COL_TILE), index)

    before = lambda which: halo(which, lambda n: jnp.maximum(n * qb - 1, 0))
    after = lambda which: halo(which, lambda n: jnp.minimum((n + 1) * qb, nblk - 1))
    steps = sub_len // qt
    o_spec = pl.BlockSpec((span, ATTN_DIM), lambda bi, n: (bi * steps + n, 0))
    l_spec = pl.BlockSpec((span, LANES), lambda bi, n: (bi * steps + n, 0))
    return pl.pallas_call(
        functools.partial(_attn_dilated_kernel, sub_len=sub_len, qt=qt, dilation=dilation),
        grid=(b, steps),
        in_specs=[cur(0), before(1), cur(1), after(1), before(2), cur(2), after(2), o_spec, l_spec],
        out_specs=[o_spec, l_spec],
        out_shape=[jax.ShapeDtypeStruct((b * s, ATTN_DIM), BF16), jax.ShapeDtypeStruct((b * s, LANES), F32)],
        scratch_shapes=[pltpu.VMEM((ATTN_HEADS // 2, span, LANES), F32), pltpu.VMEM((span, LANES), F32)],
        compiler_params=_params(("arbitrary", "arbitrary"), 56),
        name=f"attn_g{g}",
    )(*([qkv] * 7), prev[0], prev[1])


CONV_HALO = 16


Q_TILES = ML_DIM // COL_TILE


def _conv_kernel(xp_ref, xc_ref, xn_ref, w_ref, b_ref, q_ref, kt_ref, buf, *, tm):
    i = pl.program_id(1)
    c = pl.program_id(2)
    last = pl.num_programs(1) - 1
    before = jnp.where(i > 0, xp_ref[0].astype(F32), 0.0)
    centre = xc_ref[0].astype(F32)
    after = jnp.where(i < last, xn_ref[0].astype(F32), 0.0)
    w = w_ref[...]
    bias = b_ref[...]
    pieces = []
    for s in range(COL_TILE // LANES):
        lanes = slice(s * LANES, (s + 1) * LANES)
        buf[s, 0:CONV_HALO, :] = before[:, lanes]
        buf[s, CONV_HALO:CONV_HALO + tm, :] = centre[:, lanes]
        buf[s, CONV_HALO + tm:, :] = after[:, lanes]
        acc = jnp.broadcast_to(bias[:, lanes], (tm, LANES))
        for tap in range(CONV_W):
            acc = acc + w[tap:tap + 1, lanes] * buf[s, pl.ds(CONV_HALO - CONV_W // 2 + tap, tm, stride=1), :]
        pieces.append(acc)
    y = jnp.concatenate(pieces, axis=1)
    y = y * _sigmoid(y)

    @pl.when(c < Q_TILES)
    def _():
        q_ref[0] = y.astype(BF16)

    @pl.when(c >= Q_TILES)
    def _():
        kt_ref[0] = (y * ML_HEAD_DIM ** -0.5).T.astype(BF16)


def _conv_qk(proj3, conv_w, conv_b):
    b, s, _ = proj3.shape
    tm = min(1024, s)
    hb = tm // CONV_HALO
    nh = s // CONV_HALO
    return pl.pallas_call(
        functools.partial(_conv_kernel, tm=tm),
        grid=(b, s // tm, 2 * Q_TILES),
        in_specs=[
            pl.BlockSpec((1, CONV_HALO, COL_TILE), lambda bi, i, c: (bi, jnp.maximum(i * hb - 1, 0), c)),
            pl.BlockSpec((1, tm, COL_TILE), lambda bi, i, c: (bi, i, c)),
            pl.BlockSpec((1, CONV_HALO, COL_TILE), lambda bi, i, c: (bi, jnp.minimum((i + 1) * hb, nh - 1), c)),
            pl.BlockSpec((CONV_W, COL_TILE), lambda bi, i, c: (0, c)),
            pl.BlockSpec((1, COL_TILE), lambda bi, i, c: (0, c)),
        ],
        out_specs=[
            pl.BlockSpec((1, tm, COL_TILE), lambda bi, i, c: (bi, i, jnp.minimum(c, Q_TILES - 1))),
            pl.BlockSpec((1, COL_TILE, tm), lambda bi, i, c: (bi, jnp.maximum(c - Q_TILES, 0), i)),
        ],
        out_shape=[jax.ShapeDtypeStruct((b, s, ML_DIM), BF16), jax.ShapeDtypeStruct((b, ML_DIM, s), BF16)],
        scratch_shapes=[pltpu.VMEM((COL_TILE // LANES, tm + 2 * CONV_HALO, LANES), F32)],
        compiler_params=_params(("arbitrary", "arbitrary", "arbitrary"), 32),
        name="conv_qk",
    )(proj3, proj3, proj3, conv_w, conv_b)


def _gate_kernel(g_ref, b_ref, o_ref):
    g = g_ref[0] + b_ref[...]
    width = g.shape[1]
    logsig = jnp.minimum(g, 0.0) - jnp.log(1.0 + jnp.exp(-jnp.abs(g)))
    lane = lax.broadcasted_iota(jnp.int32, g.shape, 1) % ML_CHUNK
    pre = logsig
    suf = logsig
    step = 1
    while step < ML_CHUNK:
        pre = pre + jnp.where(lane >= step, pltpu.roll(pre, step, 1), 0.0)
        suf = suf + jnp.where(lane < ML_CHUNK - step, pltpu.roll(suf, width - step, 1), 0.0)
        step *= 2
    row = lax.broadcasted_iota(jnp.int32, g.shape, 0)
    is_f_fwd = (row >= ML_HEADS) & (row < 2 * ML_HEADS)
    is_f_bwd = row >= 3 * ML_HEADS
    o_ref[0] = jnp.where(is_f_fwd, pre, jnp.where(is_f_bwd, suf, g))


def _gate_prep(gates_t, gate_b):
    b, rows, s = gates_t.shape
    sb = min(2048, s)
    return pl.pallas_call(
        _gate_kernel,
        grid=(b, s // sb),
        in_specs=[
            pl.BlockSpec((1, rows, sb), lambda bi, i: (bi, 0, i)),
            pl.BlockSpec((rows, 1), lambda bi, i: (0, 0)),
        ],
        out_specs=pl.BlockSpec((1, rows, sb), lambda bi, i: (bi, 0, i)),
        out_shape=jax.ShapeDtypeStruct((b, rows, s), F32),
        compiler_params=_params(("arbitrary", "arbitrary"), 32),
        name="gate_prep",
    )(gates_t, gate_b)


ML_CHAIN_GROUP = 8


def _mlstm_kernel(qf_ref, kf_ref, vf_ref, qb_ref, kb_ref, vb_ref, grf_ref, grb_ref, gcf_ref, gcb_ref,
                  hf_ref, hb_ref, s_scr, m_scr):
    c = pl.program_id(1)

    @pl.when(c == 0)
    def _():
        def clear(i, carry):
            s_scr[i] = jnp.zeros(s_scr.shape[1:], F32)
            return carry

        lax.fori_loop(0, s_scr.shape[0], clear, 0)
        m_scr[...] = jnp.zeros_like(m_scr)

    t_i = lax.broadcasted_iota(jnp.int32, (ML_CHUNK, ML_CHUNK), 0)
    s_i = lax.broadcasted_iota(jnp.int32, (ML_CHUNK, ML_CHUNK), 1)
    ones_col = jnp.where(lax.broadcasted_iota(jnp.int32, (ML_CHUNK, LANES), 1) == 0, 1.0, 0.0).astype(BF16)
    dirs = ((qf_ref, kf_ref, vf_ref, grf_ref, gcf_ref, hf_ref), (qb_ref, kb_ref, vb_ref, grb_ref, gcb_ref, hb_ref))
    all_chains = [(dirn, head) + refs for dirn, refs in enumerate(dirs) for head in range(ML_HEADS)]

    for first in range(0, len(all_chains), ML_CHAIN_GROUP):
        chains = all_chains[first:first + ML_CHAIN_GROUP]
        matmuls = []
        for dirn, head, q_ref, k_ref, v_ref, _, _, _ in chains:
            hs = slice(head * ML_HEAD_DIM, (head + 1) * ML_HEAD_DIM)
            q = q_ref[0, :, hs]
            qk_raw = jnp.dot(q, k_ref[0, hs, :], preferred_element_type=F32)
            q_state = jnp.dot(q, s_scr[dirn * ML_HEADS + head].astype(BF16), preferred_element_type=F32)
            matmuls.append((qk_raw, q_state))

        weights = []
        for dirn, head, _, _, _, gr_ref, gc_ref, _ in chains:
            ii = dirn * 2 * ML_HEADS + head
            bi = ii + ML_HEADS
            mask = (s_i <= t_i) if dirn == 0 else (s_i >= t_i)
            i_row, b_row = gr_ref[0, ii:ii + 1, :], gr_ref[0, bi:bi + 1, :]
            b_col = gc_ref[0, :, bi:bi + 1]
            m = m_scr[dirn * ML_HEADS + head][0:1, 0:1]
            dmat = jnp.where(mask, b_col - b_row + i_row, NEG_INF)
            inter = b_col + m
            m_t = jnp.maximum(inter, jnp.max(dmat, axis=-1, keepdims=True))
            weights.append((jnp.exp(dmat - m_t), jnp.exp(inter - m_t), jnp.exp(-m_t)))

        for (dirn, head, _, _, v_ref, _, _, h_ref), (qk_raw, q_state), (w_intra, w_inter, floor) in zip(
                chains, matmuls, weights):
            hs = slice(head * ML_HEAD_DIM, (head + 1) * ML_HEAD_DIM)
            v_ext = jnp.concatenate([v_ref[0, :, hs], ones_col], axis=1)
            num = w_inter * q_state + jnp.dot((qk_raw * w_intra).astype(BF16), v_ext, preferred_element_type=F32)
            den = jnp.maximum(jnp.abs(num[:, ML_HEAD_DIM:ML_HEAD_DIM + 1]), floor)
            h_ref[0, :, hs] = (num[:, :ML_HEAD_DIM] / den).astype(BF16)

        for dirn, head, _, k_ref, v_ref, gr_ref, _, _ in chains:
            idx = dirn * ML_HEADS + head
            ii = dirn * 2 * ML_HEADS + head
            bi = ii + ML_HEADS
            hs = slice(head * ML_HEAD_DIM, (head + 1) * ML_HEAD_DIM)
            i_row, b_row = gr_ref[0, ii:ii + 1, :], gr_ref[0, bi:bi + 1, :]
            b_last = b_row[:, ML_CHUNK - 1:ML_CHUNK] if dirn == 0 else b_row[:, 0:1]
            m = m_scr[idx][0:1, 0:1]
            log_w = b_last - b_row + i_row
            m_new = jnp.maximum(b_last + m, jnp.max(log_w, axis=-1, keepdims=True))
            decay = jnp.exp(b_last + m - m_new)
            v_ext = jnp.concatenate([v_ref[0, :, hs], ones_col], axis=1)
            wk_t = (k_ref[0, hs, :].astype(F32) * jnp.exp(log_w - m_new)).astype(BF16)
            s_scr[idx] = decay * s_scr[idx] + jnp.dot(wk_t, v_ext, preferred_element_type=F32)
            m_scr[idx] = jnp.broadcast_to(m_new, m_scr.shape[1:])


def _mlstm(proj3, q, k_t, g_rows, g_cols):
    b, s, _ = proj3.shape
    nc = s // ML_CHUNK
    v_tile = 2 * ML_DIM // ML_DIM
    n_rows = g_rows.shape[1]

    def rows(col, reverse):
        return pl.BlockSpec((1, ML_CHUNK, ML_DIM), lambda bi, c: (bi, nc - 1 - c if reverse else c, col))

    def cols(height, reverse):
        return pl.BlockSpec((1, height, ML_CHUNK), lambda bi, c: (bi, 0, nc - 1 - c if reverse else c))

    return pl.pallas_call(
        _mlstm_kernel,
        grid=(b, nc),
        in_specs=[
            rows(0, False), cols(ML_DIM, False), rows(v_tile, False),
            rows(0, True), cols(ML_DIM, True), rows(v_tile, True),
            cols(n_rows, False), cols(n_rows, True),
            pl.BlockSpec((1, ML_CHUNK, n_rows), lambda bi, c: (bi, c, 0)),
            pl.BlockSpec((1, ML_CHUNK, n_rows), lambda bi, c: (bi, nc - 1 - c, 0)),
        ],
        out_specs=[rows(0, False), rows(0, True)],
        out_shape=[jax.ShapeDtypeStruct((b, s, ML_DIM), BF16)] * 2,
        scratch_shapes=[
            pltpu.VMEM((2 * ML_HEADS, ML_HEAD_DIM, ML_EXT), F32),
            pltpu.VMEM((2 * ML_HEADS, 8, LANES), F32),
        ],
        compiler_params=_params(("arbitrary", "arbitrary"), 32),
        name="mlstm",
    )(q, k_t, proj3, q, k_t, proj3, g_rows, g_rows, g_cols, g_cols)


def _pack_bf16_pair(lo, hi):
    lo_bits = lax.bitcast_convert_type(lo.astype(BF16).astype(F32), jnp.uint32)
    hi_bits = lax.bitcast_convert_type(hi.astype(BF16).astype(F32), jnp.uint32)
    return (hi_bits & jnp.uint32(0xFFFF0000)) | (lo_bits >> 16)


def _unpack_bf16_pair(packed):
    lo = lax.bitcast_convert_type(packed << 16, F32)
    hi = lax.bitcast_convert_type(packed & jnp.uint32(0xFFFF0000), F32)
    return lo, hi


MERGE_CHUNK = 512


def _merge_kernel(x_ref, ao_ref, hf_ref, hb_ref, mo_ref, mg_ref, wpa_ref, wpm_ref, wo_ref, nfw_ref,
                  rw_ref, rb_ref, x1_ref, h2_ref, idx_ref, wt_ref):
    tm = x_ref.shape[0]
    chunk = min(MERGE_CHUNK, tm)
    for c in range(tm // chunk):
        rows = slice(c * chunk, (c + 1) * chunk)
        y_attn = jnp.dot(ao_ref[rows, :], wpa_ref[...], preferred_element_type=F32)
        hsum = hf_ref[rows, :].astype(F32) + hb_ref[rows, :].astype(F32)
        ml = (_sigmoid(mo_ref[rows, :].astype(F32)) * hsum).astype(BF16)
        y_ml = jnp.dot(ml, wpm_ref[...], preferred_element_type=F32)
        gates = _sigmoid(mg_ref[rows, :].astype(F32))
        mixed = (gates[:, :D_MODEL] * y_attn + gates[:, D_MODEL:] * y_ml).astype(BF16)
        x1 = x_ref[rows, :] + jnp.dot(mixed, wo_ref[...], preferred_element_type=F32)
        x1_ref[rows, :] = x1
        ms = jnp.mean(x1 * x1, axis=-1, keepdims=True)
        h2 = x1 * lax.rsqrt(ms + RMS_EPS) * nfw_ref[...]
        h2_ref[rows, :] = _pack_bf16_pair(h2[:, :D_MODEL // 2], h2[:, D_MODEL // 2:])

        logits = lax.dot_general(rw_ref[...], h2, (((1,), (1,)), ((), ())), preferred_element_type=F32,
                                 precision=lax.Precision.HIGHEST) + rb_ref[...]
        row = lax.broadcasted_iota(jnp.int32, logits.shape, 0)
        rest = logits
        vals = []
        for k in range(TOP_K):
            mx = jnp.max(rest, axis=0, keepdims=True)
            first = jnp.min(jnp.where(rest == mx, row, N_EXPERTS), axis=0, keepdims=True)
            vals.append(mx)
            idx_ref[k:k + 1, rows] = first
            rest = jnp.where(row == first, -jnp.inf, rest)
        exps = [jnp.exp(v - vals[0]) for v in vals]
        tot = exps[0] + exps[1] + exps[2] + exps[3]
        for k in range(TOP_K):
            wt_ref[k:k + 1, rows] = exps[k] / tot


def _merge(x2d, attn_o, h_f, h_b, proj, wpa, wpm, wo, nfw, rw_t, rb):
    t = x2d.shape[0]
    tm = min(512, t)
    ml_o_tile = ML_QKV_COLS // ML_O_COLS
    merge_tile = (ML_QKV_COLS + ML_O_COLS) // MERGE_COLS
    row = lambda width: pl.BlockSpec((tm, width), lambda i: (i, 0))
    full = lambda a: pl.BlockSpec(a.shape, lambda i: (0, 0))
    return pl.pallas_call(
        _merge_kernel,
        grid=(t // tm,),
        in_specs=[
            row(D_MODEL), row(ATTN_DIM), row(ML_DIM), row(ML_DIM),
            pl.BlockSpec((tm, ML_O_COLS), lambda i: (i, ml_o_tile)),
            pl.BlockSpec((tm, MERGE_COLS), lambda i: (i, merge_tile)),
            full(wpa), full(wpm), full(wo), full(nfw), full(rw_t), full(rb),
        ],
        out_specs=[row(D_MODEL), row(D_MODEL // 2), pl.BlockSpec((TOP_K, tm), lambda i: (0, i)),
                   pl.BlockSpec((TOP_K, tm), lambda i: (0, i))],
        out_shape=[
            jax.ShapeDtypeStruct((t, D_MODEL), F32),
            jax.ShapeDtypeStruct((t, D_MODEL // 2), jnp.uint32),
            jax.ShapeDtypeStruct((TOP_K, t), jnp.int32),
            jax.ShapeDtypeStruct((TOP_K, t), F32),
        ],
        compiler_params=_params(("arbitrary",), 48),
        name="merge",
    )(x2d, attn_o, h_f, h_b, proj, proj, wpa, wpm, wo, nfw, rw_t, rb)


ROUTE_TILE = 512


def _route_kernel(idx_ref, pos_ref, cnt_ref, base):
    @pl.when(pl.program_id(0) == 0)
    def _():
        base[...] = jnp.zeros_like(base)

    idx = idx_ref[...]
    row = lax.broadcasted_iota(jnp.int32, (N_EXPERTS, ROUTE_TILE), 0)
    onehot = jnp.zeros((N_EXPERTS, ROUTE_TILE), F32)
    for k in range(TOP_K):
        onehot = onehot + jnp.where(row == idx[k:k + 1, :], 1.0, 0.0)
    s_i = lax.broadcasted_iota(jnp.int32, (ROUTE_TILE, ROUTE_TILE), 0)
    t_i = lax.broadcasted_iota(jnp.int32, (ROUTE_TILE, ROUTE_TILE), 1)
    upper = jnp.where(s_i <= t_i, 1.0, 0.0).astype(BF16)
    incl = jnp.dot(onehot.astype(BF16), upper, preferred_element_type=F32)
    before = base[:, 0:1]
    count = incl + before
    for k in range(TOP_K):
        mine = jnp.sum(jnp.where(row == idx[k:k + 1, :], count, 0.0), axis=0, keepdims=True)
        pos_ref[k:k + 1, :] = (mine - 1.0).astype(jnp.int32)
    total = before + incl[:, ROUTE_TILE - 1:ROUTE_TILE]
    base[...] = jnp.broadcast_to(total, base.shape)
    cnt_ref[...] = jnp.broadcast_to(total, cnt_ref.shape)


def _route(idx):
    t = idx.shape[1]
    return pl.pallas_call(
        _route_kernel,
        grid=(t // ROUTE_TILE,),
        in_specs=[pl.BlockSpec((TOP_K, ROUTE_TILE), lambda i: (0, i))],
        out_specs=[pl.BlockSpec((TOP_K, ROUTE_TILE), lambda i: (0, i)),
                   pl.BlockSpec((N_EXPERTS, LANES), lambda i: (0, 0))],
        out_shape=[jax.ShapeDtypeStruct((TOP_K, t), jnp.int32),
                   jax.ShapeDtypeStruct((N_EXPERTS, LANES), F32)],
        scratch_shapes=[pltpu.VMEM((N_EXPERTS, LANES), F32)],
        compiler_params=_params(("arbitrary",), 32),
        name="route",
    )(idx)


GROUP_TILE = 512
DISPATCH_TILE = 2048
ROW_WORDS = D_MODEL // 2


def _dispatch_kernel(slot_hbm, ztile_ref, nu_ref, *refs, steps):
    srcs = refs[:len(steps)]
    dst_hbm, slot_smem, zbuf, sem = refs[len(steps):]
    i = pl.program_id(0)
    n_slots = DISPATCH_TILE * TOP_K
    n_tiles = dst_hbm.shape[0] // GROUP_TILE
    load = pltpu.make_async_copy(slot_hbm.at[pl.ds(i * n_slots, n_slots)], slot_smem, sem.at[0])
    load.start()

    @pl.when(i == 0)
    def _():
        zbuf[...] = jnp.zeros_like(zbuf)

        def fill_tile(tile):
            pltpu.make_async_copy(zbuf, dst_hbm.at[pl.ds(tile * GROUP_TILE, GROUP_TILE), :], sem.at[1]).start()

        def fill_last(e, n):
            tile = ztile_ref[e]

            @pl.when(tile >= 0)
            def _():
                fill_tile(tile)

            return n + jnp.where(tile >= 0, 1, 0)

        def fill_unused(tile, c):
            fill_tile(tile)
            return c

        n_fill = lax.fori_loop(0, N_EXPERTS, fill_last, 0)
        lax.fori_loop(nu_ref[0], n_tiles, fill_unused, 0)

        def drain(_, c):
            pltpu.make_async_copy(zbuf, dst_hbm.at[pl.ds(0, GROUP_TILE), :], sem.at[1]).wait()
            return c

        lax.fori_loop(0, n_fill + n_tiles - nu_ref[0], drain, 0)

    load.wait()

    first = 0
    for src_ref, n_steps in zip(srcs, steps):
        @pl.when((i >= first) & (i < first + n_steps))
        def _(src_ref=src_ref):
            def issue(t8, c):
                for r in range(SUBLANES):
                    src = src_ref.at[t8, pl.ds(r, 1), :]
                    for k in range(TOP_K):
                        slot = slot_smem[(t8 * SUBLANES + r) * TOP_K + k]
                        pltpu.make_async_copy(src, dst_hbm.at[pl.ds(slot, 1), :],
                                              sem.at[1]).start(priority=k % 2)
                return c

            lax.fori_loop(0, DISPATCH_TILE // SUBLANES, issue, 0)

        first += n_steps
    pltpu.make_async_copy(dst_hbm.at[pl.ds(0, n_slots), :], dst_hbm.at[pl.ds(0, n_slots), :], sem.at[1]).wait()


def _dispatch(slots, ztile, n_used, sources, n_rows):
    steps = tuple(h.shape[0] // DISPATCH_TILE for h in sources)
    any_spec = pl.BlockSpec(memory_space=pl.ANY)
    smem_spec = pl.BlockSpec(memory_space=pltpu.SMEM)
    src_specs = []
    first = 0
    for n_steps in steps:
        src_specs.append(pl.BlockSpec(
            (DISPATCH_TILE // SUBLANES, SUBLANES, ROW_WORDS),
            lambda i, first=first, n_steps=n_steps: (jnp.clip(i - first, 0, n_steps - 1), 0, 0)))
        first += n_steps
    sources = [h.reshape(h.shape[0] // SUBLANES, SUBLANES, ROW_WORDS) for h in sources]
    return pl.pallas_call(
        functools.partial(_dispatch_kernel, steps=steps),
        grid=(sum(steps),),
        in_specs=[any_spec, smem_spec, smem_spec] + src_specs,
        out_specs=any_spec,
        out_shape=jax.ShapeDtypeStruct((n_rows, ROW_WORDS), jnp.uint32),
        scratch_shapes=[
            pltpu.SMEM((DISPATCH_TILE * TOP_K,), jnp.int32),
            pltpu.VMEM((GROUP_TILE, ROW_WORDS), jnp.uint32),
            pltpu.SemaphoreType.DMA((2,)),
        ],
        compiler_params=_params(("arbitrary",), 32),
        name="dispatch",
    )(slots, ztile, n_used, *sources)


EXPERT_CHUNK = 512


def _expert_kernel(te_ref, ts_ref, nu_ref, x_ref, wgu_ref, bgu_ref, wd_ref, bd_ref, y_ref):
    j = pl.program_id(0)

    @pl.when(j < nu_ref[0])
    def _():
        half = D_MODEL // 2
        lo, hi = _unpack_bf16_pair(x_ref[...])
        lo, hi = lo.astype(BF16), hi.astype(BF16)
        y = jnp.broadcast_to(bd_ref[0], (GROUP_TILE, D_MODEL))
        for c in range(D_FF // EXPERT_CHUNK):
            gcols = slice(c * EXPERT_CHUNK, (c + 1) * EXPERT_CHUNK)
            ucols = slice(D_FF + c * EXPERT_CHUNK, D_FF + (c + 1) * EXPERT_CHUNK)
            gate = (jnp.dot(lo, wgu_ref[0, :half, gcols], preferred_element_type=F32)
                    + jnp.dot(hi, wgu_ref[0, half:, gcols], preferred_element_type=F32) + bgu_ref[0, :, gcols])
            up = (jnp.dot(lo, wgu_ref[0, :half, ucols], preferred_element_type=F32)
                  + jnp.dot(hi, wgu_ref[0, half:, ucols], preferred_element_type=F32) + bgu_ref[0, :, ucols])
            gate = jnp.minimum(gate, SWIGLU_LIMIT)
            up = jnp.clip(up, -SWIGLU_LIMIT, SWIGLU_LIMIT)
            hid = (up + 1.0) * gate * _sigmoid(SWIGLU_ALPHA * gate)
            y = y + jnp.dot(hid.astype(BF16), wd_ref[0, gcols, :], preferred_element_type=F32)
        y_ref[...] = _pack_bf16_pair(y[:, :half], y[:, half:])

    @pl.when(j >= nu_ref[0])
    def _():
        y_ref[...] = jnp.zeros_like(y_ref)


def _experts(tile_expert, tile_src, n_used, xs, wgu, bgu, wd, bd):
    n_rows = xs.shape[0]
    row_spec = pl.BlockSpec((GROUP_TILE, ROW_WORDS), lambda j, te, ts, nu: (ts[j], 0))
    out_spec = pl.BlockSpec((GROUP_TILE, ROW_WORDS), lambda j, te, ts, nu: (j, 0))
    return pl.pallas_call(
        _expert_kernel,
        grid_spec=pltpu.PrefetchScalarGridSpec(
            num_scalar_prefetch=3,
            grid=(n_rows // GROUP_TILE,),
            in_specs=[
                row_spec,
                pl.BlockSpec((1, D_MODEL, 2 * D_FF), lambda j, te, ts, nu: (te[j], 0, 0)),
                pl.BlockSpec((1, 1, 2 * D_FF), lambda j, te, ts, nu: (te[j], 0, 0)),
                pl.BlockSpec((1, D_FF, D_MODEL), lambda j, te, ts, nu: (te[j], 0, 0)),
                pl.BlockSpec((1, 1, D_MODEL), lambda j, te, ts, nu: (te[j], 0, 0)),
            ],
            out_specs=out_spec,
        ),
        out_shape=jax.ShapeDtypeStruct((n_rows, ROW_WORDS), jnp.uint32),
        compiler_params=_params(("arbitrary",), 56),
        name="experts",
    )(tile_expert, tile_src, n_used, xs, wgu, bgu, wd, bd)


COMBINE_TILE = 512


def _combine_kernel(slot_hbm, ys_hbm, x1_ref, wt_ref, nw_ref, o_ref, slot_a, slot_b, buf_a, buf_b, sem):
    i = pl.program_id(0)
    n = pl.num_programs(0)
    n_slots = COMBINE_TILE * TOP_K
    slot_bufs = (slot_a, slot_b)
    row_bufs = (buf_a, buf_b)

    def slot_load(tile, par):
        return pltpu.make_async_copy(slot_hbm.at[pl.ds(tile * n_slots, n_slots)], slot_bufs[par], sem.at[par])

    def issue_rows(par):
        def issue(t8, c):
            for r in range(SUBLANES):
                for k in range(TOP_K):
                    slot = slot_bufs[par][(t8 * SUBLANES + r) * TOP_K + k]
                    pltpu.make_async_copy(ys_hbm.at[pl.ds(slot, 1), :], row_bufs[par].at[k, t8, pl.ds(r, 1), :],
                                          sem.at[2 + par]).start(priority=k % 2)
            return c

        lax.fori_loop(0, COMBINE_TILE // SUBLANES, issue, 0)

    def reduce_rows(par):
        pltpu.make_async_copy(ys_hbm.at[pl.ds(0, n_slots), :], ys_hbm.at[pl.ds(0, n_slots), :],
                              sem.at[2 + par]).wait()
        half = D_MODEL // 2
        wt = wt_ref[...]
        acc_lo = x1_ref[:, :half]
        acc_hi = x1_ref[:, half:]
        for k in range(TOP_K):
            lo, hi = _unpack_bf16_pair(row_bufs[par][k].reshape(COMBINE_TILE, ROW_WORDS))
            acc_lo = acc_lo + wt[:, k:k + 1] * lo
            acc_hi = acc_hi + wt[:, k:k + 1] * hi
        ms = (jnp.sum(acc_lo * acc_lo, axis=-1, keepdims=True)
              + jnp.sum(acc_hi * acc_hi, axis=-1, keepdims=True)) * (1.0 / D_MODEL)
        inv = lax.rsqrt(ms + RMS_EPS)
        o_ref[:, :half] = acc_lo * inv * nw_ref[:, :half]
        o_ref[:, half:] = acc_hi * inv * nw_ref[:, half:]

    @pl.when(i == 0)
    def _():
        first = slot_load(0, 0)
        first.start()
        first.wait()
        issue_rows(0)

        @pl.when(n > 1)
        def _():
            slot_load(1, 1).start()

    for par in range(2):
        @pl.when(i % 2 == par)
        def _(par=par):
            @pl.when(i + 1 < n)
            def _():
                slot_load(i + 1, 1 - par).wait()
                issue_rows(1 - par)

            @pl.when(i + 2 < n)
            def _():
                slot_load(i + 2, par).start()

            reduce_rows(par)


def _combine(slots, ys, x1, wt, nw):
    t = x1.shape[0]
    tm = COMBINE_TILE
    return pl.pallas_call(
        _combine_kernel,
        grid=(t // tm,),
        in_specs=[
            pl.BlockSpec(memory_space=pl.ANY),
            pl.BlockSpec(memory_space=pl.ANY),
            pl.BlockSpec((tm, D_MODEL), lambda i: (i, 0)),
            pl.BlockSpec((tm, TOP_K), lambda i: (i, 0)),
            pl.BlockSpec((1, D_MODEL), lambda i: (0, 0)),
        ],
        out_specs=pl.BlockSpec((tm, D_MODEL), lambda i: (i, 0)),
        out_shape=jax.ShapeDtypeStruct((t, D_MODEL), F32),
        scratch_shapes=[
            pltpu.SMEM((tm * TOP_K,), jnp.int32),
            pltpu.SMEM((tm * TOP_K,), jnp.int32),
            pltpu.VMEM((TOP_K, tm // SUBLANES, SUBLANES, ROW_WORDS), jnp.uint32),
            pltpu.VMEM((TOP_K, tm // SUBLANES, SUBLANES, ROW_WORDS), jnp.uint32),
            pltpu.SemaphoreType.DMA((4,)),
        ],
        compiler_params=_params(("arbitrary",), 32),
        name="combine",
    )(slots, ys, x1, wt, nw)


def _moe(parts, p, final_w):
    sizes = [x1.shape[0] for x1, _, _, _ in parts]
    idx = jnp.concatenate([q[2] for q in parts], axis=1)
    t_all = idx.shape[1]
    pos, cnt = _route(idx)
    counts = cnt[:, 0].astype(jnp.int32)
    tiles_e = (counts + GROUP_TILE - 1) // GROUP_TILE
    tile_end = jnp.cumsum(tiles_e)
    tile_start = tile_end - tiles_e
    n_used = tile_end[-1]
    n_tiles = (t_all * TOP_K) // GROUP_TILE + N_EXPERTS
    n_rows = n_tiles * GROUP_TILE
    first_row = tile_start * GROUP_TILE
    experts = jnp.arange(N_EXPERTS, dtype=jnp.int32)[:, None, None]
    base = jnp.sum(jnp.where(idx[None] == experts, first_row[:, None, None], 0), axis=0)
    slots = (base + pos).T.reshape(-1)
    tile_id = jnp.arange(n_tiles, dtype=jnp.int32)
    tile_src = jnp.minimum(tile_id, n_used - 1).astype(jnp.int32)
    tile_expert = jnp.sum(tile_src[:, None] >= tile_end[None, :], axis=1).astype(jnp.int32)
    tile_expert = jnp.minimum(tile_expert, N_EXPERTS - 1)
    ztile = jnp.where(counts > 0, tile_end - 1, -1).astype(jnp.int32)

    n_used = n_used.reshape(1).astype(jnp.int32)
    xs = _dispatch(slots, ztile, n_used, [q[1] for q in parts], n_rows)
    ys = _experts(tile_expert, tile_src, n_used, xs, p["wgu"], p["bgu"], p["wd"], p["bd"])
    outs = []
    off = 0
    for (x1, _, _, wt), t in zip(parts, sizes):
        outs.append(_combine(lax.slice(slots, (off * TOP_K,), ((off + t) * TOP_K,)), ys, x1, wt.T, final_w))
        off += t
    return outs


def _rope_tables(seq):
    half = ATTN_HEAD_DIM // 2
    inv_freq = ROPE_THETA ** (-jnp.arange(half, dtype=F32) / half)
    ang = jnp.arange(seq, dtype=F32)[:, None] * inv_freq[None, :]
    cos, sin = jnp.cos(ang), jnp.sin(ang)
    reps = LANES // ATTN_HEAD_DIM
    cos_t = jnp.tile(jnp.concatenate([cos, cos], axis=1), (1, reps))
    sin_t = jnp.tile(jnp.concatenate([-sin, sin], axis=1), (1, reps))
    return cos_t, sin_t


def _pack_layer(w_in, conv_w, conv_b, gate_b, wpa, wpm, wo, nmw, nfw, rw, rb, wgu, bgu, wd, bd):
    c0 = ATTN_QKV_COLS
    c1 = c0 + ML_QKV_COLS
    c2 = c1 + ML_O_COLS
    c3 = c2 + ML_GATE_COLS
    w_attn = w_in[:, :c0].reshape(D_MODEL, N_GROUPS, 3, ATTN_DIM)
    w_attn = w_attn * jnp.array([ATTN_HEAD_DIM ** -0.5, 1.0, 1.0], F32)[None, None, :, None]
    w_main = jnp.concatenate([w_in[:, c0:c2], w_in[:, c3:], w_attn.reshape(D_MODEL, c0)], axis=1).astype(BF16)
    w_gate = jnp.pad(w_in[:, c2:c3], ((0, 0), (0, LANES - ML_GATE_COLS))).astype(BF16)
    return dict(
        w_main=w_main, w_gate=w_gate, conv_w=conv_w, conv_b=conv_b.reshape(1, -1),
        gate_b=gate_b.reshape(-1, 1), wpa=wpa.astype(BF16), wpm=wpm.astype(BF16), wo=wo.astype(BF16),
        nmw=nmw.reshape(1, -1), nfw=nfw.reshape(1, -1), rw_t=rw.T, rb=rb.reshape(-1, 1),
        wgu=wgu.astype(BF16), bgu=bgu.reshape(N_EXPERTS, 1, -1), wd=wd.astype(BF16),
        bd=bd.reshape(N_EXPERTS, 1, -1))


def _layer(x, p, tables):
    b, s, _ = x.shape
    x2d = x.reshape(b * s, D_MODEL)
    proj, gates, qkv1, qkv2 = _inproj(x2d, p["nmw"], p["w_main"], p["w_gate"], tables[0], tables[1], s)
    proj3 = proj.reshape(b, s, N_MAIN)

    merged = _attn_plain(proj3)
    for g, qkv in ((1, qkv1), (2, qkv2)):
        merged = _attn_dilated(qkv, g, ATTN_PATTERNS[g][1], merged, b, s)
    attn_o = merged[0]

    ml_q, ml_kt = _conv_qk(proj3, p["conv_w"], p["conv_b"])
    gates_t = gates[:, :ML_GATE_COLS].reshape(b, s, ML_GATE_COLS).transpose(0, 2, 1)
    g_rows = _gate_prep(gates_t, p["gate_b"])
    g_cols = g_rows.transpose(0, 2, 1)
    h_f, h_b = _mlstm(proj3, ml_q, ml_kt, g_rows, g_cols)

    return _merge(x2d, attn_o, h_f.reshape(b * s, ML_DIM), h_b.reshape(b * s, ML_DIM), proj,
                  p["wpa"], p["wpm"], p["wo"], p["nfw"], p["rw_t"], p["rb"])


def kernel(x_prompt, x_sample, norm_mix_w, w_in, mlstm_conv_w, mlstm_conv_b, mlstm_gate_b, w_proj_attn,
           w_proj_mlstm, w_out, norm_ffn_w, router_w, router_b, expert_w_gu, expert_b_gu, expert_w_down,
           expert_b_down, norm_final_w):
    depth = w_in.shape[0]
    assert depth == 1, "the final RMSNorm is fused into the last layer's MoE kernel"
    p = _pack_layer(w_in[0], mlstm_conv_w[0], mlstm_conv_b[0], mlstm_gate_b[0], w_proj_attn[0],
                    w_proj_mlstm[0], w_out[0], norm_mix_w[0], norm_ffn_w[0], router_w[0], router_b[0],
                    expert_w_gu[0], expert_b_gu[0], expert_w_down[0], expert_b_down[0])
    final_w = norm_final_w.reshape(1, -1)
    xs = (x_prompt, x_sample)
    parts = [_layer(x, p, _rope_tables(x.shape[1])) for x in xs]
    outs = _moe(parts, p, final_w)
    return tuple(o.reshape(x.shape) for o, x in zip(outs, xs))
```

```python
import functools

import jax
import jax.numpy as jnp
from jax import lax
from jax.experimental import pallas as pl
from jax.experimental.pallas import tpu as pltpu

F32 = jnp.float32
BF16 = jnp.bfloat16

D_MODEL = 1024
ATTN_PATTERNS = ((128, 1), (512, 4), (2048, 16))
N_GROUPS = 3
ATTN_HEADS = 8
ATTN_HEAD_DIM = 64
ATTN_DIM = ATTN_HEADS * ATTN_HEAD_DIM
ATTN_BLOCK = 64
ROPE_THETA = 10000.0
ML_DIM = D_MODEL
ML_HEADS = 4
ML_HEAD_DIM = ML_DIM // ML_HEADS
ML_CHUNK = 256
CONV_W = 5
ATTN_QKV_COLS = N_GROUPS * 3 * ATTN_DIM
ML_QKV_COLS = 3 * ML_DIM
ML_O_COLS = ML_DIM
ML_GATE_COLS = 4 * ML_HEADS
MERGE_COLS = 2 * D_MODEL
N_EXPERTS = 32
TOP_K = 4
D_FF = D_MODEL
SWIGLU_LIMIT = 7.0
SWIGLU_ALPHA = 1.702
RMS_EPS = 1e-5
NEG_INF = -1e30

LANES = 128
SUBLANES = 8
COL_TILE = 512
N_PACKED = ML_QKV_COLS + ML_O_COLS + MERGE_COLS + ATTN_QKV_COLS
N_COL_TILES = N_PACKED // COL_TILE
ATTN_TILE0 = (ML_QKV_COLS + ML_O_COLS + MERGE_COLS) // COL_TILE
GROUP_TILES = 3 * ATTN_DIM // COL_TILE
N_MAIN_TILES = ATTN_TILE0 + GROUP_TILES
N_MAIN = N_MAIN_TILES * COL_TILE
ML_EXT = ML_HEAD_DIM + LANES


def _params(sem, vmem_mb):
    return pltpu.CompilerParams(dimension_semantics=sem, vmem_limit_bytes=vmem_mb * 1024 * 1024)


def _sigmoid(x):
    return 1.0 / (1.0 + jnp.exp(-x))


INPROJ_CHUNK = 512
WIDE_TILE = 3 * ATTN_DIM
N_WIDE_TILES = N_PACKED // WIDE_TILE
N_PLAIN_WIDE = ATTN_TILE0 * COL_TILE // WIDE_TILE


def _inproj_kernel(x_ref, nw_ref, w_ref, wg_ref, cos_ref, sin_ref, out_ref, gates_ref, g1_ref, g2_ref,
                   h_scr, slab, slab2):
    j = pl.program_id(1)
    tm = x_ref.shape[0]
    n_slabs = WIDE_TILE // LANES

    @pl.when(j == 0)
    def _():
        x = x_ref[...]
        ms = jnp.mean(x * x, axis=-1, keepdims=True)
        h = (x * lax.rsqrt(ms + RMS_EPS) * nw_ref[...]).astype(BF16)
        h_scr[...] = h
        gates_ref[...] = jnp.dot(h, wg_ref[...], preferred_element_type=F32)

    chunk = min(INPROJ_CHUNK, tm)

    def chunks():
        for c in range(tm // chunk):
            rows = slice(c * chunk, (c + 1) * chunk)
            yield c, rows, jnp.dot(h_scr[rows, :], w_ref[...], preferred_element_type=F32)

    def rope(a, rows):
        reps = ATTN_DIM // LANES
        c = jnp.concatenate([cos_ref[rows, :]] * reps, axis=1)
        s = jnp.concatenate([sin_ref[rows, :]] * reps, axis=1)
        lane = lax.broadcasted_iota(jnp.int32, a.shape, 1)
        half = ATTN_HEAD_DIM // 2
        first = (lane % ATTN_HEAD_DIM) < half
        sw = jnp.where(first, pltpu.roll(a, ATTN_DIM - half, 1), pltpu.roll(a, half, 1))
        return a * c + sw * s

    def rotated(acc, rows):
        return jnp.concatenate([rope(acc[:, :ATTN_DIM], rows), rope(acc[:, ATTN_DIM:2 * ATTN_DIM], rows),
                                acc[:, 2 * ATTN_DIM:]], axis=1)

    def deinterleave(val, c, dst_ref, dilation):
        n = chunk // dilation
        for s in range(n_slabs):
            slab[s] = val[:, s * LANES:(s + 1) * LANES]
        if dilation == 16:
            quarter = chunk // 4
            for s in range(n_slabs):
                for r4 in range(4):
                    slab2[s, r4 * quarter:(r4 + 1) * quarter, :] = slab[s, pl.ds(r4, quarter, stride=4), :]
            for r in range(dilation):
                start = (r % 4) * quarter + r // 4
                piece = jnp.concatenate([slab2[s, pl.ds(start, n, stride=4), :] for s in range(n_slabs)], axis=1)
                dst_ref[0, r, c * n:(c + 1) * n, :] = piece.astype(BF16)
            return
        for r in range(dilation):
            piece = jnp.concatenate([slab[s, pl.ds(r, n, stride=dilation), :] for s in range(n_slabs)], axis=1)
            dst_ref[0, r, c * n:(c + 1) * n, :] = piece.astype(BF16)

    @pl.when(j < N_PLAIN_WIDE)
    def _():
        for _, rows, acc in chunks():
            out_ref[rows, :] = acc.astype(BF16)

    dst = (out_ref, g1_ref, g2_ref)
    for g, (_, dilation) in enumerate(ATTN_PATTERNS):
        @pl.when(j == N_PLAIN_WIDE + g)
        def _(g=g, dilation=dilation):
            for c, rows, acc in chunks():
                val = rotated(acc, rows)
                if dilation == 1:
                    out_ref[rows, :] = val.astype(BF16)
                else:
                    deinterleave(val, c, dst[g], dilation)


def _inproj(x2d, nw, w_main, w_gate, cos_t, sin_t, seq):
    t = x2d.shape[0]
    tm = min(1024, seq)
    tiles_per_seq = seq // tm
    chunk = min(INPROJ_CHUNK, tm)

    def group_out(dilation):
        return (pl.BlockSpec((1, dilation, tm // dilation, WIDE_TILE), lambda i, j: (i, 0, 0, 0)),
                jax.ShapeDtypeStruct((t // tm, dilation, tm // dilation, WIDE_TILE), BF16))

    (g1_spec, g1_shape), (g2_spec, g2_shape) = [group_out(ATTN_PATTERNS[g][1]) for g in (1, 2)]
    return pl.pallas_call(
        _inproj_kernel,
        grid=(t // tm, N_WIDE_TILES),
        in_specs=[
            pl.BlockSpec((tm, D_MODEL), lambda i, j: (i, 0)),
            pl.BlockSpec((1, D_MODEL), lambda i, j: (0, 0)),
            pl.BlockSpec((D_MODEL, WIDE_TILE), lambda i, j: (0, j)),
            pl.BlockSpec((D_MODEL, LANES), lambda i, j: (0, 0)),
            pl.BlockSpec((tm, LANES), lambda i, j: (i % tiles_per_seq, 0)),
            pl.BlockSpec((tm, LANES), lambda i, j: (i % tiles_per_seq, 0)),
        ],
        out_specs=[
            pl.BlockSpec((tm, WIDE_TILE), lambda i, j: (i, jnp.minimum(j, N_PLAIN_WIDE))),
            pl.BlockSpec((tm, LANES), lambda i, j: (i, 0)),
            g1_spec, g2_spec,
        ],
        out_shape=[
            jax.ShapeDtypeStruct((t, N_MAIN), BF16),
            jax.ShapeDtypeStruct((t, LANES), F32),
            g1_shape, g2_shape,
        ],
        scratch_shapes=[pltpu.VMEM((tm, D_MODEL), BF16), pltpu.VMEM((WIDE_TILE // LANES, chunk, LANES), F32),
                        pltpu.VMEM((WIDE_TILE // LANES, chunk, LANES), F32)],
        compiler_params=_params(("arbitrary", "arbitrary"), 56),
        name="inproj",
    )(x2d, nw, w_main, w_gate, cos_t, sin_t)


ATTN_QUERY_TILE = 128


def _window_mask(n, qt, sub_len):
    kt = qt + 2 * ATTN_BLOCK
    qpos = n * qt + lax.broadcasted_iota(jnp.int32, (qt, kt), 0)
    kpos = n * qt - ATTN_BLOCK + lax.broadcasted_iota(jnp.int32, (qt, kt), 1)
    return (jnp.abs(kpos - qpos) <= ATTN_BLOCK) & (kpos >= 0) & (kpos < sub_len)


def _attend(q, k, v, valid):
    qt = q.shape[0]
    pairs = range(ATTN_HEADS // 2)
    lane = lax.broadcasted_iota(jnp.int32, (qt, LANES), 1)
    lo = lane < ATTN_HEAD_DIM
    valid2 = jnp.concatenate([valid, valid], axis=0)
    scores = []
    for p in pairs:
        qp = q[:, p * LANES:(p + 1) * LANES]
        zero = jnp.zeros_like(qp)
        stacked = jnp.concatenate([jnp.where(lo, qp, zero), jnp.where(lo, zero, qp)], axis=0)
        scores.append(lax.dot_general(stacked, k[:, p * LANES:(p + 1) * LANES], (((1,), (1,)), ((), ())),
                                      preferred_element_type=F32))
    probs, inv_den, lses = [], [], []
    for p in pairs:
        s = jnp.where(valid2, scores[p], NEG_INF)
        mx = jnp.max(s, axis=-1, keepdims=True)
        e = jnp.exp(s - mx)
        den = jnp.sum(e, axis=-1, keepdims=True)
        probs.append(e.astype(BF16))
        inv_den.append(1.0 / den)
        lses.append(mx + jnp.log(den))
    pvs = [jnp.dot(probs[p], v[:, p * LANES:(p + 1) * LANES], preferred_element_type=F32) for p in pairs]
    lse_out = jnp.zeros((qt, LANES), F32)
    outs = []
    for p in pairs:
        o = pvs[p] * inv_den[p]
        outs.append(jnp.where(lo, o[:qt], o[qt:]))
        lse_out = jnp.where(lane == 2 * p, lses[p][:qt], jnp.where(lane == 2 * p + 1, lses[p][qt:], lse_out))
    return outs, lse_out


def _head_expander():
    row = lax.broadcasted_iota(jnp.int32, (LANES, ATTN_DIM), 0)
    col = lax.broadcasted_iota(jnp.int32, (LANES, ATTN_DIM), 1)
    return jnp.where(col // ATTN_HEAD_DIM == row, 1.0, 0.0).astype(BF16)


def _merge_groups(o_cur, lse_cur, o_prev, lse_prev, expander):
    m = jnp.maximum(lse_prev, lse_cur)
    a = jnp.exp(lse_prev - m)
    tot = a + jnp.exp(lse_cur - m)
    w_prev = a / tot
    hi = w_prev.astype(BF16)
    lo = (w_prev - hi.astype(F32)).astype(BF16)
    spread = (jnp.dot(hi, expander, preferred_element_type=F32) + jnp.dot(lo, expander, preferred_element_type=F32))
    outs = [o_cur[p] + spread[:, p * LANES:(p + 1) * LANES] * (o_prev[p] - o_cur[p])
            for p in range(ATTN_HEADS // 2)]
    lane = lax.broadcasted_iota(jnp.int32, m.shape, 1)
    return outs, jnp.where(lane < ATTN_HEADS, m + jnp.log(tot), 0.0)


def _attn_kernel(q_ref, kp_ref, kc_ref, kn_ref, vp_ref, vc_ref, vn_ref, o_ref, l_ref, *, sub_len, qt):
    n_sub = q_ref.shape[1] // qt
    k = jnp.concatenate([kp_ref[0], kc_ref[0], kn_ref[0]], axis=0)
    v = jnp.concatenate([vp_ref[0], vc_ref[0], vn_ref[0]], axis=0)
    for sub in range(n_sub):
        valid = _window_mask(pl.program_id(1) * n_sub + sub, qt, sub_len)
        rows = slice(sub * qt, (sub + 1) * qt)
        keys = slice(sub * qt, (sub + 1) * qt + 2 * ATTN_BLOCK)
        outs, lse = _attend(q_ref[0, rows, :], k[keys], v[keys], valid)
        for p, o in enumerate(outs):
            o_ref[0, rows, p * LANES:(p + 1) * LANES] = o.astype(BF16)
        l_ref[0, rows, :] = lse


MERGE_ROWS = 256
RESIDUE_ROWS_PER_ITER = 512


def _attn_dilated_kernel(q_ref, kp_ref, kc_ref, kn_ref, vp_ref, vc_ref, vn_ref, op_ref, lp_ref, o_ref, l_ref,
                         o_slab, l_slab, *, sub_len, qt, dilation):
    valid = _window_mask(pl.program_id(1), qt, sub_len)

    per_iter = min(dilation, RESIDUE_ROWS_PER_ITER // qt)

    def residues(i, c):
        for r in [per_iter * i + u for u in range(per_iter)]:
            k = jnp.concatenate([kp_ref[0, r], kc_ref[0, r], kn_ref[0, r]], axis=0)
            v = jnp.concatenate([vp_ref[0, r], vc_ref[0, r], vn_ref[0, r]], axis=0)
            outs, lse = _attend(q_ref[0, r], k, v, valid)
            for p, o in enumerate(outs):
                o_slab[p, pl.ds(r, qt, stride=dilation), :] = o
            l_slab[pl.ds(r, qt, stride=dilation), :] = lse
        return c

    lax.fori_loop(0, dilation // per_iter, residues, 0)
    expander = _head_expander()

    def merge(i, c):
        rows = pl.ds(pl.multiple_of(i * MERGE_ROWS, MERGE_ROWS), MERGE_ROWS)
        prev = op_ref[rows, :].astype(F32)
        o_cur = [o_slab[p, rows, :] for p in range(ATTN_HEADS // 2)]
        o_prev = [prev[:, p * LANES:(p + 1) * LANES] for p in range(ATTN_HEADS // 2)]
        outs, lse = _merge_groups(o_cur, l_slab[rows, :], o_prev, lp_ref[rows, :], expander)
        for p, o in enumerate(outs):
            o_ref[rows, p * LANES:(p + 1) * LANES] = o.astype(BF16)
        l_ref[rows, :] = lse
        return c

    lax.fori_loop(0, qt * dilation // MERGE_ROWS, merge, 0)


PLAIN_STEP_TILES = 2


def _attn_plain(proj3):
    b, s, _ = proj3.shape
    qt = min(ATTN_QUERY_TILE, s)
    step = min(PLAIN_STEP_TILES * qt, s)
    nblk = s // ATTN_BLOCK
    qb = step // ATTN_BLOCK

    def cur(which):
        return pl.BlockSpec((1, step, COL_TILE), lambda bi, n: (bi, n, ATTN_TILE0 + which))

    def before(which):
        return pl.BlockSpec((1, ATTN_BLOCK, COL_TILE),
                            lambda bi, n: (bi, jnp.maximum(n * qb - 1, 0), ATTN_TILE0 + which))

    def after(which):
        return pl.BlockSpec((1, ATTN_BLOCK, COL_TILE),
                            lambda bi, n: (bi, jnp.minimum((n + 1) * qb, nblk - 1), ATTN_TILE0 + which))

    o, l = pl.pallas_call(
        functools.partial(_attn_kernel, sub_len=s, qt=qt),
        grid=(b, s // step),
        in_specs=[cur(0), before(1), cur(1), after(1), before(2), cur(2), after(2)],
        out_specs=[pl.BlockSpec((1, step, ATTN_DIM), lambda bi, n: (bi, n, 0)),
                   pl.BlockSpec((1, step, LANES), lambda bi, n: (bi, n, 0))],
        out_shape=[jax.ShapeDtypeStruct((b, s, ATTN_DIM), BF16), jax.ShapeDtypeStruct((b, s, LANES), F32)],
        compiler_params=_params(("arbitrary", "arbitrary"), 48),
        name="attn_g0",
    )(*([proj3] * 7))
    return o.reshape(b * s, ATTN_DIM), l.reshape(b * s, LANES)


def _attn_dilated(qkv, g, dilation, prev, b, s):
    n_tiles, _, rows, _ = qkv.shape
    tiles_per_seq = n_tiles // b
    sub_len = s // dilation
    qt = min(ATTN_QUERY_TILE, rows)
    span = qt * dilation
    q_per_tile = rows // qt
    h_per_tile = rows // ATTN_BLOCK
    qb = qt // ATTN_BLOCK
    nblk = sub_len // ATTN_BLOCK

    def cur(which):
        return pl.BlockSpec((1, dilation, qt, COL_TILE),
                            lambda bi, n: (bi * tiles_per_seq + n // q_per_tile, 0, n % q_per_tile, which))

    def halo(which, blk_of):
        def index(bi, n):
            blk = blk_of(n)
            return (bi * tiles_per_seq + blk // h_per_tile, 0, blk % h_per_tile, which)
        return pl.BlockSpec((1, dilation, ATTN_BLOCK, COL_TILE), index)

    before = lambda which: halo(which, lambda n: jnp.maximum(n * qb - 1, 0))
    after = lambda which: halo(which, lambda n: jnp.minimum((n + 1) * qb, nblk - 1))
    steps = sub_len // qt
    o_spec = pl.BlockSpec((span, ATTN_DIM), lambda bi, n: (bi * steps + n, 0))
    l_spec = pl.BlockSpec((span, LANES), lambda bi, n: (bi * steps + n, 0))
    return pl.pallas_call(
        functools.partial(_attn_dilated_kernel, sub_len=sub_len, qt=qt, dilation=dilation),
        grid=(b, steps),
        in_specs=[cur(0), before(1), cur(1), after(1), before(2), cur(2), after(2), o_spec, l_spec],
        out_specs=[o_spec, l_spec],
        out_shape=[jax.ShapeDtypeStruct((b * s, ATTN_DIM), BF16), jax.ShapeDtypeStruct((b * s, LANES), F32)],
        scratch_shapes=[pltpu.VMEM((ATTN_HEADS // 2, span, LANES), F32), pltpu.VMEM((span, LANES), F32)],
        compiler_params=_params(("arbitrary", "arbitrary"), 56),
        name=f"attn_g{g}",
    )(*([qkv] * 7), prev[0], prev[1])


CONV_HALO = 16


Q_TILES = ML_DIM // COL_TILE


def _conv_kernel(xp_ref, xc_ref, xn_ref, w_ref, b_ref, q_ref, kt_ref, buf, *, tm):
    i = pl.program_id(1)
    c = pl.program_id(2)
    last = pl.num_programs(1) - 1
    before = jnp.where(i > 0, xp_ref[0].astype(F32), 0.0)
    centre = xc_ref[0].astype(F32)
    after = jnp.where(i < last, xn_ref[0].astype(F32), 0.0)
    w = w_ref[...]
    bias = b_ref[...]
    pieces = []
    for s in range(COL_TILE // LANES):
        lanes = slice(s * LANES, (s + 1) * LANES)
        buf[s, 0:CONV_HALO, :] = before[:, lanes]
        buf[s, CONV_HALO:CONV_HALO + tm, :] = centre[:, lanes]
        buf[s, CONV_HALO + tm:, :] = after[:, lanes]
        acc = jnp.broadcast_to(bias[:, lanes], (tm, LANES))
        for tap in range(CONV_W):
            acc = acc + w[tap:tap + 1, lanes] * buf[s, pl.ds(CONV_HALO - CONV_W // 2 + tap, tm, stride=1), :]
        pieces.append(acc)
    y = jnp.concatenate(pieces, axis=1)
    y = y * _sigmoid(y)

    @pl.when(c < Q_TILES)
    def _():
        q_ref[0] = y.astype(BF16)

    @pl.when(c >= Q_TILES)
    def _():
        kt_ref[0] = (y * ML_HEAD_DIM ** -0.5).T.astype(BF16)


def _conv_qk(proj3, conv_w, conv_b):
    b, s, _ = proj3.shape
    tm = min(1024, s)
    hb = tm // CONV_HALO
    nh = s // CONV_HALO
    return pl.pallas_call(
        functools.partial(_conv_kernel, tm=tm),
        grid=(b, s // tm, 2 * Q_TILES),
        in_specs=[
            pl.BlockSpec((1, CONV_HALO, COL_TILE), lambda bi, i, c: (bi, jnp.maximum(i * hb - 1, 0), c)),
            pl.BlockSpec((1, tm, COL_TILE), lambda bi, i, c: (bi, i, c)),
            pl.BlockSpec((1, CONV_HALO, COL_TILE), lambda bi, i, c: (bi, jnp.minimum((i + 1) * hb, nh - 1), c)),
            pl.BlockSpec((CONV_W, COL_TILE), lambda bi, i, c: (0, c)),
            pl.BlockSpec((1, COL_TILE), lambda bi, i, c: (0, c)),
        ],
        out_specs=[
            pl.BlockSpec((1, tm, COL_TILE), lambda bi, i, c: (bi, i, jnp.minimum(c, Q_TILES - 1))),
            pl.BlockSpec((1, COL_TILE, tm), lambda bi, i, c: (bi, jnp.maximum(c - Q_TILES, 0), i)),
        ],
        out_shape=[jax.ShapeDtypeStruct((b, s, ML_DIM), BF16), jax.ShapeDtypeStruct((b, ML_DIM, s), BF16)],
        scratch_shapes=[pltpu.VMEM((COL_TILE // LANES, tm + 2 * CONV_HALO, LANES), F32)],
        compiler_params=_params(("arbitrary", "arbitrary", "arbitrary"), 32),
        name="conv_qk",
    )(proj3, proj3, proj3, conv_w, conv_b)


def _gate_kernel(g_ref, b_ref, o_ref):
    g = g_ref[0] + b_ref[...]
    width = g.shape[1]
    logsig = jnp.minimum(g, 0.0) - jnp.log(1.0 + jnp.exp(-jnp.abs(g)))
    lane = lax.broadcasted_iota(jnp.int32, g.shape, 1) % ML_CHUNK
    pre = logsig
    suf = logsig
    step = 1
    while step < ML_CHUNK:
        pre = pre + jnp.where(lane >= step, pltpu.roll(pre, step, 1), 0.0)
        suf = suf + jnp.where(lane < ML_CHUNK - step, pltpu.roll(suf, width - step, 1), 0.0)
        step *= 2
    row = lax.broadcasted_iota(jnp.int32, g.shape, 0)
    is_f_fwd = (row >= ML_HEADS) & (row < 2 * ML_HEADS)
    is_f_bwd = row >= 3 * ML_HEADS
    o_ref[0] = jnp.where(is_f_fwd, pre, jnp.where(is_f_bwd, suf, g))


def _gate_prep(gates_t, gate_b):
    b, rows, s = gates_t.shape
    sb = min(2048, s)
    return pl.pallas_call(
        _gate_kernel,
        grid=(b, s // sb),
        in_specs=[
            pl.BlockSpec((1, rows, sb), lambda bi, i: (bi, 0, i)),
            pl.BlockSpec((rows, 1), lambda bi, i: (0, 0)),
        ],
        out_specs=pl.BlockSpec((1, rows, sb), lambda bi, i: (bi, 0, i)),
        out_shape=jax.ShapeDtypeStruct((b, rows, s), F32),
        compiler_params=_params(("arbitrary", "arbitrary"), 32),
        name="gate_prep",
    )(gates_t, gate_b)


ML_CHAIN_GROUP = 8


def _mlstm_kernel(qf_ref, kf_ref, vf_ref, qb_ref, kb_ref, vb_ref, grf_ref, grb_ref, gcf_ref, gcb_ref,
                  hf_ref, hb_ref, s_scr, m_scr):
    c = pl.program_id(1)

    @pl.when(c == 0)
    def _():
        def clear(i, carry):
            s_scr[i] = jnp.zeros(s_scr.shape[1:], F32)
            return carry

        lax.fori_loop(0, s_scr.shape[0], clear, 0)
        m_scr[...] = jnp.zeros_like(m_scr)

    t_i = lax.broadcasted_iota(jnp.int32, (ML_CHUNK, ML_CHUNK), 0)
    s_i = lax.broadcasted_iota(jnp.int32, (ML_CHUNK, ML_CHUNK), 1)
    ones_col = jnp.where(lax.broadcasted_iota(jnp.int32, (ML_CHUNK, LANES), 1) == 0, 1.0, 0.0).astype(BF16)
    dirs = ((qf_ref, kf_ref, vf_ref, grf_ref, gcf_ref, hf_ref), (qb_ref, kb_ref, vb_ref, grb_ref, gcb_ref, hb_ref))
    all_chains = [(dirn, head) + refs for dirn, refs in enumerate(dirs) for head in range(ML_HEADS)]

    for first in range(0, len(all_chains), ML_CHAIN_GROUP):
        chains = all_chains[first:first + ML_CHAIN_GROUP]
        matmuls = []
        for dirn, head, q_ref, k_ref, v_ref, _, _, _ in chains:
            hs = slice(head * ML_HEAD_DIM, (head + 1) * ML_HEAD_DIM)
            q = q_ref[0, :, hs]
            qk_raw = jnp.dot(q, k_ref[0, hs, :], preferred_element_type=F32)
            q_state = jnp.dot(q, s_scr[dirn * ML_HEADS + head].astype(BF16), preferred_element_type=F32)
            matmuls.append((qk_raw, q_state))

        weights = []
        for dirn, head, _, _, _, gr_ref, gc_ref, _ in chains:
            ii = dirn * 2 * ML_HEADS + head
            bi = ii + ML_HEADS
            mask = (s_i <= t_i) if dirn == 0 else (s_i >= t_i)
            i_row, b_row = gr_ref[0, ii:ii + 1, :], gr_ref[0, bi:bi + 1, :]
            b_col = gc_ref[0, :, bi:bi + 1]
            m = m_scr[dirn * ML_HEADS + head][0:1, 0:1]
            dmat = jnp.where(mask, b_col - b_row + i_row, NEG_INF)
            inter = b_col + m
            m_t = jnp.maximum(inter, jnp.max(dmat, axis=-1, keepdims=True))
            weights.append((jnp.exp(dmat - m_t), jnp.exp(inter - m_t), jnp.exp(-m_t)))

        for (dirn, head, _, _, v_ref, _, _, h_ref), (qk_raw, q_state), (w_intra, w_inter, floor) in zip(
                chains, matmuls, weights):
            hs = slice(head * ML_HEAD_DIM, (head + 1) * ML_HEAD_DIM)
            v_ext = jnp.concatenate([v_ref[0, :, hs], ones_col], axis=1)
            num = w_inter * q_state + jnp.dot((qk_raw * w_intra).astype(BF16), v_ext, preferred_element_type=F32)
            den = jnp.maximum(jnp.abs(num[:, ML_HEAD_DIM:ML_HEAD_DIM + 1]), floor)
            h_ref[0, :, hs] = (num[:, :ML_HEAD_DIM] / den).astype(BF16)

        for dirn, head, _, k_ref, v_ref, gr_ref, _, _ in chains:
            idx = dirn * ML_HEADS + head
            ii = dirn * 2 * ML_HEADS + head
            bi = ii + ML_HEADS
            hs = slice(head * ML_HEAD_DIM, (head + 1) * ML_HEAD_DIM)
            i_row, b_row = gr_ref[0, ii:ii + 1, :], gr_ref[0, bi:bi + 1, :]
            b_last = b_row[:, ML_CHUNK - 1:ML_CHUNK] if dirn == 0 else b_row[:, 0:1]
            m = m_scr[idx][0:1, 0:1]
            log_w = b_last - b_row + i_row
            m_new = jnp.maximum(b_last + m, jnp.max(log_w, axis=-1, keepdims=True))
            decay = jnp.exp(b_last + m - m_new)
            v_ext = jnp.concatenate([v_ref[0, :, hs], ones_col], axis=1)
            wk_t = (k_ref[0, hs, :].astype(F32) * jnp.exp(log_w - m_new)).astype(BF16)
            s_scr[idx] = decay * s_scr[idx] + jnp.dot(wk_t, v_ext, preferred_element_type=F32)
            m_scr[idx] = jnp.broadcast_to(m_new, m_scr.shape[1:])


def _mlstm(proj3, q, k_t, g_rows, g_cols):
    b, s, _ = proj3.shape
    nc = s // ML_CHUNK
    v_tile = 2 * ML_DIM // ML_DIM
    n_rows = g_rows.shape[1]

    def rows(col, reverse):
        return pl.BlockSpec((1, ML_CHUNK, ML_DIM), lambda bi, c: (bi, nc - 1 - c if reverse else c, col))

    def cols(height, reverse):
        return pl.BlockSpec((1, height, ML_CHUNK), lambda bi, c: (bi, 0, nc - 1 - c if reverse else c))

    return pl.pallas_call(
        _mlstm_kernel,
        grid=(b, nc),
        in_specs=[
            rows(0, False), cols(ML_DIM, False), rows(v_tile, False),
            rows(0, True), cols(ML_DIM, True), rows(v_tile, True),
            cols(n_rows, False), cols(n_rows, True),
            pl.BlockSpec((1, ML_CHUNK, n_rows), lambda bi, c: (bi, c, 0)),
            pl.BlockSpec((1, ML_CHUNK, n_rows), lambda bi, c: (bi, nc - 1 - c, 0)),
        ],
        out_specs=[rows(0, False), rows(0, True)],
        out_shape=[jax.ShapeDtypeStruct((b, s, ML_DIM), BF16)] * 2,
        scratch_shapes=[
            pltpu.VMEM((2 * ML_HEADS, ML_HEAD_DIM, ML_EXT), F32),
            pltpu.VMEM((2 * ML_HEADS, 8, LANES), F32),
        ],
        compiler_params=_params(("arbitrary", "arbitrary"), 32),
        name="mlstm",
    )(q, k_t, proj3, q, k_t, proj3, g_rows, g_rows, g_cols, g_cols)


def _pack_bf16_pair(lo, hi):
    lo_bits = lax.bitcast_convert_type(lo.astype(BF16).astype(F32), jnp.uint32)
    hi_bits = lax.bitcast_convert_type(hi.astype(BF16).astype(F32), jnp.uint32)
    return (hi_bits & jnp.uint32(0xFFFF0000)) | (lo_bits >> 16)


def _unpack_bf16_pair(packed):
    lo = lax.bitcast_convert_type(packed << 16, F32)
    hi = lax.bitcast_convert_type(packed & jnp.uint32(0xFFFF0000), F32)
    return lo, hi


MERGE_CHUNK = 512


def _merge_kernel(x_ref, ao_ref, hf_ref, hb_ref, mo_ref, mg_ref, wpa_ref, wpm_ref, wo_ref, nfw_ref,
                  rw_ref, rb_ref, x1_ref, h2_ref, idx_ref, wt_ref):
    tm = x_ref.shape[0]
    chunk = min(MERGE_CHUNK, tm)
    for c in range(tm // chunk):
        rows = slice(c * chunk, (c + 1) * chunk)
        y_attn = jnp.dot(ao_ref[rows, :], wpa_ref[...], preferred_element_type=F32)
        hsum = hf_ref[rows, :].astype(F32) + hb_ref[rows, :].astype(F32)
        ml = (_sigmoid(mo_ref[rows, :].astype(F32)) * hsum).astype(BF16)
        y_ml = jnp.dot(ml, wpm_ref[...], preferred_element_type=F32)
        gates = _sigmoid(mg_ref[rows, :].astype(F32))
        mixed = (gates[:, :D_MODEL] * y_attn + gates[:, D_MODEL:] * y_ml).astype(BF16)
        x1 = x_ref[rows, :] + jnp.dot(mixed, wo_ref[...], preferred_element_type=F32)
        x1_ref[rows, :] = x1
        ms = jnp.mean(x1 * x1, axis=-1, keepdims=True)
        h2 = x1 * lax.rsqrt(ms + RMS_EPS) * nfw_ref[...]
        h2_ref[rows, :] = _pack_bf16_pair(h2[:, :D_MODEL // 2], h2[:, D_MODEL // 2:])

        logits = lax.dot_general(rw_ref[...], h2, (((1,), (1,)), ((), ())), preferred_element_type=F32,
                                 precision=lax.Precision.HIGHEST) + rb_ref[...]
        row = lax.broadcasted_iota(jnp.int32, logits.shape, 0)
        rest = logits
        vals = []
        for k in range(TOP_K):
            mx = jnp.max(rest, axis=0, keepdims=True)
            first = jnp.min(jnp.where(rest == mx, row, N_EXPERTS), axis=0, keepdims=True)
            vals.append(mx)
            idx_ref[k:k + 1, rows] = first
            rest = jnp.where(row == first, -jnp.inf, rest)
        exps = [jnp.exp(v - vals[0]) for v in vals]
        tot = exps[0] + exps[1] + exps[2] + exps[3]
        for k in range(TOP_K):
            wt_ref[k:k + 1, rows] = exps[k] / tot


def _merge(x2d, attn_o, h_f, h_b, proj, wpa, wpm, wo, nfw, rw_t, rb):
    t = x2d.shape[0]
    tm = min(512, t)
    ml_o_tile = ML_QKV_COLS // ML_O_COLS
    merge_tile = (ML_QKV_COLS + ML_O_COLS) // MERGE_COLS
    row = lambda width: pl.BlockSpec((tm, width), lambda i: (i, 0))
    full = lambda a: pl.BlockSpec(a.shape, lambda i: (0, 0))
    return pl.pallas_call(
        _merge_kernel,
        grid=(t // tm,),
        in_specs=[
            row(D_MODEL), row(ATTN_DIM), row(ML_DIM), row(ML_DIM),
            pl.BlockSpec((tm, ML_O_COLS), lambda i: (i, ml_o_tile)),
            pl.BlockSpec((tm, MERGE_COLS), lambda i: (i, merge_tile)),
            full(wpa), full(wpm), full(wo), full(nfw), full(rw_t), full(rb),
        ],
        out_specs=[row(D_MODEL), row(D_MODEL // 2), pl.BlockSpec((TOP_K, tm), lambda i: (0, i)),
                   pl.BlockSpec((TOP_K, tm), lambda i: (0, i))],
        out_shape=[
            jax.ShapeDtypeStruct((t, D_MODEL), F32),
            jax.ShapeDtypeStruct((t, D_MODEL // 2), jnp.uint32),
            jax.ShapeDtypeStruct((TOP_K, t), jnp.int32),
            jax.ShapeDtypeStruct((TOP_K, t), F32),
        ],
        compiler_params=_params(("arbitrary",), 48),
        name="merge",
    )(x2d, attn_o, h_f, h_b, proj, proj, wpa, wpm, wo, nfw, rw_t, rb)


ROUTE_TILE = 512


def _route_kernel(idx_ref, pos_ref, cnt_ref, base):
    @pl.when(pl.program_id(0) == 0)
    def _():
        base[...] = jnp.zeros_like(base)

    idx = idx_ref[...]
    row = lax.broadcasted_iota(jnp.int32, (N_EXPERTS, ROUTE_TILE), 0)
    onehot = jnp.zeros((N_EXPERTS, ROUTE_TILE), F32)
    for k in range(TOP_K):
        onehot = onehot + jnp.where(row == idx[k:k + 1, :], 1.0, 0.0)
    s_i = lax.broadcasted_iota(jnp.int32, (ROUTE_TILE, ROUTE_TILE), 0)
    t_i = lax.broadcasted_iota(jnp.int32, (ROUTE_TILE, ROUTE_TILE), 1)
    upper = jnp.where(s_i <= t_i, 1.0, 0.0).astype(BF16)
    incl = jnp.dot(onehot.astype(BF16), upper, preferred_element_type=F32)
    before = base[:, 0:1]
    count = incl + before
    for k in range(TOP_K):
        mine = jnp.sum(jnp.where(row == idx[k:k + 1, :], count, 0.0), axis=0, keepdims=True)
        pos_ref[k:k + 1, :] = (mine - 1.0).astype(jnp.int32)
    total = before + incl[:, ROUTE_TILE - 1:ROUTE_TILE]
    base[...] = jnp.broadcast_to(total, base.shape)
    cnt_ref[...] = jnp.broadcast_to(total, cnt_ref.shape)


def _route(idx):
    t = idx.shape[1]
    return pl.pallas_call(
        _route_kernel,
        grid=(t // ROUTE_TILE,),
        in_specs=[pl.BlockSpec((TOP_K, ROUTE_TILE), lambda i: (0, i))],
        out_specs=[pl.BlockSpec((TOP_K, ROUTE_TILE), lambda i: (0, i)),
                   pl.BlockSpec((N_EXPERTS, LANES), lambda i: (0, 0))],
        out_shape=[jax.ShapeDtypeStruct((TOP_K, t), jnp.int32),
                   jax.ShapeDtypeStruct((N_EXPERTS, LANES), F32)],
        scratch_shapes=[pltpu.VMEM((N_EXPERTS, LANES), F32)],
        compiler_params=_params(("arbitrary",), 32),
        name="route",
    )(idx)


GROUP_TILE = 512
DISPATCH_TILE = 2048
ROW_WORDS = D_MODEL // 2


def _dispatch_kernel(slot_hbm, ztile_ref, nu_ref, *refs, steps):
    srcs = refs[:len(steps)]
    dst_hbm, slot_smem, zbuf, sem = refs[len(steps):]
    i = pl.program_id(0)
    n_slots = DISPATCH_TILE * TOP_K
    n_tiles = dst_hbm.shape[0] // GROUP_TILE
    load = pltpu.make_async_copy(slot_hbm.at[pl.ds(i * n_slots, n_slots)], slot_smem, sem.at[0])
    load.start()

    @pl.when(i == 0)
    def _():
        zbuf[...] = jnp.zeros_like(zbuf)

        def fill_tile(tile):
            pltpu.make_async_copy(zbuf, dst_hbm.at[pl.ds(tile * GROUP_TILE, GROUP_TILE), :], sem.at[1]).start()

        def fill_last(e, n):
            tile = ztile_ref[e]

            @pl.when(tile >= 0)
            def _():
                fill_tile(tile)

            return n + jnp.where(tile >= 0, 1, 0)

        def fill_unused(tile, c):
            fill_tile(tile)
            return c

        n_fill = lax.fori_loop(0, N_EXPERTS, fill_last, 0)
        lax.fori_loop(nu_ref[0], n_tiles, fill_unused, 0)

        def drain(_, c):
            pltpu.make_async_copy(zbuf, dst_hbm.at[pl.ds(0, GROUP_TILE), :], sem.at[1]).wait()
            return c

        lax.fori_loop(0, n_fill + n_tiles - nu_ref[0], drain, 0)

    load.wait()

    first = 0
    for src_ref, n_steps in zip(srcs, steps):
        @pl.when((i >= first) & (i < first + n_steps))
        def _(src_ref=src_ref):
            def issue(t8, c):
                for r in range(SUBLANES):
                    src = src_ref.at[t8, pl.ds(r, 1), :]
                    for k in range(TOP_K):
                        slot = slot_smem[(t8 * SUBLANES + r) * TOP_K + k]
                        pltpu.make_async_copy(src, dst_hbm.at[pl.ds(slot, 1), :], sem.at[1]).start()
                return c

            lax.fori_loop(0, DISPATCH_TILE // SUBLANES, issue, 0)

        first += n_steps
    pltpu.make_async_copy(dst_hbm.at[pl.ds(0, n_slots), :], dst_hbm.at[pl.ds(0, n_slots), :], sem.at[1]).wait()


def _dispatch(slots, ztile, n_used, sources, n_rows):
    steps = tuple(h.shape[0] // DISPATCH_TILE for h in sources)
    any_spec = pl.BlockSpec(memory_space=pl.ANY)
    smem_spec = pl.BlockSpec(memory_space=pltpu.SMEM)
    src_specs = []
    first = 0
    for n_steps in steps:
        src_specs.append(pl.BlockSpec(
            (DISPATCH_TILE // SUBLANES, SUBLANES, ROW_WORDS),
            lambda i, first=first, n_steps=n_steps: (jnp.clip(i - first, 0, n_steps - 1), 0, 0)))
        first += n_steps
    sources = [h.reshape(h.shape[0] // SUBLANES, SUBLANES, ROW_WORDS) for h in sources]
    return pl.pallas_call(
        functools.partial(_dispatch_kernel, steps=steps),
        grid=(sum(steps),),
        in_specs=[any_spec, smem_spec, smem_spec] + src_specs,
        out_specs=any_spec,
        out_shape=jax.ShapeDtypeStruct((n_rows, ROW_WORDS), jnp.uint32),
        scratch_shapes=[
            pltpu.SMEM((DISPATCH_TILE * TOP_K,), jnp.int32),
            pltpu.VMEM((GROUP_TILE, ROW_WORDS), jnp.uint32),
            pltpu.SemaphoreType.DMA((2,)),
        ],
        compiler_params=_params(("arbitrary",), 32),
        name="dispatch",
    )(slots, ztile, n_used, *sources)


EXPERT_CHUNK = 512


def _expert_kernel(te_ref, ts_ref, nu_ref, x_ref, wgu_ref, bgu_ref, wd_ref, bd_ref, y_ref):
    j = pl.program_id(0)

    @pl.when(j < nu_ref[0])
    def _():
        half = D_MODEL // 2
        lo, hi = _unpack_bf16_pair(x_ref[...])
        lo, hi = lo.astype(BF16), hi.astype(BF16)
        y = jnp.broadcast_to(bd_ref[0], (GROUP_TILE, D_MODEL))
        for c in range(D_FF // EXPERT_CHUNK):
            gcols = slice(c * EXPERT_CHUNK, (c + 1) * EXPERT_CHUNK)
            ucols = slice(D_FF + c * EXPERT_CHUNK, D_FF + (c + 1) * EXPERT_CHUNK)
            gate = (jnp.dot(lo, wgu_ref[0, :half, gcols], preferred_element_type=F32)
                    + jnp.dot(hi, wgu_ref[0, half:, gcols], preferred_element_type=F32) + bgu_ref[0, :, gcols])
            up = (jnp.dot(lo, wgu_ref[0, :half, ucols], preferred_element_type=F32)
                  + jnp.dot(hi, wgu_ref[0, half:, ucols], preferred_element_type=F32) + bgu_ref[0, :, ucols])
            gate = jnp.minimum(gate, SWIGLU_LIMIT)
            up = jnp.clip(up, -SWIGLU_LIMIT, SWIGLU_LIMIT)
            hid = (up + 1.0) * gate * _sigmoid(SWIGLU_ALPHA * gate)
            y = y + jnp.dot(hid.astype(BF16), wd_ref[0, gcols, :], preferred_element_type=F32)
        y_ref[...] = _pack_bf16_pair(y[:, :half], y[:, half:])

    @pl.when(j >= nu_ref[0])
    def _():
        y_ref[...] = jnp.zeros_like(y_ref)


def _experts(tile_expert, tile_src, n_used, xs, wgu, bgu, wd, bd):
    n_rows = xs.shape[0]
    row_spec = pl.BlockSpec((GROUP_TILE, ROW_WORDS), lambda j, te, ts, nu: (ts[j], 0))
    out_spec = pl.BlockSpec((GROUP_TILE, ROW_WORDS), lambda j, te, ts, nu: (j, 0))
    return pl.pallas_call(
        _expert_kernel,
        grid_spec=pltpu.PrefetchScalarGridSpec(
            num_scalar_prefetch=3,
            grid=(n_rows // GROUP_TILE,),
            in_specs=[
                row_spec,
                pl.BlockSpec((1, D_MODEL, 2 * D_FF), lambda j, te, ts, nu: (te[j], 0, 0)),
                pl.BlockSpec((1, 1, 2 * D_FF), lambda j, te, ts, nu: (te[j], 0, 0)),
                pl.BlockSpec((1, D_FF, D_MODEL), lambda j, te, ts, nu: (te[j], 0, 0)),
                pl.BlockSpec((1, 1, D_MODEL), lambda j, te, ts, nu: (te[j], 0, 0)),
            ],
            out_specs=out_spec,
        ),
        out_shape=jax.ShapeDtypeStruct((n_rows, ROW_WORDS), jnp.uint32),
        compiler_params=_params(("arbitrary",), 56),
        name="experts",
    )(tile_expert, tile_src, n_used, xs, wgu, bgu, wd, bd)


COMBINE_TILE = 512


def _combine_kernel(slot_hbm, ys_hbm, x1_ref, wt_ref, nw_ref, o_ref, slot_a, slot_b, buf_a, buf_b, sem):
    i = pl.program_id(0)
    n = pl.num_programs(0)
    n_slots = COMBINE_TILE * TOP_K
    slot_bufs = (slot_a, slot_b)
    row_bufs = (buf_a, buf_b)

    def slot_load(tile, par):
        return pltpu.make_async_copy(slot_hbm.at[pl.ds(tile * n_slots, n_slots)], slot_bufs[par], sem.at[par])

    def issue_rows(par):
        def issue(t8, c):
            for r in range(SUBLANES):
                for k in range(TOP_K):
                    slot = slot_bufs[par][(t8 * SUBLANES + r) * TOP_K + k]
                    pltpu.make_async_copy(ys_hbm.at[pl.ds(slot, 1), :], row_bufs[par].at[k, t8, pl.ds(r, 1), :],
                                          sem.at[2 + par]).start()
            return c

        lax.fori_loop(0, COMBINE_TILE // SUBLANES, issue, 0)

    def reduce_rows(par):
        pltpu.make_async_copy(ys_hbm.at[pl.ds(0, n_slots), :], ys_hbm.at[pl.ds(0, n_slots), :],
                              sem.at[2 + par]).wait()
        half = D_MODEL // 2
        wt = wt_ref[...]
        acc_lo = x1_ref[:, :half]
        acc_hi = x1_ref[:, half:]
        for k in range(TOP_K):
            lo, hi = _unpack_bf16_pair(row_bufs[par][k].reshape(COMBINE_TILE, ROW_WORDS))
            acc_lo = acc_lo + wt[:, k:k + 1] * lo
            acc_hi = acc_hi + wt[:, k:k + 1] * hi
        ms = (jnp.sum(acc_lo * acc_lo, axis=-1, keepdims=True)
              + jnp.sum(acc_hi * acc_hi, axis=-1, keepdims=True)) * (1.0 / D_MODEL)
        inv = lax.rsqrt(ms + RMS_EPS)
        o_ref[:, :half] = acc_lo * inv * nw_ref[:, :half]
        o_ref[:, half:] = acc_hi * inv * nw_ref[:, half:]

    @pl.when(i == 0)
    def _():
        first = slot_load(0, 0)
        first.start()
        first.wait()
        issue_rows(0)

        @pl.when(n > 1)
        def _():
            slot_load(1, 1).start()

    for par in range(2):
        @pl.when(i % 2 == par)
        def _(par=par):
            @pl.when(i + 1 < n)
            def _():
                slot_load(i + 1, 1 - par).wait()
                issue_rows(1 - par)

            @pl.when(i + 2 < n)
            def _():
                slot_load(i + 2, par).start()

            reduce_rows(par)


def _combine(slots, ys, x1, wt, nw):
    t = x1.shape[0]
    tm = COMBINE_TILE
    return pl.pallas_call(
        _combine_kernel,
        grid=(t // tm,),
        in_specs=[
            pl.BlockSpec(memory_space=pl.ANY),
            pl.BlockSpec(memory_space=pl.ANY),
            pl.BlockSpec((tm, D_MODEL), lambda i: (i, 0)),
            pl.BlockSpec((tm, TOP_K), lambda i: (i, 0)),
            pl.BlockSpec((1, D_MODEL), lambda i: (0, 0)),
        ],
        out_specs=pl.BlockSpec((tm, D_MODEL), lambda i: (i, 0)),
        out_shape=jax.ShapeDtypeStruct((t, D_MODEL), F32),
        scratch_shapes=[
            pltpu.SMEM((tm * TOP_K,), jnp.int32),
            pltpu.SMEM((tm * TOP_K,), jnp.int32),
            pltpu.VMEM((TOP_K, tm // SUBLANES, SUBLANES, ROW_WORDS), jnp.uint32),
            pltpu.VMEM((TOP_K, tm // SUBLANES, SUBLANES, ROW_WORDS), jnp.uint32),
            pltpu.SemaphoreType.DMA((4,)),
        ],
        compiler_params=_params(("arbitrary",), 32),
        name="combine",
    )(slots, ys, x1, wt, nw)


def _moe(parts, p, final_w):
    sizes = [x1.shape[0] for x1, _, _, _ in parts]
    idx = jnp.concatenate([q[2] for q in parts], axis=1)
    t_all = idx.shape[1]
    pos, cnt = _route(idx)
    counts = cnt[:, 0].astype(jnp.int32)
    tiles_e = (counts + GROUP_TILE - 1) // GROUP_TILE
    tile_end = jnp.cumsum(tiles_e)
    tile_start = tile_end - tiles_e
    n_used = tile_end[-1]
    n_tiles = (t_all * TOP_K) // GROUP_TILE + N_EXPERTS
    n_rows = n_tiles * GROUP_TILE
    first_row = tile_start * GROUP_TILE
    experts = jnp.arange(N_EXPERTS, dtype=jnp.int32)[:, None, None]
    base = jnp.sum(jnp.where(idx[None] == experts, first_row[:, None, None], 0), axis=0)
    slots = (base + pos).T.reshape(-1)
    tile_id = jnp.arange(n_tiles, dtype=jnp.int32)
    tile_src = jnp.minimum(tile_id, n_used - 1).astype(jnp.int32)
    tile_expert = jnp.sum(tile_src[:, None] >= tile_end[None, :], axis=1).astype(jnp.int32)
    tile_expert = jnp.minimum(tile_expert, N_EXPERTS - 1)
    ztile = jnp.where(counts > 0, tile_end - 1, -1).astype(jnp.int32)

    n_used = n_used.reshape(1).astype(jnp.int32)
    xs = _dispatch(slots, ztile, n_used, [q[1] for q in parts], n_rows)
    ys = _experts(tile_expert, tile_src, n_used, xs, p["wgu"], p["bgu"], p["wd"], p["bd"])
    outs = []
    off = 0
    for (x1, _, _, wt), t in zip(parts, sizes):
        outs.append(_combine(lax.slice(slots, (off * TOP_K,), ((off + t) * TOP_K,)), ys, x1, wt.T, final_w))
        off += t
    return outs


def _rope_tables(seq):
    half = ATTN_HEAD_DIM // 2
    inv_freq = ROPE_THETA ** (-jnp.arange(half, dtype=F32) / half)
    ang = jnp.arange(seq, dtype=F32)[:, None] * inv_freq[None, :]
    cos, sin = jnp.cos(ang), jnp.sin(ang)
    reps = LANES // ATTN_HEAD_DIM
    cos_t = jnp.tile(jnp.concatenate([cos, cos], axis=1), (1, reps))
    sin_t = jnp.tile(jnp.concatenate([-sin, sin], axis=1), (1, reps))
    return cos_t, sin_t


def _pack_layer(w_in, conv_w, conv_b, gate_b, wpa, wpm, wo, nmw, nfw, rw, rb, wgu, bgu, wd, bd):
    c0 = ATTN_QKV_COLS
    c1 = c0 + ML_QKV_COLS
    c2 = c1 + ML_O_COLS
    c3 = c2 + ML_GATE_COLS
    w_attn = w_in[:, :c0].reshape(D_MODEL, N_GROUPS, 3, ATTN_DIM)
    w_attn = w_attn * jnp.array([ATTN_HEAD_DIM ** -0.5, 1.0, 1.0], F32)[None, None, :, None]
    w_main = jnp.concatenate([w_in[:, c0:c2], w_in[:, c3:], w_attn.reshape(D_MODEL, c0)], axis=1).astype(BF16)
    w_gate = jnp.pad(w_in[:, c2:c3], ((0, 0), (0, LANES - ML_GATE_COLS))).astype(BF16)
    return dict(
        w_main=w_main, w_gate=w_gate, conv_w=conv_w, conv_b=conv_b.reshape(1, -1),
        gate_b=gate_b.reshape(-1, 1), wpa=wpa.astype(BF16), wpm=wpm.astype(BF16), wo=wo.astype(BF16),
        nmw=nmw.reshape(1, -1), nfw=nfw.reshape(1, -1), rw_t=rw.T, rb=rb.reshape(-1, 1),
        wgu=wgu.astype(BF16), bgu=bgu.reshape(N_EXPERTS, 1, -1), wd=wd.astype(BF16),
        bd=bd.reshape(N_EXPERTS, 1, -1))


def _layer(x, p, tables):
    b, s, _ = x.shape
    x2d = x.reshape(b * s, D_MODEL)
    proj, gates, qkv1, qkv2 = _inproj(x2d, p["nmw"], p["w_main"], p["w_gate"], tables[0], tables[1], s)
    proj3 = proj.reshape(b, s, N_MAIN)

    merged = _attn_plain(proj3)
    for g, qkv in ((1, qkv1), (2, qkv2)):
        merged = _attn_dilated(qkv, g, ATTN_PATTERNS[g][1], merged, b, s)
    attn_o = merged[0]

    ml_q, ml_kt = _conv_qk(proj3, p["conv_w"], p["conv_b"])
    gates_t = gates[:, :ML_GATE_COLS].reshape(b, s, ML_GATE_COLS).transpose(0, 2, 1)
    g_rows = _gate_prep(gates_t, p["gate_b"])
    g_cols = g_rows.transpose(0, 2, 1)
    h_f, h_b = _mlstm(proj3, ml_q, ml_kt, g_rows, g_cols)

    return _merge(x2d, attn_o, h_f.reshape(b * s, ML_DIM), h_b.reshape(b * s, ML_DIM), proj,
                  p["wpa"], p["wpm"], p["wo"], p["nfw"], p["rw_t"], p["rb"])


def kernel(x_prompt, x_sample, norm_mix_w, w_in, mlstm_conv_w, mlstm_conv_b, mlstm_gate_b, w_proj_attn,
           w_proj_mlstm, w_out, norm_ffn_w, router_w, router_b, expert_w_gu, expert_b_gu, expert_w_down,
           expert_b_down, norm_final_w):
    depth = w_in.shape[0]
    assert depth == 1, "the final RMSNorm is fused into the last layer's MoE kernel"
    p = _pack_layer(w_in[0], mlstm_conv_w[0], mlstm_conv_b[0], mlstm_gate_b[0], w_proj_attn[0],
                    w_proj_mlstm[0], w_out[0], norm_mix_w[0], norm_ffn_w[0], router_w[0], router_b[0],
                    expert_w_gu[0], expert_b_gu[0], expert_w_down[0], expert_b_down[0])
    final_w = norm_final_w.reshape(1, -1)
    xs = (x_prompt, x_sample)
    parts = [_layer(x, p, _rope_tables(x.shape[1])) for x in xs]
    outs = _moe(parts, p, final_w)
    return tuple(o.reshape(x.shape) for o, x in zip(outs, xs))
```

```python
import functools

import jax
import jax.numpy as jnp
from jax import lax
from jax.experimental import pallas as pl
from jax.experimental.pallas import tpu as pltpu

F32 = jnp.float32
BF16 = jnp.bfloat16

D_MODEL = 1024
ATTN_PATTERNS = ((128, 1), (512, 4), (2048, 16))
N_GROUPS = 3
ATTN_HEADS = 8
ATTN_HEAD_DIM = 64
ATTN_DIM = ATTN_HEADS * ATTN_HEAD_DIM
ATTN_BLOCK = 64
ROPE_THETA = 10000.0
ML_DIM = D_MODEL
ML_HEADS = 4
ML_HEAD_DIM = ML_DIM // ML_HEADS
ML_CHUNK = 256
CONV_W = 5
ATTN_QKV_COLS = N_GROUPS * 3 * ATTN_DIM
ML_QKV_COLS = 3 * ML_DIM
ML_O_COLS = ML_DIM
ML_GATE_COLS = 4 * ML_HEADS
MERGE_COLS = 2 * D_MODEL
N_EXPERTS = 32
TOP_K = 4
D_FF = D_MODEL
SWIGLU_LIMIT = 7.0
SWIGLU_ALPHA = 1.702
RMS_EPS = 1e-5
NEG_INF = -1e30

LANES = 128
SUBLANES = 8
COL_TILE = 512
N_PACKED = ML_QKV_COLS + ML_O_COLS + MERGE_COLS + ATTN_QKV_COLS
N_COL_TILES = N_PACKED // COL_TILE
ATTN_TILE0 = (ML_QKV_COLS + ML_O_COLS + MERGE_COLS) // COL_TILE
GROUP_TILES = 3 * ATTN_DIM // COL_TILE
N_MAIN_TILES = ATTN_TILE0 + GROUP_TILES
N_MAIN = N_MAIN_TILES * COL_TILE
ML_EXT = ML_HEAD_DIM + LANES


def _params(sem, vmem_mb):
    return pltpu.CompilerParams(dimension_semantics=sem, vmem_limit_bytes=vmem_mb * 1024 * 1024)


def _sigmoid(x):
    return 1.0 / (1.0 + jnp.exp(-x))


INPROJ_CHUNK = 512
WIDE_TILE = 3 * ATTN_DIM
N_WIDE_TILES = N_PACKED // WIDE_TILE
N_PLAIN_WIDE = ATTN_TILE0 * COL_TILE // WIDE_TILE


def _inproj_kernel(x_ref, nw_ref, w_ref, wg_ref, cos_ref, sin_ref, out_ref, gates_ref, g1_ref, g2_ref,
                   h_scr, slab, slab2):
    j = pl.program_id(1)
    tm = x_ref.shape[0]
    n_slabs = WIDE_TILE // LANES

    @pl.when(j == 0)
    def _():
        x = x_ref[...]
        ms = jnp.mean(x * x, axis=-1, keepdims=True)
        h = (x * lax.rsqrt(ms + RMS_EPS) * nw_ref[...]).astype(BF16)
        h_scr[...] = h
        gates_ref[...] = jnp.dot(h, wg_ref[...], preferred_element_type=F32)

    chunk = min(INPROJ_CHUNK, tm)

    def chunks():
        for c in range(tm // chunk):
            rows = slice(c * chunk, (c + 1) * chunk)
            yield c, rows, jnp.dot(h_scr[rows, :], w_ref[...], preferred_element_type=F32)

    def rope(a, rows):
        reps = ATTN_DIM // LANES
        c = jnp.concatenate([cos_ref[rows, :]] * reps, axis=1)
        s = jnp.concatenate([sin_ref[rows, :]] * reps, axis=1)
        lane = lax.broadcasted_iota(jnp.int32, a.shape, 1)
        half = ATTN_HEAD_DIM // 2
        first = (lane % ATTN_HEAD_DIM) < half
        sw = jnp.where(first, pltpu.roll(a, ATTN_DIM - half, 1), pltpu.roll(a, half, 1))
        return a * c + sw * s

    def rotated(acc, rows):
        return jnp.concatenate([rope(acc[:, :ATTN_DIM], rows), rope(acc[:, ATTN_DIM:2 * ATTN_DIM], rows),
                                acc[:, 2 * ATTN_DIM:]], axis=1)

    def deinterleave(val, c, dst_ref, dilation):
        n = chunk // dilation
        for s in range(n_slabs):
            slab[s] = val[:, s * LANES:(s + 1) * LANES]
        if dilation == 16:
            quarter = chunk // 4
            for s in range(n_slabs):
                for r4 in range(4):
                    slab2[s, r4 * quarter:(r4 + 1) * quarter, :] = slab[s, pl.ds(r4, quarter, stride=4), :]
            for r in range(dilation):
                start = (r % 4) * quarter + r // 4
                piece = jnp.concatenate([slab2[s, pl.ds(start, n, stride=4), :] for s in range(n_slabs)], axis=1)
                dst_ref[0, r, c * n:(c + 1) * n, :] = piece.astype(BF16)
            return
        for r in range(dilation):
            piece = jnp.concatenate([slab[s, pl.ds(r, n, stride=dilation), :] for s in range(n_slabs)], axis=1)
            dst_ref[0, r, c * n:(c + 1) * n, :] = piece.astype(BF16)

    @pl.when(j < N_PLAIN_WIDE)
    def _():
        for _, rows, acc in chunks():
            out_ref[rows, :] = acc.astype(BF16)

    dst = (out_ref, g1_ref, g2_ref)
    for g, (_, dilation) in enumerate(ATTN_PATTERNS):
        @pl.when(j == N_PLAIN_WIDE + g)
        def _(g=g, dilation=dilation):
            for c, rows, acc in chunks():
                val = rotated(acc, rows)
                if dilation == 1:
                    out_ref[rows, :] = val.astype(BF16)
                else:
                    deinterleave(val, c, dst[g], dilation)


def _inproj(x2d, nw, w_main, w_gate, cos_t, sin_t, seq):
    t = x2d.shape[0]
    tm = min(1024, seq)
    tiles_per_seq = seq // tm
    chunk = min(INPROJ_CHUNK, tm)

    def group_out(dilation):
        return (pl.BlockSpec((1, dilation, tm // dilation, WIDE_TILE), lambda i, j: (i, 0, 0, 0)),
                jax.ShapeDtypeStruct((t // tm, dilation, tm // dilation, WIDE_TILE), BF16))

    (g1_spec, g1_shape), (g2_spec, g2_shape) = [group_out(ATTN_PATTERNS[g][1]) for g in (1, 2)]
    return pl.pallas_call(
        _inproj_kernel,
        grid=(t // tm, N_WIDE_TILES),
        in_specs=[
            pl.BlockSpec((tm, D_MODEL), lambda i, j: (i, 0)),
            pl.BlockSpec((1, D_MODEL), lambda i, j: (0, 0)),
            pl.BlockSpec((D_MODEL, WIDE_TILE), lambda i, j: (0, j)),
            pl.BlockSpec((D_MODEL, LANES), lambda i, j: (0, 0)),
            pl.BlockSpec((tm, LANES), lambda i, j: (i % tiles_per_seq, 0)),
            pl.BlockSpec((tm, LANES), lambda i, j: (i % tiles_per_seq, 0)),
        ],
        out_specs=[
            pl.BlockSpec((tm, WIDE_TILE), lambda i, j: (i, jnp.minimum(j, N_PLAIN_WIDE))),
            pl.BlockSpec((tm, LANES), lambda i, j: (i, 0)),
            g1_spec, g2_spec,
        ],
        out_shape=[
            jax.ShapeDtypeStruct((t, N_MAIN), BF16),
            jax.ShapeDtypeStruct((t, LANES), F32),
            g1_shape, g2_shape,
        ],
        scratch_shapes=[pltpu.VMEM((tm, D_MODEL), BF16), pltpu.VMEM((WIDE_TILE // LANES, chunk, LANES), F32),
                        pltpu.VMEM((WIDE_TILE // LANES, chunk, LANES), F32)],
        compiler_params=_params(("arbitrary", "arbitrary"), 56),
        name="inproj",
    )(x2d, nw, w_main, w_gate, cos_t, sin_t)


ATTN_QUERY_TILE = 128


def _window_mask(n, qt, sub_len):
    kt = qt + 2 * ATTN_BLOCK
    qpos = n * qt + lax.broadcasted_iota(jnp.int32, (qt, kt), 0)
    kpos = n * qt - ATTN_BLOCK + lax.broadcasted_iota(jnp.int32, (qt, kt), 1)
    return (jnp.abs(kpos - qpos) <= ATTN_BLOCK) & (kpos >= 0) & (kpos < sub_len)


def _attend(q, k, v, valid):
    qt = q.shape[0]
    pairs = range(ATTN_HEADS // 2)
    lane = lax.broadcasted_iota(jnp.int32, (qt, LANES), 1)
    lo = lane < ATTN_HEAD_DIM
    valid2 = jnp.concatenate([valid, valid], axis=0)
    scores = []
    for p in pairs:
        qp = q[:, p * LANES:(p + 1) * LANES]
        zero = jnp.zeros_like(qp)
        stacked = jnp.concatenate([jnp.where(lo, qp, zero), jnp.where(lo, zero, qp)], axis=0)
        scores.append(lax.dot_general(stacked, k[:, p * LANES:(p + 1) * LANES], (((1,), (1,)), ((), ())),
                                      preferred_element_type=F32))
    probs, inv_den, lses = [], [], []
    for p in pairs:
        s = jnp.where(valid2, scores[p], NEG_INF)
        mx = jnp.max(s, axis=-1, keepdims=True)
        e = jnp.exp(s - mx)
        den = jnp.sum(e, axis=-1, keepdims=True)
        probs.append(e.astype(BF16))
        inv_den.append(1.0 / den)
        lses.append(mx + jnp.log(den))
    pvs = [jnp.dot(probs[p], v[:, p * LANES:(p + 1) * LANES], preferred_element_type=F32) for p in pairs]
    lse_out = jnp.zeros((qt, LANES), F32)
    outs = []
    for p in pairs:
        o = pvs[p] * inv_den[p]
        outs.append(jnp.where(lo, o[:qt], o[qt:]))
        lse_out = jnp.where(lane == 2 * p, lses[p][:qt], jnp.where(lane == 2 * p + 1, lses[p][qt:], lse_out))
    return outs, lse_out


def _head_expander():
    row = lax.broadcasted_iota(jnp.int32, (LANES, ATTN_DIM), 0)
    col = lax.broadcasted_iota(jnp.int32, (LANES, ATTN_DIM), 1)
    return jnp.where(col // ATTN_HEAD_DIM == row, 1.0, 0.0).astype(BF16)


def _merge_groups(o_cur, lse_cur, o_prev, lse_prev, expander):
    m = jnp.maximum(lse_prev, lse_cur)
    a = jnp.exp(lse_prev - m)
    tot = a + jnp.exp(lse_cur - m)
    w_prev = a / tot
    hi = w_prev.astype(BF16)
    lo = (w_prev - hi.astype(F32)).astype(BF16)
    spread = (jnp.dot(hi, expander, preferred_element_type=F32) + jnp.dot(lo, expander, preferred_element_type=F32))
    outs = [o_cur[p] + spread[:, p * LANES:(p + 1) * LANES] * (o_prev[p] - o_cur[p])
            for p in range(ATTN_HEADS // 2)]
    lane = lax.broadcasted_iota(jnp.int32, m.shape, 1)
    return outs, jnp.where(lane < ATTN_HEADS, m + jnp.log(tot), 0.0)


def _attn_kernel(q_ref, kp_ref, kc_ref, kn_ref, vp_ref, vc_ref, vn_ref, o_ref, l_ref, *, sub_len, qt):
    n_sub = q_ref.shape[1] // qt
    k = jnp.concatenate([kp_ref[0], kc_ref[0], kn_ref[0]], axis=0)
    v = jnp.concatenate([vp_ref[0], vc_ref[0], vn_ref[0]], axis=0)
    for sub in range(n_sub):
        valid = _window_mask(pl.program_id(1) * n_sub + sub, qt, sub_len)
        rows = slice(sub * qt, (sub + 1) * qt)
        keys = slice(sub * qt, (sub + 1) * qt + 2 * ATTN_BLOCK)
        outs, lse = _attend(q_ref[0, rows, :], k[keys], v[keys], valid)
        for p, o in enumerate(outs):
            o_ref[0, rows, p * LANES:(p + 1) * LANES] = o.astype(BF16)
        l_ref[0, rows, :] = lse


MERGE_ROWS = 256
RESIDUE_ROWS_PER_ITER = 512


def _attn_dilated_kernel(q_ref, kp_ref, kc_ref, kn_ref, vp_ref, vc_ref, vn_ref, op_ref, lp_ref, o_ref, l_ref,
                         o_slab, l_slab, *, sub_len, qt, dilation):
    valid = _window_mask(pl.program_id(1), qt, sub_len)

    per_iter = min(dilation, RESIDUE_ROWS_PER_ITER // qt)

    def residues(i, c):
        for r in [per_iter * i + u for u in range(per_iter)]:
            k = jnp.concatenate([kp_ref[0, r], kc_ref[0, r], kn_ref[0, r]], axis=0)
            v = jnp.concatenate([vp_ref[0, r], vc_ref[0, r], vn_ref[0, r]], axis=0)
            outs, lse = _attend(q_ref[0, r], k, v, valid)
            for p, o in enumerate(outs):
                o_slab[p, pl.ds(r, qt, stride=dilation), :] = o
            l_slab[pl.ds(r, qt, stride=dilation), :] = lse
        return c

    lax.fori_loop(0, dilation // per_iter, residues, 0)
    expander = _head_expander()

    def merge(i, c):
        rows = pl.ds(pl.multiple_of(i * MERGE_ROWS, MERGE_ROWS), MERGE_ROWS)
        prev = op_ref[rows, :].astype(F32)
        o_cur = [o_slab[p, rows, :] for p in range(ATTN_HEADS // 2)]
        o_prev = [prev[:, p * LANES:(p + 1) * LANES] for p in range(ATTN_HEADS // 2)]
        outs, lse = _merge_groups(o_cur, l_slab[rows, :], o_prev, lp_ref[rows, :], expander)
        for p, o in enumerate(outs):
            o_ref[rows, p * LANES:(p + 1) * LANES] = o.astype(BF16)
        l_ref[rows, :] = lse
        return c

    lax.fori_loop(0, qt * dilation // MERGE_ROWS, merge, 0)


PLAIN_STEP_TILES = 2


def _attn_plain(proj3):
    b, s, _ = proj3.shape
    qt = min(ATTN_QUERY_TILE, s)
    step = min(PLAIN_STEP_TILES * qt, s)
    nblk = s // ATTN_BLOCK
    qb = step // ATTN_BLOCK

    def cur(which):
        return pl.BlockSpec((1, step, COL_TILE), lambda bi, n: (bi, n, ATTN_TILE0 + which))

    def before(which):
        return pl.BlockSpec((1, ATTN_BLOCK, COL_TILE),
                            lambda bi, n: (bi, jnp.maximum(n * qb - 1, 0), ATTN_TILE0 + which))

    def after(which):
        return pl.BlockSpec((1, ATTN_BLOCK, COL_TILE),
                            lambda bi, n: (bi, jnp.minimum((n + 1) * qb, nblk - 1), ATTN_TILE0 + which))

    o, l = pl.pallas_call(
        functools.partial(_attn_kernel, sub_len=s, qt=qt),
        grid=(b, s // step),
        in_specs=[cur(0), before(1), cur(1), after(1), before(2), cur(2), after(2)],
        out_specs=[pl.BlockSpec((1, step, ATTN_DIM), lambda bi, n: (bi, n, 0)),
                   pl.BlockSpec((1, step, LANES), lambda bi, n: (bi, n, 0))],
        out_shape=[jax.ShapeDtypeStruct((b, s, ATTN_DIM), BF16), jax.ShapeDtypeStruct((b, s, LANES), F32)],
        compiler_params=_params(("arbitrary", "arbitrary"), 48),
        name="attn_g0",
    )(*([proj3] * 7))
    return o.reshape(b * s, ATTN_DIM), l.reshape(b * s, LANES)


def _attn_dilated(qkv, g, dilation, prev, b, s):
    n_tiles, _, rows, _ = qkv.shape
    tiles_per_seq = n_tiles // b
    sub_len = s // dilation
    qt = min(ATTN_QUERY_TILE, rows)
    span = qt * dilation
    q_per_tile = rows // qt
    h_per_tile = rows // ATTN_BLOCK
    qb = qt // ATTN_BLOCK
    nblk = sub_len // ATTN_BLOCK

    def cur(which):
        return pl.BlockSpec((1, dilation, qt, COL_TILE),
                            lambda bi, n: (bi * tiles_per_seq + n // q_per_tile, 0, n % q_per_tile, which))

    def halo(which, blk_of):
        def index(bi, n):
            blk = blk_of(n)
            return (bi * tiles_per_seq + blk // h_per_tile, 0, blk % h_per_tile, which)
        return pl.BlockSpec((1, dilation, ATTN_BLOCK, COL_TILE), index)

    before = lambda which: halo(which, lambda n: jnp.maximum(n * qb - 1, 0))
    after = lambda which: halo(which, lambda n: jnp.minimum((n + 1) * qb, nblk - 1))
    steps = sub_len // qt
    o_spec = pl.BlockSpec((span, ATTN_DIM), lambda bi, n: (bi * steps + n, 0))
    l_spec = pl.BlockSpec((span, LANES), lambda bi, n: (bi * steps + n, 0))
    return pl.pallas_call(
        functools.partial(_attn_dilated_kernel, sub_len=sub_len, qt=qt, dilation=dilation),
        grid=(b, steps),
        in_specs=[cur(0), before(1), cur(1), after(1), before(2), cur(2), after(2), o_spec, l_spec],
        out_specs=[o_spec, l_spec],
        out_shape=[jax.ShapeDtypeStruct((b * s, ATTN_DIM), BF16), jax.ShapeDtypeStruct((b * s, LANES), F32)],
        scratch_shapes=[pltpu.VMEM((ATTN_HEADS // 2, span, LANES), F32), pltpu.VMEM((span, LANES), F32)],
        compiler_params=_params(("arbitrary", "arbitrary"), 56),
        name=f"attn_g{g}",
    )(*([qkv] * 7), prev[0], prev[1])


CONV_HALO = 16


Q_TILES = ML_DIM // COL_TILE


def _conv_kernel(xp_ref, xc_ref, xn_ref, w_ref, b_ref, q_ref, kt_ref, buf, *, tm):
    i = pl.program_id(1)
    c = pl.program_id(2)
    last = pl.num_programs(1) - 1
    before = jnp.where(i > 0, xp_ref[0].astype(F32), 0.0)
    centre = xc_ref[0].astype(F32)
    after = jnp.where(i < last, xn_ref[0].astype(F32), 0.0)
    w = w_ref[...]
    bias = b_ref[...]
    pieces = []
    for s in range(COL_TILE // LANES):
        lanes = slice(s * LANES, (s + 1) * LANES)
        buf[s, 0:CONV_HALO, :] = before[:, lanes]
        buf[s, CONV_HALO:CONV_HALO + tm, :] = centre[:, lanes]
        buf[s, CONV_HALO + tm:, :] = after[:, lanes]
        acc = jnp.broadcast_to(bias[:, lanes], (tm, LANES))
        for tap in range(CONV_W):
            acc = acc + w[tap:tap + 1, lanes] * buf[s, pl.ds(CONV_HALO - CONV_W // 2 + tap, tm, stride=1), :]
        pieces.append(acc)
    y = jnp.concatenate(pieces, axis=1)
    y = y * _sigmoid(y)

    @pl.when(c < Q_TILES)
    def _():
        q_ref[0] = y.astype(BF16)

    @pl.when(c >= Q_TILES)
    def _():
        kt_ref[0] = (y * ML_HEAD_DIM ** -0.5).T.astype(BF16)


def _conv_qk(proj3, conv_w, conv_b):
    b, s, _ = proj3.shape
    tm = min(1024, s)
    hb = tm // CONV_HALO
    nh = s // CONV_HALO
    return pl.pallas_call(
        functools.partial(_conv_kernel, tm=tm),
        grid=(b, s // tm, 2 * Q_TILES),
        in_specs=[
            pl.BlockSpec((1, CONV_HALO, COL_TILE), lambda bi, i, c: (bi, jnp.maximum(i * hb - 1, 0), c)),
            pl.BlockSpec((1, tm, COL_TILE), lambda bi, i, c: (bi, i, c)),
            pl.BlockSpec((1, CONV_HALO, COL_TILE), lambda bi, i, c: (bi, jnp.minimum((i + 1) * hb, nh - 1), c)),
            pl.BlockSpec((CONV_W, COL_TILE), lambda bi, i, c: (0, c)),
            pl.BlockSpec((1, COL_TILE), lambda bi, i, c: (0, c)),
        ],
        out_specs=[
            pl.BlockSpec((1, tm, COL_TILE), lambda bi, i, c: (bi, i, jnp.minimum(c, Q_TILES - 1))),
            pl.BlockSpec((1, COL_TILE, tm), lambda bi, i, c: (bi, jnp.maximum(c - Q_TILES, 0), i)),
        ],
        out_shape=[jax.ShapeDtypeStruct((b, s, ML_DIM), BF16), jax.ShapeDtypeStruct((b, ML_DIM, s), BF16)],
        scratch_shapes=[pltpu.VMEM((COL_TILE // LANES, tm + 2 * CONV_HALO, LANES), F32)],
        compiler_params=_params(("arbitrary", "arbitrary", "arbitrary"), 32),
        name="conv_qk",
    )(proj3, proj3, proj3, conv_w, conv_b)


def _gate_kernel(g_ref, b_ref, o_ref):
    g = g_ref[0] + b_ref[...]
    width = g.shape[1]
    logsig = jnp.minimum(g, 0.0) - jnp.log(1.0 + jnp.exp(-jnp.abs(g)))
    lane = lax.broadcasted_iota(jnp.int32, g.shape, 1) % ML_CHUNK
    pre = logsig
    suf = logsig
    step = 1
    while step < ML_CHUNK:
        pre = pre + jnp.where(lane >= step, pltpu.roll(pre, step, 1), 0.0)
        suf = suf + jnp.where(lane < ML_CHUNK - step, pltpu.roll(suf, width - step, 1), 0.0)
        step *= 2
    row = lax.broadcasted_iota(jnp.int32, g.shape, 0)
    is_f_fwd = (row >= ML_HEADS) & (row < 2 * ML_HEADS)
    is_f_bwd = row >= 3 * ML_HEADS
    o_ref[0] = jnp.where(is_f_fwd, pre, jnp.where(is_f_bwd, suf, g))


def _gate_prep(gates_t, gate_b):
    b, rows, s = gates_t.shape
    sb = min(2048, s)
    return pl.pallas_call(
        _gate_kernel,
        grid=(b, s // sb),
        in_specs=[
            pl.BlockSpec((1, rows, sb), lambda bi, i: (bi, 0, i)),
            pl.BlockSpec((rows, 1), lambda bi, i: (0, 0)),
        ],
        out_specs=pl.BlockSpec((1, rows, sb), lambda bi, i: (bi, 0, i)),
        out_shape=jax.ShapeDtypeStruct((b, rows, s), F32),
        compiler_params=_params(("arbitrary", "arbitrary"), 32),
        name="gate_prep",
    )(gates_t, gate_b)


ML_CHAIN_GROUP = 8


def _mlstm_kernel(qf_ref, kf_ref, vf_ref, qb_ref, kb_ref, vb_ref, grf_ref, grb_ref, gcf_ref, gcb_ref,
                  hf_ref, hb_ref, s_scr, m_scr):
    c = pl.program_id(1)

    @pl.when(c == 0)
    def _():
        def clear(i, carry):
            s_scr[i] = jnp.zeros(s_scr.shape[1:], F32)
            return carry

        lax.fori_loop(0, s_scr.shape[0], clear, 0)
        m_scr[...] = jnp.zeros_like(m_scr)

    t_i = lax.broadcasted_iota(jnp.int32, (ML_CHUNK, ML_CHUNK), 0)
    s_i = lax.broadcasted_iota(jnp.int32, (ML_CHUNK, ML_CHUNK), 1)
    ones_col = jnp.where(lax.broadcasted_iota(jnp.int32, (ML_CHUNK, LANES), 1) == 0, 1.0, 0.0).astype(BF16)
    dirs = ((qf_ref, kf_ref, vf_ref, grf_ref, gcf_ref, hf_ref), (qb_ref, kb_ref, vb_ref, grb_ref, gcb_ref, hb_ref))
    all_chains = [(dirn, head) + refs for dirn, refs in enumerate(dirs) for head in range(ML_HEADS)]

    for first in range(0, len(all_chains), ML_CHAIN_GROUP):
        chains = all_chains[first:first + ML_CHAIN_GROUP]
        matmuls = []
        for dirn, head, q_ref, k_ref, v_ref, _, _, _ in chains:
            hs = slice(head * ML_HEAD_DIM, (head + 1) * ML_HEAD_DIM)
            q = q_ref[0, :, hs]
            qk_raw = jnp.dot(q, k_ref[0, hs, :], preferred_element_type=F32)
            q_state = jnp.dot(q, s_scr[dirn * ML_HEADS + head].astype(BF16), preferred_element_type=F32)
            matmuls.append((qk_raw, q_state))

        weights = []
        for dirn, head, _, _, _, gr_ref, gc_ref, _ in chains:
            ii = dirn * 2 * ML_HEADS + head
            bi = ii + ML_HEADS
            mask = (s_i <= t_i) if dirn == 0 else (s_i >= t_i)
            i_row, b_row = gr_ref[0, ii:ii + 1, :], gr_ref[0, bi:bi + 1, :]
            b_col = gc_ref[0, :, bi:bi + 1]
            m = m_scr[dirn * ML_HEADS + head][0:1, 0:1]
            dmat = jnp.where(mask, b_col - b_row + i_row, NEG_INF)
            inter = b_col + m
            m_t = jnp.maximum(inter, jnp.max(dmat, axis=-1, keepdims=True))
            weights.append((jnp.exp(dmat - m_t), jnp.exp(inter - m_t), jnp.exp(-m_t)))

        for (dirn, head, _, _, v_ref, _, _, h_ref), (qk_raw, q_state), (w_intra, w_inter, floor) in zip(
                chains, matmuls, weights):
            hs = slice(head * ML_HEAD_DIM, (head + 1) * ML_HEAD_DIM)
            v_ext = jnp.concatenate([v_ref[0, :, hs], ones_col], axis=1)
            num = w_inter * q_state + jnp.dot((qk_raw * w_intra).astype(BF16), v_ext, preferred_element_type=F32)
            den = jnp.maximum(jnp.abs(num[:, ML_HEAD_DIM:ML_HEAD_DIM + 1]), floor)
            h_ref[0, :, hs] = (num[:, :ML_HEAD_DIM] / den).astype(BF16)

        for dirn, head, _, k_ref, v_ref, gr_ref, _, _ in chains:
            idx = dirn * ML_HEADS + head
            ii = dirn * 2 * ML_HEADS + head
            bi = ii + ML_HEADS
            hs = slice(head * ML_HEAD_DIM, (head + 1) * ML_HEAD_DIM)
            i_row, b_row = gr_ref[0, ii:ii + 1, :], gr_ref[0, bi:bi + 1, :]
            b_last = b_row[:, ML_CHUNK - 1:ML_CHUNK] if dirn == 0 else b_row[:, 0:1]
            m = m_scr[idx][0:1, 0:1]
            log_w = b_last - b_row + i_row
            m_new = jnp.maximum(b_last + m, jnp.max(log_w, axis=-1, keepdims=True))
            decay = jnp.exp(b_last + m - m_new)
            v_ext = jnp.concatenate([v_ref[0, :, hs], ones_col], axis=1)
            wk_t = (k_ref[0, hs, :].astype(F32) * jnp.exp(log_w - m_new)).astype(BF16)
            s_scr[idx] = decay * s_scr[idx] + jnp.dot(wk_t, v_ext, preferred_element_type=F32)
            m_scr[idx] = jnp.broadcast_to(m_new, m_scr.shape[1:])


def _mlstm(proj3, q, k_t, g_rows, g_cols):
    b, s, _ = proj3.shape
    nc = s // ML_CHUNK
    v_tile = 2 * ML_DIM // ML_DIM
    n_rows = g_rows.shape[1]

    def rows(col, reverse):
        return pl.BlockSpec((1, ML_CHUNK, ML_DIM), lambda bi, c: (bi, nc - 1 - c if reverse else c, col))

    def cols(height, reverse):
        return pl.BlockSpec((1, height, ML_CHUNK), lambda bi, c: (bi, 0, nc - 1 - c if reverse else c))

    return pl.pallas_call(
        _mlstm_kernel,
        grid=(b, nc),
        in_specs=[
            rows(0, False), cols(ML_DIM, False), rows(v_tile, False),
            rows(0, True), cols(ML_DIM, True), rows(v_tile, True),
            cols(n_rows, False), cols(n_rows, True),
            pl.BlockSpec((1, ML_CHUNK, n_rows), lambda bi, c: (bi, c, 0)),
            pl.BlockSpec((1, ML_CHUNK, n_rows), lambda bi, c: (bi, nc - 1 - c, 0)),
        ],
        out_specs=[rows(0, False), rows(0, True)],
        out_shape=[jax.ShapeDtypeStruct((b, s, ML_DIM), BF16)] * 2,
        scratch_shapes=[
            pltpu.VMEM((2 * ML_HEADS, ML_HEAD_DIM, ML_EXT), F32),
            pltpu.VMEM((2 * ML_HEADS, 8, LANES), F32),
        ],
        compiler_params=_params(("arbitrary", "arbitrary"), 32),
        name="mlstm",
    )(q, k_t, proj3, q, k_t, proj3, g_rows, g_rows, g_cols, g_cols)


def _pack_bf16_pair(lo, hi):
    lo_bits = lax.bitcast_convert_type(lo.astype(BF16).astype(F32), jnp.uint32)
    hi_bits = lax.bitcast_convert_type(hi.astype(BF16).astype(F32), jnp.uint32)
    return (hi_bits & jnp.uint32(0xFFFF0000)) | (lo_bits >> 16)


def _unpack_bf16_pair(packed):
    lo = lax.bitcast_convert_type(packed << 16, F32)
    hi = lax.bitcast_convert_type(packed & jnp.uint32(0xFFFF0000), F32)
    return lo, hi


MERGE_CHUNK = 512


def _merge_kernel(x_ref, ao_ref, hf_ref, hb_ref, mo_ref, mg_ref, wpa_ref, wpm_ref, wo_ref, nfw_ref,
                  rw_ref, rb_ref, x1_ref, h2_ref, idx_ref, wt_ref):
    tm = x_ref.shape[0]
    chunk = min(MERGE_CHUNK, tm)
    for c in range(tm // chunk):
        rows = slice(c * chunk, (c + 1) * chunk)
        y_attn = jnp.dot(ao_ref[rows, :], wpa_ref[...], preferred_element_type=F32)
        hsum = hf_ref[rows, :].astype(F32) + hb_ref[rows, :].astype(F32)
        ml = (_sigmoid(mo_ref[rows, :]).astype(F32) * hsum).astype(BF16)
        y_ml = jnp.dot(ml, wpm_ref[...], preferred_element_type=F32)
        gates = _sigmoid(mg_ref[rows, :]).astype(F32)
        mixed = (gates[:, :D_MODEL] * y_attn + gates[:, D_MODEL:] * y_ml).astype(BF16)
        x1 = x_ref[rows, :] + jnp.dot(mixed, wo_ref[...], preferred_element_type=F32)
        x1_ref[rows, :] = x1
        ms = jnp.mean(x1 * x1, axis=-1, keepdims=True)
        h2 = x1 * lax.rsqrt(ms + RMS_EPS) * nfw_ref[...]
        h2_ref[rows, :] = _pack_bf16_pair(h2[:, :D_MODEL // 2], h2[:, D_MODEL // 2:])

        logits = lax.dot_general(rw_ref[...], h2, (((1,), (1,)), ((), ())), preferred_element_type=F32,
                                 precision=lax.Precision.HIGHEST) + rb_ref[...]
        row = lax.broadcasted_iota(jnp.int32, logits.shape, 0)
        rest = logits
        vals = []
        for k in range(TOP_K):
            mx = jnp.max(rest, axis=0, keepdims=True)
            first = jnp.min(jnp.where(rest == mx, row, N_EXPERTS), axis=0, keepdims=True)
            vals.append(mx)
            idx_ref[k:k + 1, rows] = first
            rest = jnp.where(row == first, -jnp.inf, rest)
        exps = [jnp.exp(v - vals[0]) for v in vals]
        tot = exps[0] + exps[1] + exps[2] + exps[3]
        for k in range(TOP_K):
            wt_ref[k:k + 1, rows] = exps[k] / tot


def _merge(x2d, attn_o, h_f, h_b, proj, wpa, wpm, wo, nfw, rw_t, rb):
    t = x2d.shape[0]
    tm = min(512, t)
    ml_o_tile = ML_QKV_COLS // ML_O_COLS
    merge_tile = (ML_QKV_COLS + ML_O_COLS) // MERGE_COLS
    row = lambda width: pl.BlockSpec((tm, width), lambda i: (i, 0))
    full = lambda a: pl.BlockSpec(a.shape, lambda i: (0, 0))
    return pl.pallas_call(
        _merge_kernel,
        grid=(t // tm,),
        in_specs=[
            row(D_MODEL), row(ATTN_DIM), row(ML_DIM), row(ML_DIM),
            pl.BlockSpec((tm, ML_O_COLS), lambda i: (i, ml_o_tile)),
            pl.BlockSpec((tm, MERGE_COLS), lambda i: (i, merge_tile)),
            full(wpa), full(wpm), full(wo), full(nfw), full(rw_t), full(rb),
        ],
        out_specs=[row(D_MODEL), row(D_MODEL // 2), pl.BlockSpec((TOP_K, tm), lambda i: (0, i)),
                   pl.BlockSpec((TOP_K, tm), lambda i: (0, i))],
        out_shape=[
            jax.ShapeDtypeStruct((t, D_MODEL), F32),
            jax.ShapeDtypeStruct((t, D_MODEL // 2), jnp.uint32),
            jax.ShapeDtypeStruct((TOP_K, t), jnp.int32),
            jax.ShapeDtypeStruct((TOP_K, t), F32),
        ],
        compiler_params=_params(("arbitrary",), 48),
        name="merge",
    )(x2d, attn_o, h_f, h_b, proj, proj, wpa, wpm, wo, nfw, rw_t, rb)


ROUTE_TILE = 512


def _route_kernel(idx_ref, pos_ref, cnt_ref, base):
    @pl.when(pl.program_id(0) == 0)
    def _():
        base[...] = jnp.zeros_like(base)

    idx = idx_ref[...]
    row = lax.broadcasted_iota(jnp.int32, (N_EXPERTS, ROUTE_TILE), 0)
    onehot = jnp.zeros((N_EXPERTS, ROUTE_TILE), F32)
    for k in range(TOP_K):
        onehot = onehot + jnp.where(row == idx[k:k + 1, :], 1.0, 0.0)
    s_i = lax.broadcasted_iota(jnp.int32, (ROUTE_TILE, ROUTE_TILE), 0)
    t_i = lax.broadcasted_iota(jnp.int32, (ROUTE_TILE, ROUTE_TILE), 1)
    upper = jnp.where(s_i <= t_i, 1.0, 0.0).astype(BF16)
    incl = jnp.dot(onehot.astype(BF16), upper, preferred_element_type=F32)
    before = base[:, 0:1]
    count = incl + before
    for k in range(TOP_K):
        mine = jnp.sum(jnp.where(row == idx[k:k + 1, :], count, 0.0), axis=0, keepdims=True)
        pos_ref[k:k + 1, :] = (mine - 1.0).astype(jnp.int32)
    total = before + incl[:, ROUTE_TILE - 1:ROUTE_TILE]
    base[...] = jnp.broadcast_to(total, base.shape)
    cnt_ref[...] = jnp.broadcast_to(total, cnt_ref.shape)


def _route(idx):
    t = idx.shape[1]
    return pl.pallas_call(
        _route_kernel,
        grid=(t // ROUTE_TILE,),
        in_specs=[pl.BlockSpec((TOP_K, ROUTE_TILE), lambda i: (0, i))],
        out_specs=[pl.BlockSpec((TOP_K, ROUTE_TILE), lambda i: (0, i)),
                   pl.BlockSpec((N_EXPERTS, LANES), lambda i: (0, 0))],
        out_shape=[jax.ShapeDtypeStruct((TOP_K, t), jnp.int32),
                   jax.ShapeDtypeStruct((N_EXPERTS, LANES), F32)],
        scratch_shapes=[pltpu.VMEM((N_EXPERTS, LANES), F32)],
        compiler_params=_params(("arbitrary",), 32),
        name="route",
    )(idx)


GROUP_TILE = 512
DISPATCH_TILE = 2048
ROW_WORDS = D_MODEL // 2


def _dispatch_kernel(slot_hbm, ztile_ref, nu_ref, *refs, steps):
    srcs = refs[:len(steps)]
    dst_hbm, slot_smem, zbuf, sem = refs[len(steps):]
    i = pl.program_id(0)
    n_slots = DISPATCH_TILE * TOP_K
    n_tiles = dst_hbm.shape[0] // GROUP_TILE
    load = pltpu.make_async_copy(slot_hbm.at[pl.ds(i * n_slots, n_slots)], slot_smem, sem.at[0])
    load.start()

    @pl.when(i == 0)
    def _():
        zbuf[...] = jnp.zeros_like(zbuf)

        def fill_tile(tile):
            pltpu.make_async_copy(zbuf, dst_hbm.at[pl.ds(tile * GROUP_TILE, GROUP_TILE), :], sem.at[1]).start()

        def fill_last(e, n):
            tile = ztile_ref[e]

            @pl.when(tile >= 0)
            def _():
                fill_tile(tile)

            return n + jnp.where(tile >= 0, 1, 0)

        def fill_unused(tile, c):
            fill_tile(tile)
            return c

        n_fill = lax.fori_loop(0, N_EXPERTS, fill_last, 0)
        lax.fori_loop(nu_ref[0], n_tiles, fill_unused, 0)

        def drain(_, c):
            pltpu.make_async_copy(zbuf, dst_hbm.at[pl.ds(0, GROUP_TILE), :], sem.at[1]).wait()
            return c

        lax.fori_loop(0, n_fill + n_tiles - nu_ref[0], drain, 0)

    load.wait()

    first = 0
    for src_ref, n_steps in zip(srcs, steps):
        @pl.when((i >= first) & (i < first + n_steps))
        def _(src_ref=src_ref):
            def issue(t8, c):
                for r in range(SUBLANES):
                    src = src_ref.at[t8, pl.ds(r, 1), :]
                    for k in range(TOP_K):
                        slot = slot_smem[(t8 * SUBLANES + r) * TOP_K + k]
                        pltpu.make_async_copy(src, dst_hbm.at[pl.ds(slot, 1), :],
                                              sem.at[1]).start(priority=k % 2)
                return c

            lax.fori_loop(0, DISPATCH_TILE // SUBLANES, issue, 0)

        first += n_steps
    pltpu.make_async_copy(dst_hbm.at[pl.ds(0, n_slots), :], dst_hbm.at[pl.ds(0, n_slots), :], sem.at[1]).wait()


def _dispatch(slots, ztile, n_used, sources, n_rows):
    steps = tuple(h.shape[0] // DISPATCH_TILE for h in sources)
    any_spec = pl.BlockSpec(memory_space=pl.ANY)
    smem_spec = pl.BlockSpec(memory_space=pltpu.SMEM)
    src_specs = []
    first = 0
    for n_steps in steps:
        src_specs.append(pl.BlockSpec(
            (DISPATCH_TILE // SUBLANES, SUBLANES, ROW_WORDS),
            lambda i, first=first, n_steps=n_steps: (jnp.clip(i - first, 0, n_steps - 1), 0, 0)))
        first += n_steps
    sources = [h.reshape(h.shape[0] // SUBLANES, SUBLANES, ROW_WORDS) for h in sources]
    return pl.pallas_call(
        functools.partial(_dispatch_kernel, steps=steps),
        grid=(sum(steps),),
        in_specs=[any_spec, smem_spec, smem_spec] + src_specs,
        out_specs=any_spec,
        out_shape=jax.ShapeDtypeStruct((n_rows, ROW_WORDS), jnp.uint32),
        scratch_shapes=[
            pltpu.SMEM((DISPATCH_TILE * TOP_K,), jnp.int32),
            pltpu.VMEM((GROUP_TILE, ROW_WORDS), jnp.uint32),
            pltpu.SemaphoreType.DMA((2,)),
        ],
        compiler_params=_params(("arbitrary",), 32),
        name="dispatch",
    )(slots, ztile, n_used, *sources)


EXPERT_CHUNK = 512


def _expert_kernel(te_ref, ts_ref, nu_ref, x_ref, wgu_ref, bgu_ref, wd_ref, bd_ref, y_ref):
    j = pl.program_id(0)

    @pl.when(j < nu_ref[0])
    def _():
        half = D_MODEL // 2
        lo, hi = _unpack_bf16_pair(x_ref[...])
        lo, hi = lo.astype(BF16), hi.astype(BF16)
        y = jnp.broadcast_to(bd_ref[0], (GROUP_TILE, D_MODEL))
        for c in range(D_FF // EXPERT_CHUNK):
            gcols = slice(c * EXPERT_CHUNK, (c + 1) * EXPERT_CHUNK)
            ucols = slice(D_FF + c * EXPERT_CHUNK, D_FF + (c + 1) * EXPERT_CHUNK)
            gate = (jnp.dot(lo, wgu_ref[0, :half, gcols], preferred_element_type=F32)
                    + jnp.dot(hi, wgu_ref[0, half:, gcols], preferred_element_type=F32) + bgu_ref[0, :, gcols])
            up = (jnp.dot(lo, wgu_ref[0, :half, ucols], preferred_element_type=F32)
                  + jnp.dot(hi, wgu_ref[0, half:, ucols], preferred_element_type=F32) + bgu_ref[0, :, ucols])
            gate = jnp.minimum(gate, SWIGLU_LIMIT)
            up = jnp.clip(up, -SWIGLU_LIMIT, SWIGLU_LIMIT)
            hid = (up + 1.0) * gate * _sigmoid(SWIGLU_ALPHA * gate)
            y = y + jnp.dot(hid.astype(BF16), wd_ref[0, gcols, :], preferred_element_type=F32)
        y_ref[...] = _pack_bf16_pair(y[:, :half], y[:, half:])

    @pl.when(j >= nu_ref[0])
    def _():
        y_ref[...] = jnp.zeros_like(y_ref)


def _experts(tile_expert, tile_src, n_used, xs, wgu, bgu, wd, bd):
    n_rows = xs.shape[0]
    row_spec = pl.BlockSpec((GROUP_TILE, ROW_WORDS), lambda j, te, ts, nu: (ts[j], 0))
    out_spec = pl.BlockSpec((GROUP_TILE, ROW_WORDS), lambda j, te, ts, nu: (j, 0))
    return pl.pallas_call(
        _expert_kernel,
        grid_spec=pltpu.PrefetchScalarGridSpec(
            num_scalar_prefetch=3,
            grid=(n_rows // GROUP_TILE,),
            in_specs=[
                row_spec,
                pl.BlockSpec((1, D_MODEL, 2 * D_FF), lambda j, te, ts, nu: (te[j], 0, 0)),
                pl.BlockSpec((1, 1, 2 * D_FF), lambda j, te, ts, nu: (te[j], 0, 0)),
                pl.BlockSpec((1, D_FF, D_MODEL), lambda j, te, ts, nu: (te[j], 0, 0)),
                pl.BlockSpec((1, 1, D_MODEL), lambda j, te, ts, nu: (te[j], 0, 0)),
            ],
            out_specs=out_spec,
        ),
        out_shape=jax.ShapeDtypeStruct((n_rows, ROW_WORDS), jnp.uint32),
        compiler_params=_params(("arbitrary",), 56),
        name="experts",
    )(tile_expert, tile_src, n_used, xs, wgu, bgu, wd, bd)


COMBINE_TILE = 512


def _combine_kernel(slot_hbm, ys_hbm, x1_ref, wt_ref, nw_ref, o_ref, slot_a, slot_b, buf_a, buf_b, sem):
    i = pl.program_id(0)
    n = pl.num_programs(0)
    n_slots = COMBINE_TILE * TOP_K
    slot_bufs = (slot_a, slot_b)
    row_bufs = (buf_a, buf_b)

    def slot_load(tile, par):
        return pltpu.make_async_copy(slot_hbm.at[pl.ds(tile * n_slots, n_slots)], slot_bufs[par], sem.at[par])

    def issue_rows(par):
        def issue(t8, c):
            for r in range(SUBLANES):
                for k in range(TOP_K):
                    slot = slot_bufs[par][(t8 * SUBLANES + r) * TOP_K + k]
                    pltpu.make_async_copy(ys_hbm.at[pl.ds(slot, 1), :], row_bufs[par].at[k, t8, pl.ds(r, 1), :],
                                          sem.at[2 + par]).start(priority=k % 2)
            return c

        lax.fori_loop(0, COMBINE_TILE // SUBLANES, issue, 0)

    def reduce_rows(par):
        pltpu.make_async_copy(ys_hbm.at[pl.ds(0, n_slots), :], ys_hbm.at[pl.ds(0, n_slots), :],
                              sem.at[2 + par]).wait()
        half = D_MODEL // 2
        wt = wt_ref[...]
        acc_lo = x1_ref[:, :half]
        acc_hi = x1_ref[:, half:]
        for k in range(TOP_K):
            lo, hi = _unpack_bf16_pair(row_bufs[par][k].reshape(COMBINE_TILE, ROW_WORDS))
            acc_lo = acc_lo + wt[:, k:k + 1] * lo
            acc_hi = acc_hi + wt[:, k:k + 1] * hi
        ms = (jnp.sum(acc_lo * acc_lo, axis=-1, keepdims=True)
              + jnp.sum(acc_hi * acc_hi, axis=-1, keepdims=True)) * (1.0 / D_MODEL)
        inv = lax.rsqrt(ms + RMS_EPS)
        o_ref[:, :half] = acc_lo * inv * nw_ref[:, :half]
        o_ref[:, half:] = acc_hi * inv * nw_ref[:, half:]

    @pl.when(i == 0)
    def _():
        first = slot_load(0, 0)
        first.start()
        first.wait()
        issue_rows(0)

        @pl.when(n > 1)
        def _():
            slot_load(1, 1).start()

    for par in range(2):
        @pl.when(i % 2 == par)
        def _(par=par):
            @pl.when(i + 1 < n)
            def _():
                slot_load(i + 1, 1 - par).wait()
                issue_rows(1 - par)

            @pl.when(i + 2 < n)
            def _():
                slot_load(i + 2, par).start()

            reduce_rows(par)


def _combine(slots, ys, x1, wt, nw):
    t = x1.shape[0]
    tm = COMBINE_TILE
    return pl.pallas_call(
        _combine_kernel,
        grid=(t // tm,),
        in_specs=[
            pl.BlockSpec(memory_space=pl.ANY),
            pl.BlockSpec(memory_space=pl.ANY),
            pl.BlockSpec((tm, D_MODEL), lambda i: (i, 0)),
            pl.BlockSpec((tm, TOP_K), lambda i: (i, 0)),
            pl.BlockSpec((1, D_MODEL), lambda i: (0, 0)),
        ],
        out_specs=pl.BlockSpec((tm, D_MODEL), lambda i: (i, 0)),
        out_shape=jax.ShapeDtypeStruct((t, D_MODEL), F32),
        scratch_shapes=[
            pltpu.SMEM((tm * TOP_K,), jnp.int32),
            pltpu.SMEM((tm * TOP_K,), jnp.int32),
            pltpu.VMEM((TOP_K, tm // SUBLANES, SUBLANES, ROW_WORDS), jnp.uint32),
            pltpu.VMEM((TOP_K, tm // SUBLANES, SUBLANES, ROW_WORDS), jnp.uint32),
            pltpu.SemaphoreType.DMA((4,)),
        ],
        compiler_params=_params(("arbitrary",), 32),
        name="combine",
    )(slots, ys, x1, wt, nw)


def _moe(parts, p, final_w):
    sizes = [x1.shape[0] for x1, _, _, _ in parts]
    idx = jnp.concatenate([q[2] for q in parts], axis=1)
    t_all = idx.shape[1]
    pos, cnt = _route(idx)
    counts = cnt[:, 0].astype(jnp.int32)
    tiles_e = (counts + GROUP_TILE - 1) // GROUP_TILE
    tile_end = jnp.cumsum(tiles_e)
    tile_start = tile_end - tiles_e
    n_used = tile_end[-1]
    n_tiles = (t_all * TOP_K) // GROUP_TILE + N_EXPERTS
    n_rows = n_tiles * GROUP_TILE
    first_row = tile_start * GROUP_TILE
    experts = jnp.arange(N_EXPERTS, dtype=jnp.int32)[:, None, None]
    base = jnp.sum(jnp.where(idx[None] == experts, first_row[:, None, None], 0), axis=0)
    slots = (base + pos).T.reshape(-1)
    tile_id = jnp.arange(n_tiles, dtype=jnp.int32)
    tile_src = jnp.minimum(tile_id, n_used - 1).astype(jnp.int32)
    tile_expert = jnp.sum(tile_src[:, None] >= tile_end[None, :], axis=1).astype(jnp.int32)
    tile_expert = jnp.minimum(tile_expert, N_EXPERTS - 1)
    ztile = jnp.where(counts > 0, tile_end - 1, -1).astype(jnp.int32)

    n_used = n_used.reshape(1).astype(jnp.int32)
    xs = _dispatch(slots, ztile, n_used, [q[1] for q in parts], n_rows)
    ys = _experts(tile_expert, tile_src, n_used, xs, p["wgu"], p["bgu"], p["wd"], p["bd"])
    outs = []
    off = 0
    for (x1, _, _, wt), t in zip(parts, sizes):
        outs.append(_combine(lax.slice(slots, (off * TOP_K,), ((off + t) * TOP_K,)), ys, x1, wt.T, final_w))
        off += t
    return outs


def _rope_tables(seq):
    half = ATTN_HEAD_DIM // 2
    inv_freq = ROPE_THETA ** (-jnp.arange(half, dtype=F32) / half)
    ang = jnp.arange(seq, dtype=F32)[:, None] * inv_freq[None, :]
    cos, sin = jnp.cos(ang), jnp.sin(ang)
    reps = LANES // ATTN_HEAD_DIM
    cos_t = jnp.tile(jnp.concatenate([cos, cos], axis=1), (1, reps))
    sin_t = jnp.tile(jnp.concatenate([-sin, sin], axis=1), (1, reps))
    return cos_t, sin_t


def _pack_layer(w_in, conv_w, conv_b, gate_b, wpa, wpm, wo, nmw, nfw, rw, rb, wgu, bgu, wd, bd):
    c0 = ATTN_QKV_COLS
    c1 = c0 + ML_QKV_COLS
    c2 = c1 + ML_O_COLS
    c3 = c2 + ML_GATE_COLS
    w_attn = w_in[:, :c0].reshape(D_MODEL, N_GROUPS, 3, ATTN_DIM)
    w_attn = w_attn * jnp.array([ATTN_HEAD_DIM ** -0.5, 1.0, 1.0], F32)[None, None, :, None]
    w_main = jnp.concatenate([w_in[:, c0:c2], w_in[:, c3:], w_attn.reshape(D_MODEL, c0)], axis=1).astype(BF16)
    w_gate = jnp.pad(w_in[:, c2:c3], ((0, 0), (0, LANES - ML_GATE_COLS))).astype(BF16)
    return dict(
        w_main=w_main, w_gate=w_gate, conv_w=conv_w, conv_b=conv_b.reshape(1, -1),
        gate_b=gate_b.reshape(-1, 1), wpa=wpa.astype(BF16), wpm=wpm.astype(BF16), wo=wo.astype(BF16),
        nmw=nmw.reshape(1, -1), nfw=nfw.reshape(1, -1), rw_t=rw.T, rb=rb.reshape(-1, 1),
        wgu=wgu.astype(BF16), bgu=bgu.reshape(N_EXPERTS, 1, -1), wd=wd.astype(BF16),
        bd=bd.reshape(N_EXPERTS, 1, -1))


def _layer(x, p, tables):
    b, s, _ = x.shape
    x2d = x.reshape(b * s, D_MODEL)
    proj, gates, qkv1, qkv2 = _inproj(x2d, p["nmw"], p["w_main"], p["w_gate"], tables[0], tables[1], s)
    proj3 = proj.reshape(b, s, N_MAIN)

    merged = _attn_plain(proj3)
    for g, qkv in ((1, qkv1), (2, qkv2)):
        merged = _attn_dilated(qkv, g, ATTN_PATTERNS[g][1], merged, b, s)
    attn_o = merged[0]

    ml_q, ml_kt = _conv_qk(proj3, p["conv_w"], p["conv_b"])
    gates_t = gates[:, :ML_GATE_COLS].reshape(b, s, ML_GATE_COLS).transpose(0, 2, 1)
    g_rows = _gate_prep(gates_t, p["gate_b"])
    g_cols = g_rows.transpose(0, 2, 1)
    h_f, h_b = _mlstm(proj3, ml_q, ml_kt, g_rows, g_cols)

    return _merge(x2d, attn_o, h_f.reshape(b * s, ML_DIM), h_b.reshape(b * s, ML_DIM), proj,
                  p["wpa"], p["wpm"], p["wo"], p["nfw"], p["rw_t"], p["rb"])


def kernel(x_prompt, x_sample, norm_mix_w, w_in, mlstm_conv_w, mlstm_conv_b, mlstm_gate_b, w_proj_attn,
           w_proj_mlstm, w_out, norm_ffn_w, router_w, router_b, expert_w_gu, expert_b_gu, expert_w_down,
           expert_b_down, norm_final_w):
    depth = w_in.shape[0]
    assert depth == 1, "the final RMSNorm is fused into the last layer's MoE kernel"
    p = _pack_layer(w_in[0], mlstm_conv_w[0], mlstm_conv_b[0], mlstm_gate_b[0], w_proj_attn[0],
                    w_proj_mlstm[0], w_out[0], norm_mix_w[0], norm_ffn_w[0], router_w[0], router_b[0],
                    expert_w_gu[0], expert_b_gu[0], expert_w_down[0], expert_b_down[0])
    final_w = norm_final_w.reshape(1, -1)
    xs = (x_prompt, x_sample)
    parts = [_layer(x, p, _rope_tables(x.shape[1])) for x in xs]
    outs = _moe(parts, p, final_w)
    return tuple(o.reshape(x.shape) for o, x in zip(outs, xs))
```
